```python
import functools
import jax, jax.numpy as jnp
from jax import lax
import numpy as np

D_MODEL = 1024
BATCH = 8
SEQ = 8192
DEPTH = 1
DEC_BATCH = 32
DEC_SEQ = 16
PAST_LEN = 2048

CHUNK = 64
LEFT_CHUNKS = 8
ATT_WINDOW = LEFT_CHUNKS * CHUNK
BAND = (LEFT_CHUNKS + 1) * CHUNK
N_HEADS = 8
HEAD_DIM = 64
D_ATT = N_HEADS * HEAD_DIM
D_CONV = D_MODEL // 2
CONV_WIDTH = 3
MAX_REL = 128
D_FF = 2816
D_PLE = 256
N_NORMS = 7
EPS = 1e-6
NEG_INF = -1e30
SPLITS = (D_ATT, 2 * D_ATT, 3 * D_ATT, 3 * D_ATT + D_CONV, 3 * D_ATT + 2 * D_CONV,
          3 * D_ATT + 3 * D_CONV, 3 * D_ATT + 3 * D_CONV + D_MODEL)
D_IN = 3 * D_ATT + 3 * D_CONV + 2 * D_MODEL

kernel_name = "streaming_hybrid_chunkattn_shortconv_step"


def rmsnorm(x, g):
    xf = x.astype(jnp.float32)
    y = xf * lax.rsqrt(jnp.mean(xf * xf, axis=-1, keepdims=True) + EPS) * g.astype(jnp.float32)
    return y.astype(x.dtype)


def swiglu(x, w_gate, w_up, w_down):
    return (jax.nn.silu(x @ w_gate) * (x @ w_up)) @ w_down


def rel_bias_lookup(table, d):
    idx = jnp.clip(d, -MAX_REL, MAX_REL) + MAX_REL
    return table[:, idx].astype(jnp.float32)


def chunk_band_attention(q, k, v, table):
    S = q.shape[1]
    nc = S // CHUNK
    n = jnp.arange(BAND)
    i = jnp.arange(CHUNK)
    bias = rel_bias_lookup(table, i[:, None] + ATT_WINDOW - n[None, :])
    c = jnp.arange(nc)
    valid = (c[:, None] * CHUNK - ATT_WINDOW + n[None, :]) >= 0
    scale = HEAD_DIM ** -0.5
    pad = ((ATT_WINDOW, 0), (0, 0), (0, 0))

    def one_seq(args):
        qs, ks, vs = args
        kp = jnp.pad(ks, pad).reshape(nc + LEFT_CHUNKS, CHUNK, N_HEADS, HEAD_DIM)
        vp = jnp.pad(vs, pad).reshape(nc + LEFT_CHUNKS, CHUNK, N_HEADS, HEAD_DIM)
        kb = jnp.concatenate([kp[j:j + nc] for j in range(LEFT_CHUNKS + 1)], axis=1)
        vb = jnp.concatenate([vp[j:j + nc] for j in range(LEFT_CHUNKS + 1)], axis=1)
        qc = qs.reshape(nc, CHUNK, N_HEADS, HEAD_DIM)
        s = jnp.einsum('cqhd,ckhd->chqk', qc, kb).astype(jnp.float32) * scale + bias[None]
        s = jnp.where(valid[:, None, None, :], s, NEG_INF)
        p = jax.nn.softmax(s, axis=-1).astype(vs.dtype)
        o = jnp.einsum('chqk,ckhd->cqhd', p, vb)
        return o.reshape(S, D_ATT)

    return lax.map(one_seq, (q, k, v))


def cached_band_attention(q, k, v, k_cache, v_cache, table):
    Lc = k_cache.shape[1]
    T = q.shape[1]
    kk = jnp.concatenate([k_cache.astype(k.dtype), k], axis=1)
    vv = jnp.concatenate([v_cache.astype(v.dtype), v], axis=1)
    d = (Lc + jnp.arange(T))[:, None] - jnp.arange(Lc + T)[None, :]
    bias = rel_bias_lookup(table, d)
    s = jnp.einsum('bqhd,bkhd->bhqk', q, kk).astype(jnp.float32) * (HEAD_DIM ** -0.5) + bias[None]
    p = jax.nn.softmax(s, axis=-1).astype(vv.dtype)
    o = jnp.einsum('bhqk,bkhd->bqhd', p, vv)
    return o.reshape(q.shape[0], T, D_ATT)


def short_conv(s, prefix, w):
    T = s.shape[1]
    sf = jnp.concatenate([prefix.astype(s.dtype), s], axis=1)
    y = sum(sf[:, j:j + T] * w[j] for j in range(CONV_WIDTH))
    return y, sf[:, -(CONV_WIDTH - 1):]


def layer_forward(h, p_l, conv_prefix, attend, norm_g, w1_gate, w1_up, w1_down, w_in, conv_w,
                  w_att_out, w_conv_out, w_out, w2_gate, w2_up, w2_down, w_ple_gate, w_ple_proj):
    Bn, T, _ = h.shape
    h = h + 0.5 * rmsnorm(swiglu(rmsnorm(h, norm_g[0]), w1_gate, w1_up, w1_down), norm_g[1])
    u = rmsnorm(h, norm_g[2])
    z = u @ w_in
    q, k, v, x_in, b_gate, c_gate, g_att, g_conv = jnp.split(z, SPLITS, axis=-1)
    q = q.reshape(Bn, T, N_HEADS, HEAD_DIM)
    k = k.reshape(Bn, T, N_HEADS, HEAD_DIM)
    v = v.reshape(Bn, T, N_HEADS, HEAD_DIM)
    o_att = attend(q, k, v)
    conv_y, conv_state = short_conv(c_gate * x_in, conv_prefix, conv_w)
    y_conv = b_gate * conv_y
    m = jax.nn.sigmoid(g_att) * (o_att @ w_att_out) + jax.nn.sigmoid(g_conv) * (y_conv @ w_conv_out)
    h = h + rmsnorm(m @ w_out, norm_g[3])
    h = h + 0.5 * rmsnorm(swiglu(rmsnorm(h, norm_g[4]), w2_gate, w2_up, w2_down), norm_g[5])
    h = h + rmsnorm(jax.nn.sigmoid(h @ w_ple_gate) * (p_l @ w_ple_proj), norm_g[6])
    return h, k, v, conv_state


def setup_inputs(seed: int = 0) -> dict:
    key = jax.random.key(seed)
    ks = jax.random.split(key, 24)
    f32 = jnp.float32
    att_cache = min(ATT_WINDOW, PAST_LEN)

    def w(k, shape, fan_in):
        return jax.random.normal(k, shape, f32) * (fan_in ** -0.5)

    return {
        "x_prompt": jax.random.normal(ks[0], (BATCH, SEQ, D_MODEL), f32),
        "x_sample": jax.random.normal(ks[1], (DEC_BATCH, DEC_SEQ, D_MODEL), f32),
        "cache_k": jax.random.normal(ks[2], (DEPTH, DEC_BATCH, att_cache, N_HEADS, HEAD_DIM), f32),
        "cache_v": jax.random.normal(ks[3], (DEPTH, DEC_BATCH, att_cache, N_HEADS, HEAD_DIM), f32),
        "cache_conv": jax.random.normal(ks[4], (DEPTH, DEC_BATCH, CONV_WIDTH - 1, D_CONV), f32),
        "p_prompt": jax.random.normal(ks[5], (DEPTH, BATCH, SEQ, D_PLE), f32),
        "p_sample": jax.random.normal(ks[6], (DEPTH, DEC_BATCH, DEC_SEQ, D_PLE), f32),
        "norm_g": 1.0 + 0.05 * jax.random.normal(ks[7], (DEPTH, N_NORMS, D_MODEL), f32),
        "w1_gate": w(ks[8], (DEPTH, D_MODEL, D_FF), D_MODEL),
        "w1_up": w(ks[9], (DEPTH, D_MODEL, D_FF), D_MODEL),
        "w1_down": w(ks[10], (DEPTH, D_FF, D_MODEL), D_FF),
        "w_in": w(ks[11], (DEPTH, D_MODEL, D_IN), D_MODEL),
        "conv_w": w(ks[12], (DEPTH, CONV_WIDTH, D_CONV), CONV_WIDTH),
        "rel_bias": 0.2 * jax.random.normal(ks[13], (DEPTH, N_HEADS, 2 * MAX_REL + 1), f32),
        "w_att_out": w(ks[14], (DEPTH, D_ATT, D_MODEL), D_ATT),
        "w_conv_out": w(ks[15], (DEPTH, D_CONV, D_MODEL), D_CONV),
        "w_out": w(ks[16], (DEPTH, D_MODEL, D_MODEL), D_MODEL),
        "w2_gate": w(ks[17], (DEPTH, D_MODEL, D_FF), D_MODEL),
        "w2_up": w(ks[18], (DEPTH, D_MODEL, D_FF), D_MODEL),
        "w2_down": w(ks[19], (DEPTH, D_FF, D_MODEL), D_FF),
        "w_ple_gate": w(ks[20], (DEPTH, D_MODEL, D_MODEL), D_MODEL),
        "w_ple_proj": w(ks[21], (DEPTH, D_PLE, D_MODEL), D_PLE),
    }


def reference(x_prompt, x_sample, cache_k, cache_v, cache_conv, p_prompt, p_sample, norm_g,
              w1_gate, w1_up, w1_down, w_in, conv_w, rel_bias, w_att_out, w_conv_out, w_out,
              w2_gate, w2_up, w2_down, w_ple_gate, w_ple_proj):
    hp, hs = x_prompt, x_sample
    kp_l, vp_l, cp_l, ks_l, vs_l, cs_l = [], [], [], [], [], []
    for l in range(DEPTH):
        weights = (norm_g[l], w1_gate[l], w1_up[l], w1_down[l], w_in[l], conv_w[l],
                   w_att_out[l], w_conv_out[l], w_out[l], w2_gate[l], w2_up[l], w2_down[l],
                   w_ple_gate[l], w_ple_proj[l])
        attend_p = functools.partial(chunk_band_attention, table=rel_bias[l])
        prefix_p = jnp.zeros((hp.shape[0], CONV_WIDTH - 1, D_CONV), hp.dtype)
        hp, k_p, v_p, c_p = layer_forward(hp, p_prompt[l], prefix_p, attend_p, *weights)
        keep = min(ATT_WINDOW, k_p.shape[1])
        kp_l.append(k_p[:, -keep:])
        vp_l.append(v_p[:, -keep:])
        cp_l.append(c_p)
        attend_s = functools.partial(cached_band_attention, k_cache=cache_k[l], v_cache=cache_v[l],
                                     table=rel_bias[l])
        hs, k_s, v_s, c_s = layer_forward(hs, p_sample[l], cache_conv[l], attend_s, *weights)
        ks_l.append(k_s)
        vs_l.append(v_s)
        cs_l.append(c_s)
    return (hp, hs, jnp.stack(kp_l), jnp.stack(vp_l), jnp.stack(cp_l),
            jnp.stack(ks_l), jnp.stack(vs_l), jnp.stack(cs_l))
```

```python
import functools

import jax
import jax.numpy as jnp
from jax import lax
from jax.experimental import pallas as pl
from jax.experimental.pallas import tpu as pltpu

F32 = jnp.float32
BF16 = jnp.bfloat16

D_MODEL = 1024
CHUNK = 64
LEFT_CHUNKS = 8
ATT_WINDOW = LEFT_CHUNKS * CHUNK
BAND = ATT_WINDOW + CHUNK
N_HEADS = 8
HEAD_DIM = 64
D_ATT = N_HEADS * HEAD_DIM
D_CONV = D_MODEL // 2
CONV_WIDTH = 3
MAX_REL = 128
D_FF = 2816
D_PLE = 256
EPS = 1e-6
NEG_INF = -1e30

FF_CHUNK = 256
N_FF_CHUNKS = D_FF // FF_CHUNK
PAIR = 2 * CHUNK
PAIR_KEYS = ATT_WINDOW + PAIR
TOKEN_TILE = 512
CONV_PAD = 8
VMEM_LIMIT_BYTES = 56 * 1024 * 1024


def _dot(a, b):
    return jnp.dot(a, b, preferred_element_type=F32)


def _dot_nt(a, b):
    return lax.dot_general(a, b, (((1,), (1,)), ((), ())), preferred_element_type=F32)


def _rmsnorm(x, g):
    return x * lax.rsqrt(jnp.mean(x * x, axis=-1, keepdims=True) + EPS) * g


def _sigmoid(x):
    return 1.0 / (1.0 + jnp.exp(-x))


def _resident(shape):
    return pl.BlockSpec(shape, lambda *_: (0,) * len(shape), pipeline_mode=pl.Buffered(1))


def _params(n_grid_dims):
    return pltpu.CompilerParams(
        dimension_semantics=("arbitrary",) * n_grid_dims,
        vmem_limit_bytes=VMEM_LIMIT_BYTES,
    )


def _ffn_body(x, g_pre, g_post, wgu_ref, wd_ref, act_scr):
    u = _rmsnorm(x, g_pre).astype(BF16)
    for c in range(N_FF_CHUNKS):
        r = _dot(u, wgu_ref[:, c * 2 * FF_CHUNK:(c + 1) * 2 * FF_CHUNK])
        gate = r[:, :FF_CHUNK]
        up = r[:, FF_CHUNK:]
        act_scr[:, c * FF_CHUNK:(c + 1) * FF_CHUNK] = (gate * _sigmoid(gate) * up).astype(BF16)
    y = _dot(act_scr[...], wd_ref[...])
    return x + 0.5 * _rmsnorm(y, g_post)


def _ffn_kernel(x_ref, g_ref, wgu_ref, wd_ref, o_ref, act_scr, *, pre, post):
    o_ref[...] = _ffn_body(x_ref[...], g_ref[pre:pre + 1, :], g_ref[post:post + 1, :],
                           wgu_ref, wd_ref, act_scr)


def _ffn_ple_kernel(x_ref, p_ref, g_ref, wgu_ref, wd_ref, wpg_ref, wpp_ref, o_ref, act_scr,
                    *, pre, post, ple):
    h = _ffn_body(x_ref[...], g_ref[pre:pre + 1, :], g_ref[post:post + 1, :],
                  wgu_ref, wd_ref, act_scr)
    gate = _sigmoid(_dot(h.astype(BF16), wpg_ref[...]))
    proj = _dot(p_ref[...].astype(BF16), wpp_ref[...])
    o_ref[...] = h + _rmsnorm(gate * proj, g_ref[ple:ple + 1, :])


def _ffn_call(x, norm_g, wgu, wd, pre, post):
    n = x.shape[0]
    tm = min(TOKEN_TILE, n)
    row = lambda i: (i, 0)
    return pl.pallas_call(
        functools.partial(_ffn_kernel, pre=pre, post=post),
        grid=(n // tm,),
        in_specs=[pl.BlockSpec((tm, D_MODEL), row), _resident(norm_g.shape),
                  _resident(wgu.shape), _resident(wd.shape)],
        out_specs=pl.BlockSpec((tm, D_MODEL), row),
        out_shape=jax.ShapeDtypeStruct((n, D_MODEL), F32),
        scratch_shapes=[pltpu.VMEM((tm, D_FF), BF16)],
        compiler_params=_params(1),
        name="ffn",
    )(x, norm_g, wgu, wd)


def _ffn_ple_call(x, p, norm_g, wgu, wd, wpg, wpp, pre, post, ple):
    n = x.shape[0]
    tm = min(TOKEN_TILE, n)
    row = lambda i: (i, 0)
    return pl.pallas_call(
        functools.partial(_ffn_ple_kernel, pre=pre, post=post, ple=ple),
        grid=(n // tm,),
        in_specs=[pl.BlockSpec((tm, D_MODEL), row), pl.BlockSpec((tm, D_PLE), row),
                  _resident(norm_g.shape), _resident(wgu.shape), _resident(wd.shape),
                  _resident(wpg.shape), _resident(wpp.shape)],
        out_specs=pl.BlockSpec((tm, D_MODEL), row),
        out_shape=jax.ShapeDtypeStruct((n, D_MODEL), F32),
        scratch_shapes=[pltpu.VMEM((tm, D_FF), BF16)],
        compiler_params=_params(1),
        name="ffn_ple",
    )(x, p, norm_g, wgu, wd, wpg, wpp)


def _build_bias(tab_ref, head, dist, lo, hi):
    def body(j, acc):
        v = lo + j
        return jnp.where(dist == v, tab_ref[head, v + MAX_REL], acc)
    return lax.fori_loop(0, hi - lo + 1, body, jnp.full(dist.shape, NEG_INF, F32))


def _short_conv(s, bgate, convw_ref, conv_scr):
    t = s.shape[0]
    conv_scr[CONV_PAD:CONV_PAD + t, :] = s
    y = (convw_ref[0:1, :] * conv_scr[CONV_PAD - 2:CONV_PAD - 2 + t, :]
         + convw_ref[1:2, :] * conv_scr[CONV_PAD - 1:CONV_PAD - 1 + t, :]
         + convw_ref[2:3, :] * s)
    return bgate * y


def _mix_out(h1, o_att, y_conv, gates, g_post, wao_ref, wco_ref, wo_ref):
    m = (_sigmoid(gates[:, :D_MODEL]) * _dot(o_att.astype(BF16), wao_ref[...])
         + _sigmoid(gates[:, D_MODEL:]) * _dot(y_conv.astype(BF16), wco_ref[...]))
    return h1 + _rmsnorm(_dot(m.astype(BF16), wo_ref[...]), g_post)


def _mixer_prompt_kernel(tab_ref, h1_ref, g_ref, wq_ref, wkt_ref, wk_ref, wv_ref, wxbc_ref, wg_ref,
                         convw_ref, wao_ref, wco_ref, wo_ref,
                         h2_ref, kout_ref, vout_ref, cout_ref,
                         q_scr, kt_scr, v_scr, bias_scr, o_scr, conv_scr, *, tm, tiles_per_seq):
    b = pl.program_id(0)
    i = pl.program_id(1)
    n_pairs = tm // PAIR

    @pl.when((b == 0) & (i == 0))
    def _():
        rows = 32
        for rb in range(PAIR // rows):
            r = lax.broadcasted_iota(jnp.int32, (rows, PAIR_KEYS), 0) + rb * rows
            c = lax.broadcasted_iota(jnp.int32, (rows, PAIR_KEYS), 1)
            n = c - (r & CHUNK)
            d = jnp.minimum((r & (CHUNK - 1)) + ATT_WINDOW - n, MAX_REL)
            d = jnp.where((n >= 0) & (n < BAND), d, MAX_REL + 1)

            def per_head(h, carry):
                bias_scr[h, rb * rows:(rb + 1) * rows, :] = _build_bias(
                    tab_ref, h, d, -(CHUNK - 1), MAX_REL)
                return carry
            lax.fori_loop(0, N_HEADS, per_head, 0)

    @pl.when(i == 0)
    def _():
        kt_scr[:, :, :ATT_WINDOW] = jnp.zeros((N_HEADS, HEAD_DIM, ATT_WINDOW), BF16)
        v_scr[:, :ATT_WINDOW, :] = jnp.zeros((N_HEADS, ATT_WINDOW, HEAD_DIM), BF16)
        conv_scr[0:CONV_PAD, :] = jnp.zeros((CONV_PAD, D_CONV), F32)

    h1 = h1_ref[...]
    u = _rmsnorm(h1, g_ref[2:3, :]).astype(BF16)

    q = _dot(u, wq_ref[...]) * (HEAD_DIM ** -0.5)
    kt = _dot_nt(wkt_ref[...], u)
    v = _dot(u, wv_ref[...])
    for h in range(N_HEADS):
        hs = slice(h * HEAD_DIM, (h + 1) * HEAD_DIM)
        q_scr[h] = q[:, hs].astype(BF16)
        kt_scr[h, :, ATT_WINDOW:] = kt[hs, :].astype(BF16)
        v_scr[h, ATT_WINDOW:, :] = v[:, hs].astype(BF16)

    @pl.when(i == tiles_per_seq - 1)
    def _():
        kout_ref[0] = _dot(u, wk_ref[...])[tm - ATT_WINDOW:, :]
        vout_ref[0] = v[tm - ATT_WINDOW:, :]

    first_valid = jnp.where(i == 0, ATT_WINDOW, 0)
    col = lax.broadcasted_iota(jnp.int32, (1, PAIR_KEYS), 1)

    def per_head(h, carry):
        for jj in range(n_pairs):
            k0 = jj * PAIR
            neg = jnp.where(col + k0 < first_valid, NEG_INF, 0.0)
            s = _dot(q_scr[h, k0:k0 + PAIR, :], kt_scr[h, :, k0:k0 + PAIR_KEYS])
            s = s + bias_scr[h] + neg
            p = jnp.exp(s - jnp.max(s, axis=-1, keepdims=True))
            l = jnp.sum(p, axis=-1, keepdims=True)
            o = _dot(p.astype(BF16), v_scr[h, k0:k0 + PAIR_KEYS, :])
            o_scr[h, k0:k0 + PAIR, :] = o * (1.0 / l)
        return carry
    lax.fori_loop(0, N_HEADS, per_head, 0)
    o_att = jnp.concatenate([o_scr[h] for h in range(N_HEADS)], axis=-1)

    xbc = _dot(u, wxbc_ref[...])
    s_conv = xbc[:, 2 * D_CONV:] * xbc[:, :D_CONV]
    y_conv = _short_conv(s_conv, xbc[:, D_CONV:2 * D_CONV], convw_ref, conv_scr)

    @pl.when(i == tiles_per_seq - 1)
    def _():
        cout_ref[0] = s_conv[tm - 2:, :]

    gates = _dot(u, wg_ref[...])
    h2_ref[...] = _mix_out(h1, o_att, y_conv, gates, g_ref[3:4, :], wao_ref, wco_ref, wo_ref)

    conv_scr[CONV_PAD - 2:CONV_PAD, :] = s_conv[tm - 2:, :]
    kt_scr[:, :, :ATT_WINDOW] = kt_scr[:, :, tm:]
    v_scr[:, :ATT_WINDOW, :] = v_scr[:, tm:, :]


def _mixer_prompt_call(tab, h1, norm_g, wq, wkt, wk, wv, wxbc, wg, convw, wao, wco, wo, batch, seq):
    tm = TOKEN_TILE
    tiles = seq // tm
    row = lambda b, i: (b * tiles + i, 0)
    per_seq = lambda b, i: (b, 0, 0)
    weights = (wq, wkt, wk, wv, wxbc, wg, convw, wao, wco, wo)
    return pl.pallas_call(
        functools.partial(_mixer_prompt_kernel, tm=tm, tiles_per_seq=tiles),
        grid=(batch, tiles),
        in_specs=[pl.BlockSpec(memory_space=pltpu.SMEM),
                  pl.BlockSpec((tm, D_MODEL), row), _resident(norm_g.shape)]
                 + [_resident(w.shape) for w in weights],
        out_specs=[pl.BlockSpec((tm, D_MODEL), row),
                   pl.BlockSpec((1, ATT_WINDOW, D_ATT), per_seq),
                   pl.BlockSpec((1, ATT_WINDOW, D_ATT), per_seq),
                   pl.BlockSpec((1, CONV_WIDTH - 1, D_CONV), per_seq)],
        out_shape=[jax.ShapeDtypeStruct((batch * seq, D_MODEL), F32),
                   jax.ShapeDtypeStruct((batch, ATT_WINDOW, D_ATT), F32),
                   jax.ShapeDtypeStruct((batch, ATT_WINDOW, D_ATT), F32),
                   jax.ShapeDtypeStruct((batch, CONV_WIDTH - 1, D_CONV), F32)],
        scratch_shapes=[pltpu.VMEM((N_HEADS, tm, HEAD_DIM), BF16),
                        pltpu.VMEM((N_HEADS, HEAD_DIM, ATT_WINDOW + tm), BF16),
                        pltpu.VMEM((N_HEADS, ATT_WINDOW + tm, HEAD_DIM), BF16),
                        pltpu.VMEM((N_HEADS, PAIR, PAIR_KEYS), F32),
                        pltpu.VMEM((N_HEADS, tm, HEAD_DIM), F32),
                        pltpu.VMEM((CONV_PAD + tm, D_CONV), F32)],
        compiler_params=_params(2),
        name="mixer_prompt",
    )(tab, h1, norm_g, *weights)


def _inproj_kernel(h1_ref, g_ref, wq_ref, wk_ref, wv_ref, wxbc_ref, wg_ref,
                   q_ref, k_ref, v_ref, xbc_ref, gates_ref):
    u = _rmsnorm(h1_ref[...], g_ref[2:3, :]).astype(BF16)
    q_ref[...] = _dot(u, wq_ref[...]) * (HEAD_DIM ** -0.5)
    k_ref[...] = _dot(u, wk_ref[...])
    v_ref[...] = _dot(u, wv_ref[...])
    xbc_ref[...] = _dot(u, wxbc_ref[...])
    gates_ref[...] = _dot(u, wg_ref[...])


def _inproj_call(h1, norm_g, wq, wk, wv, wxbc, wg):
    n = h1.shape[0]
    ins = (h1, norm_g, wq, wk, wv, wxbc, wg)
    widths = (D_ATT, D_ATT, D_ATT, 3 * D_CONV, 2 * D_MODEL)
    return pl.pallas_call(
        _inproj_kernel,
        grid=(1,),
        in_specs=[_resident(a.shape) for a in ins],
        out_specs=[pl.BlockSpec((n, w), lambda i: (0, 0)) for w in widths],
        out_shape=[jax.ShapeDtypeStruct((n, w), F32) for w in widths],
        compiler_params=_params(1),
        name="inproj_sample",
    )(*ins)


def _attn_sample_kernel(tab_ref, q_ref, k_ref, v_ref, ck_ref, cv_ref, xbc_ref, cconv_ref, convw_ref,
                        o_ref, y_ref, cout_ref, bias_c_scr, bias_n_scr, conv_scr, *, t, lc):
    @pl.when(pl.program_id(0) == 0)
    def _():
        r = lax.broadcasted_iota(jnp.int32, (t, lc), 0)
        c = lax.broadcasted_iota(jnp.int32, (t, lc), 1)
        d_c = jnp.minimum(lc + r - c, MAX_REL)
        r = lax.broadcasted_iota(jnp.int32, (t, t), 0)
        c = lax.broadcasted_iota(jnp.int32, (t, t), 1)
        d_n = jnp.clip(r - c, -MAX_REL, MAX_REL)

        def per_head(h, carry):
            bias_c_scr[h] = _build_bias(tab_ref, h, d_c, 1, MAX_REL)
            bias_n_scr[h] = _build_bias(tab_ref, h, d_n, -min(t - 1, MAX_REL), min(t - 1, MAX_REL))
            return carry
        lax.fori_loop(0, N_HEADS, per_head, 0)

    q = q_ref[...]
    k = k_ref[...]
    v = v_ref[...]
    outs = []
    for h in range(N_HEADS):
        hs = slice(h * HEAD_DIM, (h + 1) * HEAD_DIM)
        qh = q[:, hs].astype(BF16)
        s_c = _dot_nt(qh, ck_ref[0, :, hs].astype(BF16)) + bias_c_scr[h]
        s_n = _dot_nt(qh, k[:, hs].astype(BF16)) + bias_n_scr[h]
        m = jnp.maximum(jnp.max(s_c, axis=-1, keepdims=True), jnp.max(s_n, axis=-1, keepdims=True))
        p_c = jnp.exp(s_c - m)
        p_n = jnp.exp(s_n - m)
        l = jnp.sum(p_c, axis=-1, keepdims=True) + jnp.sum(p_n, axis=-1, keepdims=True)
        o = (_dot(p_c.astype(BF16), cv_ref[0, :, hs].astype(BF16))
             + _dot(p_n.astype(BF16), v[:, hs].astype(BF16)))
        outs.append(o * (1.0 / l))
    o_ref[...] = jnp.concatenate(outs, axis=-1)

    xbc = xbc_ref[...]
    s_conv = xbc[:, 2 * D_CONV:] * xbc[:, :D_CONV]
    conv_scr[0:CONV_PAD, :] = jnp.zeros((CONV_PAD, D_CONV), F32)
    conv_scr[CONV_PAD - 2:CONV_PAD, :] = cconv_ref[0]
    y_ref[...] = _short_conv(s_conv, xbc[:, D_CONV:2 * D_CONV], convw_ref, conv_scr)
    cout_ref[0] = conv_scr[CONV_PAD + t - 2:CONV_PAD + t, :]


def _attn_sample_call(tab, q, k, v, cache_k, cache_v, xbc, cache_conv, convw, batch, t):
    lc = cache_k.shape[1]
    row = lambda b: (b, 0)
    per_seq = lambda b: (b, 0, 0)
    return pl.pallas_call(
        functools.partial(_attn_sample_kernel, t=t, lc=lc),
        grid=(batch,),
        in_specs=[pl.BlockSpec(memory_space=pltpu.SMEM),
                  pl.BlockSpec((t, D_ATT), row), pl.BlockSpec((t, D_ATT), row),
                  pl.BlockSpec((t, D_ATT), row),
                  pl.BlockSpec((1, lc, D_ATT), per_seq), pl.BlockSpec((1, lc, D_ATT), per_seq),
                  pl.BlockSpec((t, 3 * D_CONV), row),
                  pl.BlockSpec((1, CONV_WIDTH - 1, D_CONV), per_seq),
                  _resident(convw.shape)],
        out_specs=[pl.BlockSpec((t, D_ATT), row), pl.BlockSpec((t, D_CONV), row),
                   pl.BlockSpec((1, CONV_WIDTH - 1, D_CONV), per_seq)],
        out_shape=[jax.ShapeDtypeStruct((batch * t, D_ATT), F32),
                   jax.ShapeDtypeStruct((batch * t, D_CONV), F32),
                   jax.ShapeDtypeStruct((batch, CONV_WIDTH - 1, D_CONV), F32)],
        scratch_shapes=[pltpu.VMEM((N_HEADS, t, lc), F32),
                        pltpu.VMEM((N_HEADS, t, t), F32),
                        pltpu.VMEM((CONV_PAD + t, D_CONV), F32)],
        compiler_params=_params(1),
        name="attn_sample",
    )(tab, q, k, v, cache_k, cache_v, xbc, cache_conv, convw)


def _mix_kernel(h1_ref, o_ref, y_ref, gates_ref, g_ref, wao_ref, wco_ref, wo_ref, h2_ref):
    h2_ref[...] = _mix_out(h1_ref[...], o_ref[...], y_ref[...], gates_ref[...], g_ref[3:4, :],
                           wao_ref, wco_ref, wo_ref)


def _mix_call(h1, o_att, y_conv, gates, norm_g, wao, wco, wo):
    ins = (h1, o_att, y_conv, gates, norm_g, wao, wco, wo)
    return pl.pallas_call(
        _mix_kernel,
        grid=(1,),
        in_specs=[_resident(a.shape) for a in ins],
        out_specs=pl.BlockSpec(h1.shape, lambda i: (0, 0)),
        out_shape=jax.ShapeDtypeStruct(h1.shape, F32),
        compiler_params=_params(1),
        name="mix_sample",
    )(*ins)


def _interleave_gate_up(w_gate, w_up):
    d = w_gate.shape[0]
    g = w_gate.reshape(d, N_FF_CHUNKS, FF_CHUNK)
    u = w_up.reshape(d, N_FF_CHUNKS, FF_CHUNK)
    return jnp.concatenate([g, u], axis=-1).reshape(d, 2 * D_FF).astype(BF16)


def kernel(x_prompt, x_sample, cache_k, cache_v, cache_conv, p_prompt, p_sample, norm_g,
           w1_gate, w1_up, w1_down, w_in, conv_w, rel_bias, w_att_out, w_conv_out, w_out,
           w2_gate, w2_up, w2_down, w_ple_gate, w_ple_proj):
    depth = norm_g.shape[0]
    assert depth == 1, "one layer per step"
    batch, seq, _ = x_prompt.shape
    dec_batch, dec_seq, _ = x_sample.shape
    assert seq % TOKEN_TILE == 0 and TOKEN_TILE % PAIR == 0 and TOKEN_TILE >= ATT_WINDOW
    l = 0

    g = norm_g[l]
    w1gu = _interleave_gate_up(w1_gate[l], w1_up[l])
    w2gu = _interleave_gate_up(w2_gate[l], w2_up[l])
    w1d = w1_down[l].astype(BF16)
    w2d = w2_down[l].astype(BF16)
    win = w_in[l].astype(BF16)
    wq = win[:, :D_ATT]
    wk = win[:, D_ATT:2 * D_ATT]
    wv = win[:, 2 * D_ATT:3 * D_ATT]
    wxbc = win[:, 3 * D_ATT:3 * D_ATT + 3 * D_CONV]
    wg = win[:, 3 * D_ATT + 3 * D_CONV:]
    wkt = wk.T
    wao = w_att_out[l].astype(BF16)
    wco = w_conv_out[l].astype(BF16)
    wo = w_out[l].astype(BF16)
    wpg = w_ple_gate[l].astype(BF16)
    wpp = w_ple_proj[l].astype(BF16)
    tab = rel_bias[l]
    convw = conv_w[l]

    xp = x_prompt.reshape(batch * seq, D_MODEL)
    h1p = _ffn_call(xp, g, w1gu, w1d, 0, 1)
    h2p, k_p, v_p, c_p = _mixer_prompt_call(tab, h1p, g, wq, wkt, wk, wv, wxbc, wg, convw,
                                            wao, wco, wo, batch, seq)
    y_p = _ffn_ple_call(h2p, p_prompt[l].reshape(batch * seq, D_PLE), g, w2gu, w2d, wpg, wpp, 4, 5, 6)

    lc = cache_k.shape[2]
    xs = x_sample.reshape(dec_batch * dec_seq, D_MODEL)
    h1s = _ffn_call(xs, g, w1gu, w1d, 0, 1)
    q_s, k_s, v_s, xbc_s, gates_s = _inproj_call(h1s, g, wq, wk, wv, wxbc, wg)
    o_s, yc_s, c_s = _attn_sample_call(tab, q_s, k_s, v_s,
                                       cache_k[l].reshape(dec_batch, lc, D_ATT),
                                       cache_v[l].reshape(dec_batch, lc, D_ATT),
                                       xbc_s, cache_conv[l], convw, dec_batch, dec_seq)
    h2s = _mix_call(h1s, o_s, yc_s, gates_s, g, wao, wco, wo)
    y_s = _ffn_ple_call(h2s, p_sample[l].reshape(dec_batch * dec_seq, D_PLE), g, w2gu, w2d,
                        wpg, wpp, 4, 5, 6)

    kv_p = (1, batch, ATT_WINDOW, N_HEADS, HEAD_DIM)
    kv_s = (1, dec_batch, dec_seq, N_HEADS, HEAD_DIM)
    return (y_p.reshape(batch, seq, D_MODEL), y_s.reshape(dec_batch, dec_seq, D_MODEL),
            k_p.reshape(kv_p), v_p.reshape(kv_p), c_p[None],
            k_s.reshape(kv_s), v_s.reshape(kv_s), c_s[None])
```

```python
import functools

import jax
import jax.numpy as jnp
from jax import lax
from jax.experimental import pallas as pl
from jax.experimental.pallas import tpu as pltpu

F32 = jnp.float32
BF16 = jnp.bfloat16

D_MODEL = 1024
CHUNK = 64
LEFT_CHUNKS = 8
ATT_WINDOW = LEFT_CHUNKS * CHUNK
BAND = ATT_WINDOW + CHUNK
N_HEADS = 8
HEAD_DIM = 64
D_ATT = N_HEADS * HEAD_DIM
D_CONV = D_MODEL // 2
CONV_WIDTH = 3
MAX_REL = 128
D_FF = 2816
D_PLE = 256
EPS = 1e-6
NEG_INF = -1e30

FF_CHUNK = 256
N_FF_CHUNKS = D_FF // FF_CHUNK
PAIR = 2 * CHUNK
PAIR_KEYS = ATT_WINDOW + PAIR
HEADS_PER_STEP = 4
ATT_LOOKAHEAD = 3
TOKEN_TILE = 512
CONV_PAD = 8
VMEM_LIMIT_BYTES = 56 * 1024 * 1024


def _dot(a, b):
    return jnp.dot(a, b, preferred_element_type=F32)


def _dot_nt(a, b):
    return lax.dot_general(a, b, (((1,), (1,)), ((), ())), preferred_element_type=F32)


def _rmsnorm(x, g):
    return x * lax.rsqrt(jnp.mean(x * x, axis=-1, keepdims=True) + EPS) * g


def _sigmoid(x):
    return 1.0 / (1.0 + jnp.exp(-x))


def _resident(shape):
    return pl.BlockSpec(shape, lambda *_: (0,) * len(shape), pipeline_mode=pl.Buffered(1))


def _params(n_grid_dims):
    return pltpu.CompilerParams(
        dimension_semantics=("arbitrary",) * n_grid_dims,
        vmem_limit_bytes=VMEM_LIMIT_BYTES,
    )


def _ffn_body(x, g_pre, g_post, wgu_ref, wd_ref, act_scr):
    u = _rmsnorm(x, g_pre).astype(BF16)
    for c in range(N_FF_CHUNKS):
        r = _dot(u, wgu_ref[:, c * 2 * FF_CHUNK:(c + 1) * 2 * FF_CHUNK])
        gate = r[:, :FF_CHUNK]
        up = r[:, FF_CHUNK:]
        act_scr[:, c * FF_CHUNK:(c + 1) * FF_CHUNK] = (gate * _sigmoid(gate) * up).astype(BF16)
    y = _dot(act_scr[...], wd_ref[...])
    return x + 0.5 * _rmsnorm(y, g_post)


def _ffn_kernel(x_ref, g_ref, wgu_ref, wd_ref, o_ref, act_scr, *, pre, post):
    o_ref[...] = _ffn_body(x_ref[...], g_ref[pre:pre + 1, :], g_ref[post:post + 1, :],
                           wgu_ref, wd_ref, act_scr)


def _ffn_ple_kernel(x_ref, p_ref, g_ref, wgu_ref, wd_ref, wpg_ref, wpp_ref, o_ref, act_scr,
                    *, pre, post, ple):
    h = _ffn_body(x_ref[...], g_ref[pre:pre + 1, :], g_ref[post:post + 1, :],
                  wgu_ref, wd_ref, act_scr)
    gate = _sigmoid(_dot(h.astype(BF16), wpg_ref[...]))
    proj = _dot(p_ref[...].astype(BF16), wpp_ref[...])
    o_ref[...] = h + _rmsnorm(gate * proj, g_ref[ple:ple + 1, :])


def _ffn_call(x, norm_g, wgu, wd, pre, post):
    n = x.shape[0]
    tm = min(TOKEN_TILE, n)
    row = lambda i: (i, 0)
    return pl.pallas_call(
        functools.partial(_ffn_kernel, pre=pre, post=post),
        grid=(n // tm,),
        in_specs=[pl.BlockSpec((tm, D_MODEL), row), _resident(norm_g.shape),
                  _resident(wgu.shape), _resident(wd.shape)],
        out_specs=pl.BlockSpec((tm, D_MODEL), row),
        out_shape=jax.ShapeDtypeStruct((n, D_MODEL), F32),
        scratch_shapes=[pltpu.VMEM((tm, D_FF), BF16)],
        compiler_params=_params(1),
        name="ffn",
    )(x, norm_g, wgu, wd)


def _ffn_ple_call(x, p, norm_g, wgu, wd, wpg, wpp, pre, post, ple):
    n = x.shape[0]
    tm = min(TOKEN_TILE, n)
    row = lambda i: (i, 0)
    return pl.pallas_call(
        functools.partial(_ffn_ple_kernel, pre=pre, post=post, ple=ple),
        grid=(n // tm,),
        in_specs=[pl.BlockSpec((tm, D_MODEL), row), pl.BlockSpec((tm, D_PLE), row),
                  _resident(norm_g.shape), _resident(wgu.shape), _resident(wd.shape),
                  _resident(wpg.shape), _resident(wpp.shape)],
        out_specs=pl.BlockSpec((tm, D_MODEL), row),
        out_shape=jax.ShapeDtypeStruct((n, D_MODEL), F32),
        scratch_shapes=[pltpu.VMEM((tm, D_FF), BF16)],
        compiler_params=_params(1),
        name="ffn_ple",
    )(x, p, norm_g, wgu, wd, wpg, wpp)


def _build_bias(tab_ref, head, dist, lo, hi):
    def body(j, acc):
        v = lo + j
        return jnp.where(dist == v, tab_ref[head, v + MAX_REL], acc)
    return lax.fori_loop(0, hi - lo + 1, body, jnp.full(dist.shape, NEG_INF, F32))


def _short_conv(s, bgate, convw_ref, conv_scr):
    t = s.shape[0]
    conv_scr[CONV_PAD:CONV_PAD + t, :] = s
    y = (convw_ref[0:1, :] * conv_scr[CONV_PAD - 2:CONV_PAD - 2 + t, :]
         + convw_ref[1:2, :] * conv_scr[CONV_PAD - 1:CONV_PAD - 1 + t, :]
         + convw_ref[2:3, :] * s)
    return bgate * y


def _mix_out(h1, o_att, y_conv, gates, g_post, wao_ref, wco_ref, wo_ref):
    m = (_sigmoid(gates[:, :D_MODEL]) * _dot(o_att.astype(BF16), wao_ref[...])
         + _sigmoid(gates[:, D_MODEL:]) * _dot(y_conv.astype(BF16), wco_ref[...]))
    return h1 + _rmsnorm(_dot(m.astype(BF16), wo_ref[...]), g_post)


def _mixer_prompt_kernel(tab_ref, h1_ref, g_ref, wq_ref, wkt_ref, wk_ref, wv_ref, wxbc_ref, wg_ref,
                         convw_ref, wao_ref, wco_ref, wo_ref,
                         h2_ref, kout_ref, vout_ref, cout_ref,
                         q_scr, kt_scr, v_scr, bias_scr, s_scr, o_scr, conv_scr,
                         *, tm, tiles_per_seq):
    b = pl.program_id(0)
    i = pl.program_id(1)
    n_pairs = tm // PAIR

    @pl.when((b == 0) & (i == 0))
    def _():
        rows = 32
        for rb in range(PAIR // rows):
            r = lax.broadcasted_iota(jnp.int32, (rows, PAIR_KEYS), 0) + rb * rows
            c = lax.broadcasted_iota(jnp.int32, (rows, PAIR_KEYS), 1)
            n = c - (r & CHUNK)
            d = jnp.minimum((r & (CHUNK - 1)) + ATT_WINDOW - n, MAX_REL)
            d = jnp.where((n >= 0) & (n < BAND), d, MAX_REL + 1)

            def per_head(h, carry):
                bias_scr[h, rb * rows:(rb + 1) * rows, :] = _build_bias(
                    tab_ref, h, d, -(CHUNK - 1), MAX_REL)
                return carry
            lax.fori_loop(0, N_HEADS, per_head, 0)

    @pl.when(i == 0)
    def _():
        kt_scr[:, :, :ATT_WINDOW] = jnp.zeros((N_HEADS, HEAD_DIM, ATT_WINDOW), BF16)
        v_scr[:, :ATT_WINDOW, :] = jnp.zeros((N_HEADS, ATT_WINDOW, 2 * HEAD_DIM), BF16)
        conv_scr[0:CONV_PAD, :] = jnp.zeros((CONV_PAD, D_CONV), F32)

    h1 = h1_ref[...]
    u = _rmsnorm(h1, g_ref[2:3, :]).astype(BF16)

    q = _dot(u, wq_ref[...]) * (HEAD_DIM ** -0.5)
    kt = _dot_nt(wkt_ref[...], u)
    v = _dot(u, wv_ref[...])
    ones = jnp.ones((tm, HEAD_DIM), F32)
    for h in range(N_HEADS):
        hs = slice(h * HEAD_DIM, (h + 1) * HEAD_DIM)
        q_scr[h] = q[:, hs].astype(BF16)
        kt_scr[h, :, ATT_WINDOW:] = kt[hs, :].astype(BF16)
        v_scr[h, ATT_WINDOW:, :] = jnp.concatenate([v[:, hs], ones], axis=-1).astype(BF16)

    @pl.when(i == tiles_per_seq - 1)
    def _():
        kout_ref[0] = _dot(u, wk_ref[...])[tm - ATT_WINDOW:, :]
        vout_ref[0] = v[tm - ATT_WINDOW:, :]

    def attend(first_tile):
        col = lax.broadcasted_iota(jnp.int32, (1, PAIR_KEYS), 1)

        def scores(h, slot, jj):
            k0 = jj * PAIR
            s = _dot(q_scr[h, k0:k0 + PAIR, :], kt_scr[h, :, k0:k0 + PAIR_KEYS])
            s = s + bias_scr[h]
            if first_tile:
                s = s + jnp.where(col + k0 < ATT_WINDOW, NEG_INF, 0.0)
            s_scr[slot, jj] = s
            return jnp.max(s, axis=-1, keepdims=True)

        def weighted_values(h, slot, jj, m):
            k0 = jj * PAIR
            p = jnp.exp(s_scr[slot, jj] - m).astype(BF16)
            o = _dot(p, v_scr[h, k0:k0 + PAIR_KEYS, :])
            o_scr[h, k0:k0 + PAIR, :] = o[:, :HEAD_DIM] / o[:, HEAD_DIM:]

        def per_head_group(t, carry):
            steps = [(t * HEADS_PER_STEP + slot, slot, jj)
                     for slot in range(HEADS_PER_STEP) for jj in range(n_pairs)]
            maxes = {}
            for k in range(len(steps) + ATT_LOOKAHEAD):
                if k < len(steps):
                    maxes[k] = scores(*steps[k])
                if k >= ATT_LOOKAHEAD:
                    weighted_values(*steps[k - ATT_LOOKAHEAD], maxes.pop(k - ATT_LOOKAHEAD))
            return carry
        lax.fori_loop(0, N_HEADS // HEADS_PER_STEP, per_head_group, 0)

    @pl.when(i == 0)
    def _():
        attend(True)

    @pl.when(i > 0)
    def _():
        attend(False)
    o_att = jnp.concatenate([o_scr[h] for h in range(N_HEADS)], axis=-1)

    xbc = _dot(u, wxbc_ref[...])
    s_conv = xbc[:, 2 * D_CONV:] * xbc[:, :D_CONV]
    y_conv = _short_conv(s_conv, xbc[:, D_CONV:2 * D_CONV], convw_ref, conv_scr)

    @pl.when(i == tiles_per_seq - 1)
    def _():
        cout_ref[0] = s_conv[tm - 2:, :]

    gates = _dot(u, wg_ref[...])
    h2_ref[...] = _mix_out(h1, o_att, y_conv, gates, g_ref[3:4, :], wao_ref, wco_ref, wo_ref)

    conv_scr[CONV_PAD - 2:CONV_PAD, :] = s_conv[tm - 2:, :]
    kt_scr[:, :, :ATT_WINDOW] = kt_scr[:, :, tm:]
    v_scr[:, :ATT_WINDOW, :] = v_scr[:, tm:, :]


def _mixer_prompt_call(tab, h1, norm_g, wq, wkt, wk, wv, wxbc, wg, convw, wao, wco, wo, batch, seq):
    tm = TOKEN_TILE
    tiles = seq // tm
    row = lambda b, i: (b * tiles + i, 0)
    per_seq = lambda b, i: (b, 0, 0)
    weights = (wq, wkt, wk, wv, wxbc, wg, convw, wao, wco, wo)
    return pl.pallas_call(
        functools.partial(_mixer_prompt_kernel, tm=tm, tiles_per_seq=tiles),
        grid=(batch, tiles),
        in_specs=[pl.BlockSpec(memory_space=pltpu.SMEM),
                  pl.BlockSpec((tm, D_MODEL), row), _resident(norm_g.shape)]
                 + [_resident(w.shape) for w in weights],
        out_specs=[pl.BlockSpec((tm, D_MODEL), row),
                   pl.BlockSpec((1, ATT_WINDOW, D_ATT), per_seq),
                   pl.BlockSpec((1, ATT_WINDOW, D_ATT), per_seq),
                   pl.BlockSpec((1, CONV_WIDTH - 1, D_CONV), per_seq)],
        out_shape=[jax.ShapeDtypeStruct((batch * seq, D_MODEL), F32),
                   jax.ShapeDtypeStruct((batch, ATT_WINDOW, D_ATT), F32),
                   jax.ShapeDtypeStruct((batch, ATT_WINDOW, D_ATT), F32),
                   jax.ShapeDtypeStruct((batch, CONV_WIDTH - 1, D_CONV), F32)],
        scratch_shapes=[pltpu.VMEM((N_HEADS, tm, HEAD_DIM), BF16),
                        pltpu.VMEM((N_HEADS, HEAD_DIM, ATT_WINDOW + tm), BF16),
                        pltpu.VMEM((N_HEADS, ATT_WINDOW + tm, 2 * HEAD_DIM), BF16),
                        pltpu.VMEM((N_HEADS, PAIR, PAIR_KEYS), F32),
                        pltpu.VMEM((HEADS_PER_STEP, tm // PAIR, PAIR, PAIR_KEYS), F32),
                        pltpu.VMEM((N_HEADS, tm, HEAD_DIM), F32),
                        pltpu.VMEM((CONV_PAD + tm, D_CONV), F32)],
        compiler_params=_params(2),
        name="mixer_prompt",
    )(tab, h1, norm_g, *weights)


def _inproj_kernel(h1_ref, g_ref, wq_ref, wk_ref, wv_ref, wxbc_ref, wg_ref,
                   q_ref, k_ref, v_ref, xbc_ref, gates_ref):
    u = _rmsnorm(h1_ref[...], g_ref[2:3, :]).astype(BF16)
    q_ref[...] = _dot(u, wq_ref[...]) * (HEAD_DIM ** -0.5)
    k_ref[...] = _dot(u, wk_ref[...])
    v_ref[...] = _dot(u, wv_ref[...])
    xbc_ref[...] = _dot(u, wxbc_ref[...])
    gates_ref[...] = _dot(u, wg_ref[...])


def _inproj_call(h1, norm_g, wq, wk, wv, wxbc, wg):
    n = h1.shape[0]
    ins = (h1, norm_g, wq, wk, wv, wxbc, wg)
    widths = (D_ATT, D_ATT, D_ATT, 3 * D_CONV, 2 * D_MODEL)
    return pl.pallas_call(
        _inproj_kernel,
        grid=(1,),
        in_specs=[_resident(a.shape) for a in ins],
        out_specs=[pl.BlockSpec((n, w), lambda i: (0, 0)) for w in widths],
        out_shape=[jax.ShapeDtypeStruct((n, w), F32) for w in widths],
        compiler_params=_params(1),
        name="inproj_sample",
    )(*ins)


def _attn_sample_kernel(tab_ref, q_ref, k_ref, v_ref, ck_ref, cv_ref, xbc_ref, cconv_ref, convw_ref,
                        o_ref, y_ref, cout_ref, bias_c_scr, bias_n_scr, conv_scr, *, t, lc):
    @pl.when(pl.program_id(0) == 0)
    def _():
        r = lax.broadcasted_iota(jnp.int32, (t, lc), 0)
        c = lax.broadcasted_iota(jnp.int32, (t, lc), 1)
        d_c = jnp.minimum(lc + r - c, MAX_REL)
        r = lax.broadcasted_iota(jnp.int32, (t, t), 0)
        c = lax.broadcasted_iota(jnp.int32, (t, t), 1)
        d_n = jnp.clip(r - c, -MAX_REL, MAX_REL)

        def per_head(h, carry):
            bias_c_scr[h] = _build_bias(tab_ref, h, d_c, 1, MAX_REL)
            bias_n_scr[h] = _build_bias(tab_ref, h, d_n, -min(t - 1, MAX_REL), min(t - 1, MAX_REL))
            return carry
        lax.fori_loop(0, N_HEADS, per_head, 0)

    q = q_ref[...]
    k = k_ref[...]
    v = v_ref[...]
    outs = []
    for h in range(N_HEADS):
        hs = slice(h * HEAD_DIM, (h + 1) * HEAD_DIM)
        qh = q[:, hs].astype(BF16)
        s_c = _dot_nt(qh, ck_ref[0, :, hs].astype(BF16)) + bias_c_scr[h]
        s_n = _dot_nt(qh, k[:, hs].astype(BF16)) + bias_n_scr[h]
        m = jnp.maximum(jnp.max(s_c, axis=-1, keepdims=True), jnp.max(s_n, axis=-1, keepdims=True))
        p_c = jnp.exp(s_c - m)
        p_n = jnp.exp(s_n - m)
        l = jnp.sum(p_c, axis=-1, keepdims=True) + jnp.sum(p_n, axis=-1, keepdims=True)
        o = (_dot(p_c.astype(BF16), cv_ref[0, :, hs].astype(BF16))
             + _dot(p_n.astype(BF16), v[:, hs].astype(BF16)))
        outs.append(o * (1.0 / l))
    o_ref[...] = jnp.concatenate(outs, axis=-1)

    xbc = xbc_ref[...]
    s_conv = xbc[:, 2 * D_CONV:] * xbc[:, :D_CONV]
    conv_scr[0:CONV_PAD, :] = jnp.zeros((CONV_PAD, D_CONV), F32)
    conv_scr[CONV_PAD - 2:CONV_PAD, :] = cconv_ref[0]
    y_ref[...] = _short_conv(s_conv, xbc[:, D_CONV:2 * D_CONV], convw_ref, conv_scr)
    cout_ref[0] = conv_scr[CONV_PAD + t - 2:CONV_PAD + t, :]


def _attn_sample_call(tab, q, k, v, cache_k, cache_v, xbc, cache_conv, convw, batch, t):
    lc = cache_k.shape[1]
    row = lambda b: (b, 0)
    per_seq = lambda b: (b, 0, 0)
    return pl.pallas_call(
        functools.partial(_attn_sample_kernel, t=t, lc=lc),
        grid=(batch,),
        in_specs=[pl.BlockSpec(memory_space=pltpu.SMEM),
                  pl.BlockSpec((t, D_ATT), row), pl.BlockSpec((t, D_ATT), row),
                  pl.BlockSpec((t, D_ATT), row),
                  pl.BlockSpec((1, lc, D_ATT), per_seq), pl.BlockSpec((1, lc, D_ATT), per_seq),
                  pl.BlockSpec((t, 3 * D_CONV), row),
                  pl.BlockSpec((1, CONV_WIDTH - 1, D_CONV), per_seq),
                  _resident(convw.shape)],
        out_specs=[pl.BlockSpec((t, D_ATT), row), pl.BlockSpec((t, D_CONV), row),
                   pl.BlockSpec((1, CONV_WIDTH - 1, D_CONV), per_seq)],
        out_shape=[jax.ShapeDtypeStruct((batch * t, D_ATT), F32),
                   jax.ShapeDtypeStruct((batch * t, D_CONV), F32),
                   jax.ShapeDtypeStruct((batch, CONV_WIDTH - 1, D_CONV), F32)],
        scratch_shapes=[pltpu.VMEM((N_HEADS, t, lc), F32),
                        pltpu.VMEM((N_HEADS, t, t), F32),
                        pltpu.VMEM((CONV_PAD + t, D_CONV), F32)],
        compiler_params=_params(1),
        name="attn_sample",
    )(tab, q, k, v, cache_k, cache_v, xbc, cache_conv, convw)


def _mix_kernel(h1_ref, o_ref, y_ref, gates_ref, g_ref, wao_ref, wco_ref, wo_ref, h2_ref):
    h2_ref[...] = _mix_out(h1_ref[...], o_ref[...], y_ref[...], gates_ref[...], g_ref[3:4, :],
                           wao_ref, wco_ref, wo_ref)


def _mix_call(h1, o_att, y_conv, gates, norm_g, wao, wco, wo):
    ins = (h1, o_att, y_conv, gates, norm_g, wao, wco, wo)
    return pl.pallas_call(
        _mix_kernel,
        grid=(1,),
        in_specs=[_resident(a.shape) for a in ins],
        out_specs=pl.BlockSpec(h1.shape, lambda i: (0, 0)),
        out_shape=jax.ShapeDtypeStruct(h1.shape, F32),
        compiler_params=_params(1),
        name="mix_sample",
    )(*ins)


def _interleave_gate_up(w_gate, w_up):
    d = w_gate.shape[0]
    g = w_gate.reshape(d, N_FF_CHUNKS, FF_CHUNK)
    u = w_up.reshape(d, N_FF_CHUNKS, FF_CHUNK)
    return jnp.concatenate([g, u], axis=-1).reshape(d, 2 * D_FF).astype(BF16)


def kernel(x_prompt, x_sample, cache_k, cache_v, cache_conv, p_prompt, p_sample, norm_g,
           w1_gate, w1_up, w1_down, w_in, conv_w, rel_bias, w_att_out, w_conv_out, w_out,
           w2_gate, w2_up, w2_down, w_ple_gate, w_ple_proj):
    depth = norm_g.shape[0]
    assert depth == 1, "one layer per step"
    batch, seq, _ = x_prompt.shape
    dec_batch, dec_seq, _ = x_sample.shape
    assert seq % TOKEN_TILE == 0 and TOKEN_TILE % PAIR == 0 and TOKEN_TILE >= ATT_WINDOW
    l = 0

    g = norm_g[l]
    w1gu = _interleave_gate_up(w1_gate[l], w1_up[l])
    w2gu = _interleave_gate_up(w2_gate[l], w2_up[l])
    w1d = w1_down[l].astype(BF16)
    w2d = w2_down[l].astype(BF16)
    win = w_in[l].astype(BF16)
    wq = win[:, :D_ATT]
    wk = win[:, D_ATT:2 * D_ATT]
    wv = win[:, 2 * D_ATT:3 * D_ATT]
    wxbc = win[:, 3 * D_ATT:3 * D_ATT + 3 * D_CONV]
    wg = win[:, 3 * D_ATT + 3 * D_CONV:]
    wkt = wk.T
    wao = w_att_out[l].astype(BF16)
    wco = w_conv_out[l].astype(BF16)
    wo = w_out[l].astype(BF16)
    wpg = w_ple_gate[l].astype(BF16)
    wpp = w_ple_proj[l].astype(BF16)
    tab = rel_bias[l]
    convw = conv_w[l]

    xp = x_prompt.reshape(batch * seq, D_MODEL)
    h1p = _ffn_call(xp, g, w1gu, w1d, 0, 1)
    h2p, k_p, v_p, c_p = _mixer_prompt_call(tab, h1p, g, wq, wkt, wk, wv, wxbc, wg, convw,
                                            wao, wco, wo, batch, seq)
    y_p = _ffn_ple_call(h2p, p_prompt[l].reshape(batch * seq, D_PLE), g, w2gu, w2d, wpg, wpp, 4, 5, 6)

    lc = cache_k.shape[2]
    xs = x_sample.reshape(dec_batch * dec_seq, D_MODEL)
    h1s = _ffn_call(xs, g, w1gu, w1d, 0, 1)
    q_s, k_s, v_s, xbc_s, gates_s = _inproj_call(h1s, g, wq, wk, wv, wxbc, wg)
    o_s, yc_s, c_s = _attn_sample_call(tab, q_s, k_s, v_s,
                                       cache_k[l].reshape(dec_batch, lc, D_ATT),
                                       cache_v[l].reshape(dec_batch, lc, D_ATT),
                                       xbc_s, cache_conv[l], convw, dec_batch, dec_seq)
    h2s = _mix_call(h1s, o_s, yc_s, gates_s, g, wao, wco, wo)
    y_s = _ffn_ple_call(h2s, p_sample[l].reshape(dec_batch * dec_seq, D_PLE), g, w2gu, w2d,
                        wpg, wpp, 4, 5, 6)

    kv_p = (1, batch, ATT_WINDOW, N_HEADS, HEAD_DIM)
    kv_s = (1, dec_batch, dec_seq, N_HEADS, HEAD_DIM)
    return (y_p.reshape(batch, seq, D_MODEL), y_s.reshape(dec_batch, dec_seq, D_MODEL),
            k_p.reshape(kv_p), v_p.reshape(kv_p), c_p[None],
            k_s.reshape(kv_s), v_s.reshape(kv_s), c_s[None])
```

```python
import functools

import jax
import jax.numpy as jnp
from jax import lax
from jax.experimental import pallas as pl
from jax.experimental.pallas import tpu as pltpu

F32 = jnp.float32
BF16 = jnp.bfloat16

D_MODEL = 1024
CHUNK = 64
LEFT_CHUNKS = 8
ATT_WINDOW = LEFT_CHUNKS * CHUNK
BAND = ATT_WINDOW + CHUNK
N_HEADS = 8
HEAD_DIM = 64
D_ATT = N_HEADS * HEAD_DIM
D_CONV = D_MODEL // 2
CONV_WIDTH = 3
MAX_REL = 128
D_FF = 2816
D_PLE = 256
EPS = 1e-6
NEG_INF = -1e30

FF_CHUNK = 256
N_FF_CHUNKS = D_FF // FF_CHUNK
PAIR = 2 * CHUNK
PAIR_KEYS = ATT_WINDOW + PAIR
HEADS_PER_STEP = 8
ATT_LOOKAHEAD = 3
SCORE_SLOTS = ATT_LOOKAHEAD + 2
TOKEN_TILE = 512
CONV_PAD = 8
VMEM_LIMIT_BYTES = 56 * 1024 * 1024


def _dot(a, b):
    return jnp.dot(a, b, preferred_element_type=F32)


def _dot_nt(a, b):
    return lax.dot_general(a, b, (((1,), (1,)), ((), ())), preferred_element_type=F32)


def _rmsnorm(x, g):
    return x * lax.rsqrt(jnp.mean(x * x, axis=-1, keepdims=True) + EPS) * g


def _sigmoid(x):
    return 1.0 / (1.0 + jnp.exp(-x))


def _resident(shape):
    return pl.BlockSpec(shape, lambda *_: (0,) * len(shape), pipeline_mode=pl.Buffered(1))


def _params(n_grid_dims):
    return pltpu.CompilerParams(
        dimension_semantics=("arbitrary",) * n_grid_dims,
        vmem_limit_bytes=VMEM_LIMIT_BYTES,
    )


def _ffn_body(x, g_pre, g_post, wg_ref, wu_ref, wd_ref, act_scr):
    u = _rmsnorm(x, g_pre).astype(BF16)
    for c in range(N_FF_CHUNKS):
        cs = slice(c * FF_CHUNK, (c + 1) * FF_CHUNK)
        gate = _dot(u, wg_ref[:, cs])
        up = _dot(u, wu_ref[:, cs])
        act_scr[:, cs] = (gate * _sigmoid(gate) * up).astype(BF16)
    y = _dot(act_scr[...], wd_ref[...])
    return x + 0.5 * _rmsnorm(y, g_post)


def _ffn_kernel(x_ref, g_ref, wg_ref, wu_ref, wd_ref, o_ref, act_scr, *, pre, post):
    o_ref[...] = _ffn_body(x_ref[...], g_ref[pre:pre + 1, :], g_ref[post:post + 1, :],
                           wg_ref, wu_ref, wd_ref, act_scr)


def _ffn_ple_kernel(x_ref, p_ref, g_ref, wg_ref, wu_ref, wd_ref, wpg_ref, wpp_ref, o_ref, act_scr,
                    *, pre, post, ple):
    h = _ffn_body(x_ref[...], g_ref[pre:pre + 1, :], g_ref[post:post + 1, :],
                  wg_ref, wu_ref, wd_ref, act_scr)
    gate = _sigmoid(_dot(h.astype(BF16), wpg_ref[...]))
    proj = _dot(p_ref[...].astype(BF16), wpp_ref[...])
    o_ref[...] = h + _rmsnorm(gate * proj, g_ref[ple:ple + 1, :])


def _ffn_call(x, norm_g, wg, wu, wd, pre, post):
    n = x.shape[0]
    tm = min(TOKEN_TILE, n)
    row = lambda i: (i, 0)
    return pl.pallas_call(
        functools.partial(_ffn_kernel, pre=pre, post=post),
        grid=(n // tm,),
        in_specs=[pl.BlockSpec((tm, D_MODEL), row), _resident(norm_g.shape),
                  _resident(wg.shape), _resident(wu.shape), _resident(wd.shape)],
        out_specs=pl.BlockSpec((tm, D_MODEL), row),
        out_shape=jax.ShapeDtypeStruct((n, D_MODEL), F32),
        scratch_shapes=[pltpu.VMEM((tm, D_FF), BF16)],
        compiler_params=_params(1),
        name="ffn",
    )(x, norm_g, wg, wu, wd)


def _ffn_ple_call(x, p, norm_g, wg, wu, wd, wpg, wpp, pre, post, ple):
    n = x.shape[0]
    tm = min(TOKEN_TILE, n)
    row = lambda i: (i, 0)
    return pl.pallas_call(
        functools.partial(_ffn_ple_kernel, pre=pre, post=post, ple=ple),
        grid=(n // tm,),
        in_specs=[pl.BlockSpec((tm, D_MODEL), row), pl.BlockSpec((tm, D_PLE), row),
                  _resident(norm_g.shape), _resident(wg.shape), _resident(wu.shape),
                  _resident(wd.shape), _resident(wpg.shape), _resident(wpp.shape)],
        out_specs=pl.BlockSpec((tm, D_MODEL), row),
        out_shape=jax.ShapeDtypeStruct((n, D_MODEL), F32),
        scratch_shapes=[pltpu.VMEM((tm, D_FF), BF16)],
        compiler_params=_params(1),
        name="ffn_ple",
    )(x, p, norm_g, wg, wu, wd, wpg, wpp)


def _build_bias(tab_ref, head, dist, lo, hi):
    def body(j, acc):
        v = lo + j
        return jnp.where(dist == v, tab_ref[head, v + MAX_REL], acc)
    return lax.fori_loop(0, hi - lo + 1, body, jnp.full(dist.shape, NEG_INF, F32))


def _short_conv(s, bgate, convw_ref, conv_scr):
    t = s.shape[0]
    conv_scr[CONV_PAD:CONV_PAD + t, :] = s
    y = (convw_ref[0:1, :] * conv_scr[CONV_PAD - 2:CONV_PAD - 2 + t, :]
         + convw_ref[1:2, :] * conv_scr[CONV_PAD - 1:CONV_PAD - 1 + t, :]
         + convw_ref[2:3, :] * s)
    return bgate * y


def _mix_out(h1, o_att, y_conv, gates, g_post, wao_ref, wco_ref, wo_ref):
    m = (_sigmoid(gates[:, :D_MODEL]) * _dot(o_att.astype(BF16), wao_ref[...])
         + _sigmoid(gates[:, D_MODEL:]) * _dot(y_conv.astype(BF16), wco_ref[...]))
    return h1 + _rmsnorm(_dot(m.astype(BF16), wo_ref[...]), g_post)


def _mixer_prompt_kernel(tab_ref, h1_ref, g_ref, wq_ref, wkt_ref, wk_ref, wv_ref, wxbc_ref, wg_ref,
                         convw_ref, wao_ref, wco_ref, wo_ref,
                         h2_ref, kout_ref, vout_ref, cout_ref,
                         q_scr, kt_scr, v_scr, bias_scr, s_scr, o_scr, conv_scr, yc_scr, gate_scr,
                         *, tm, tiles_per_seq):
    b = pl.program_id(0)
    i = pl.program_id(1)
    n_pairs = tm // PAIR

    @pl.when((b == 0) & (i == 0))
    def _():
        rows = 32
        for rb in range(PAIR // rows):
            r = lax.broadcasted_iota(jnp.int32, (rows, PAIR_KEYS), 0) + rb * rows
            c = lax.broadcasted_iota(jnp.int32, (rows, PAIR_KEYS), 1)
            n = c - (r & CHUNK)
            d = jnp.minimum((r & (CHUNK - 1)) + ATT_WINDOW - n, MAX_REL)
            d = jnp.where((n >= 0) & (n < BAND), d, MAX_REL + 1)

            def per_head(h, carry):
                bias_scr[h, rb * rows:(rb + 1) * rows, :] = _build_bias(
                    tab_ref, h, d, -(CHUNK - 1), MAX_REL)
                return carry
            lax.fori_loop(0, N_HEADS, per_head, 0)

    @pl.when(i == 0)
    def _():
        kt_scr[:, :, :ATT_WINDOW] = jnp.zeros((N_HEADS, HEAD_DIM, ATT_WINDOW), BF16)
        v_scr[:, :ATT_WINDOW, :] = jnp.zeros((N_HEADS, ATT_WINDOW, 2 * HEAD_DIM), BF16)
        conv_scr[0:CONV_PAD, :] = jnp.zeros((CONV_PAD, D_CONV), F32)

    u = _rmsnorm(h1_ref[...], g_ref[2:3, :]).astype(BF16)

    q = _dot(u, wq_ref[...]) * (HEAD_DIM ** -0.5)
    kt = _dot_nt(wkt_ref[...], u)
    v = _dot(u, wv_ref[...])
    xbc = _dot(u, wxbc_ref[...])
    ones = jnp.ones((tm, HEAD_DIM), F32)
    for h in range(N_HEADS):
        hs = slice(h * HEAD_DIM, (h + 1) * HEAD_DIM)
        q_scr[h] = q[:, hs].astype(BF16)
        kt_scr[h, :, ATT_WINDOW:] = kt[hs, :].astype(BF16)
        v_scr[h, ATT_WINDOW:, :] = jnp.concatenate([v[:, hs], ones], axis=-1).astype(BF16)

    gates = _dot(u, wg_ref[...])
    s_conv = xbc[:, 2 * D_CONV:] * xbc[:, :D_CONV]
    yc_scr[...] = _short_conv(s_conv, xbc[:, D_CONV:2 * D_CONV], convw_ref, conv_scr).astype(BF16)
    conv_scr[CONV_PAD - 2:CONV_PAD, :] = s_conv[tm - 2:, :]
    gate_scr[...] = _sigmoid(gates)

    @pl.when(i == tiles_per_seq - 1)
    def _():
        kout_ref[0] = _dot(u, wk_ref[...])[tm - ATT_WINDOW:, :]
        vout_ref[0] = v[tm - ATT_WINDOW:, :]
        cout_ref[0] = s_conv[tm - 2:, :]

    def attend(first_tile):
        col = lax.broadcasted_iota(jnp.int32, (1, PAIR_KEYS), 1)

        def scores(h, jj, slot):
            k0 = jj * PAIR
            s = _dot(q_scr[h, k0:k0 + PAIR, :], kt_scr[h, :, k0:k0 + PAIR_KEYS])
            s = s + bias_scr[h]
            if first_tile:
                s = s + jnp.where(col + k0 < ATT_WINDOW, NEG_INF, 0.0)
            s_scr[slot] = s
            return jnp.max(s, axis=-1, keepdims=True)

        def weighted_values(h, jj, slot, m):
            k0 = jj * PAIR
            p = jnp.exp(s_scr[slot] - m).astype(BF16)
            o = _dot(p, v_scr[h, k0:k0 + PAIR_KEYS, :])
            o_scr[h, k0:k0 + PAIR, :] = o[:, :HEAD_DIM] / o[:, HEAD_DIM:]

        def per_head_group(t, carry):
            steps = [(t * HEADS_PER_STEP + hh, jj)
                     for hh in range(HEADS_PER_STEP) for jj in range(n_pairs)]
            maxes = {}
            for k in range(len(steps) + ATT_LOOKAHEAD):
                if k < len(steps):
                    maxes[k] = scores(*steps[k], k % SCORE_SLOTS)
                if k >= ATT_LOOKAHEAD:
                    j = k - ATT_LOOKAHEAD
                    weighted_values(*steps[j], j % SCORE_SLOTS, maxes.pop(j))
            return carry
        lax.fori_loop(0, N_HEADS // HEADS_PER_STEP, per_head_group, 0)

    @pl.when(i == 0)
    def _():
        attend(True)

    @pl.when(i > 0)
    def _():
        attend(False)

    m_conv = gate_scr[:, D_MODEL:] * _dot(yc_scr[...], wco_ref[...])
    o_att = jnp.concatenate([o_scr[h] for h in range(N_HEADS)], axis=-1).astype(BF16)
    m = (gate_scr[:, :D_MODEL] * _dot(o_att, wao_ref[...]) + m_conv).astype(BF16)
    kt_scr[:, :, :ATT_WINDOW] = kt_scr[:, :, tm:]
    v_scr[:, :ATT_WINDOW, :] = v_scr[:, tm:, :]
    h2_ref[...] = h1_ref[...] + _rmsnorm(_dot(m, wo_ref[...]), g_ref[3:4, :])


def _mixer_prompt_call(tab, h1, norm_g, wq, wkt, wk, wv, wxbc, wg, convw, wao, wco, wo, batch, seq):
    tm = TOKEN_TILE
    tiles = seq // tm
    row = lambda b, i: (b * tiles + i, 0)
    per_seq = lambda b, i: (b, 0, 0)
    weights = (wq, wkt, wk, wv, wxbc, wg, convw, wao, wco, wo)
    return pl.pallas_call(
        functools.partial(_mixer_prompt_kernel, tm=tm, tiles_per_seq=tiles),
        grid=(batch, tiles),
        in_specs=[pl.BlockSpec(memory_space=pltpu.SMEM),
                  pl.BlockSpec((tm, D_MODEL), row), _resident(norm_g.shape)]
                 + [_resident(w.shape) for w in weights],
        out_specs=[pl.BlockSpec((tm, D_MODEL), row),
                   pl.BlockSpec((1, ATT_WINDOW, D_ATT), per_seq),
                   pl.BlockSpec((1, ATT_WINDOW, D_ATT), per_seq),
                   pl.BlockSpec((1, CONV_WIDTH - 1, D_CONV), per_seq)],
        out_shape=[jax.ShapeDtypeStruct((batch * seq, D_MODEL), F32),
                   jax.ShapeDtypeStruct((batch, ATT_WINDOW, D_ATT), F32),
                   jax.ShapeDtypeStruct((batch, ATT_WINDOW, D_ATT), F32),
                   jax.ShapeDtypeStruct((batch, CONV_WIDTH - 1, D_CONV), F32)],
        scratch_shapes=[pltpu.VMEM((N_HEADS, tm, HEAD_DIM), BF16),
                        pltpu.VMEM((N_HEADS, HEAD_DIM, ATT_WINDOW + tm), BF16),
                        pltpu.VMEM((N_HEADS, ATT_WINDOW + tm, 2 * HEAD_DIM), BF16),
                        pltpu.VMEM((N_HEADS, PAIR, PAIR_KEYS), F32),
                        pltpu.VMEM((SCORE_SLOTS, PAIR, PAIR_KEYS), F32),
                        pltpu.VMEM((N_HEADS, tm, HEAD_DIM), F32),
                        pltpu.VMEM((CONV_PAD + tm, D_CONV), F32),
                        pltpu.VMEM((tm, D_CONV), BF16),
                        pltpu.VMEM((tm, 2 * D_MODEL), F32)],
        compiler_params=_params(2),
        name="mixer_prompt",
    )(tab, h1, norm_g, *weights)


def _inproj_kernel(h1_ref, g_ref, wq_ref, wk_ref, wv_ref, wxbc_ref, wg_ref,
                   q_ref, k_ref, v_ref, xbc_ref, gates_ref):
    u = _rmsnorm(h1_ref[...], g_ref[2:3, :]).astype(BF16)
    q_ref[...] = _dot(u, wq_ref[...]) * (HEAD_DIM ** -0.5)
    k_ref[...] = _dot(u, wk_ref[...])
    v_ref[...] = _dot(u, wv_ref[...])
    xbc_ref[...] = _dot(u, wxbc_ref[...])
    gates_ref[...] = _dot(u, wg_ref[...])


def _inproj_call(h1, norm_g, wq, wk, wv, wxbc, wg):
    n = h1.shape[0]
    ins = (h1, norm_g, wq, wk, wv, wxbc, wg)
    widths = (D_ATT, D_ATT, D_ATT, 3 * D_CONV, 2 * D_MODEL)
    return pl.pallas_call(
        _inproj_kernel,
        grid=(1,),
        in_specs=[_resident(a.shape) for a in ins],
        out_specs=[pl.BlockSpec((n, w), lambda i: (0, 0)) for w in widths],
        out_shape=[jax.ShapeDtypeStruct((n, w), F32) for w in widths],
        compiler_params=_params(1),
        name="inproj_sample",
    )(*ins)


def _attn_sample_kernel(tab_ref, q_ref, k_ref, v_ref, ck_ref, cv_ref, xbc_ref, cconv_ref, convw_ref,
                        o_ref, y_ref, cout_ref, bias_c_scr, bias_n_scr, conv_scr, *, t, lc):
    @pl.when(pl.program_id(0) == 0)
    def _():
        r = lax.broadcasted_iota(jnp.int32, (t, lc), 0)
        c = lax.broadcasted_iota(jnp.int32, (t, lc), 1)
        d_c = jnp.minimum(lc + r - c, MAX_REL)
        r = lax.broadcasted_iota(jnp.int32, (t, t), 0)
        c = lax.broadcasted_iota(jnp.int32, (t, t), 1)
        d_n = jnp.clip(r - c, -MAX_REL, MAX_REL)

        def per_head(h, carry):
            bias_c_scr[h] = _build_bias(tab_ref, h, d_c, 1, MAX_REL)
            bias_n_scr[h] = _build_bias(tab_ref, h, d_n, -min(t - 1, MAX_REL), min(t - 1, MAX_REL))
            return carry
        lax.fori_loop(0, N_HEADS, per_head, 0)

    q = q_ref[...]
    k = k_ref[...]
    v = v_ref[...]
    outs = []
    for h in range(N_HEADS):
        hs = slice(h * HEAD_DIM, (h + 1) * HEAD_DIM)
        qh = q[:, hs].astype(BF16)
        s_c = _dot_nt(qh, ck_ref[0, :, hs].astype(BF16)) + bias_c_scr[h]
        s_n = _dot_nt(qh, k[:, hs].astype(BF16)) + bias_n_scr[h]
        m = jnp.maximum(jnp.max(s_c, axis=-1, keepdims=True), jnp.max(s_n, axis=-1, keepdims=True))
        p_c = jnp.exp(s_c - m)
        p_n = jnp.exp(s_n - m)
        l = jnp.sum(p_c, axis=-1, keepdims=True) + jnp.sum(p_n, axis=-1, keepdims=True)
        o = (_dot(p_c.astype(BF16), cv_ref[0, :, hs].astype(BF16))
             + _dot(p_n.astype(BF16), v[:, hs].astype(BF16)))
        outs.append(o * (1.0 / l))
    o_ref[...] = jnp.concatenate(outs, axis=-1)

    xbc = xbc_ref[...]
    s_conv = xbc[:, 2 * D_CONV:] * xbc[:, :D_CONV]
    conv_scr[0:CONV_PAD, :] = jnp.zeros((CONV_PAD, D_CONV), F32)
    conv_scr[CONV_PAD - 2:CONV_PAD, :] = cconv_ref[0]
    y_ref[...] = _short_conv(s_conv, xbc[:, D_CONV:2 * D_CONV], convw_ref, conv_scr)
    cout_ref[0] = conv_scr[CONV_PAD + t - 2:CONV_PAD + t, :]


def _attn_sample_call(tab, q, k, v, cache_k, cache_v, xbc, cache_conv, convw, batch, t):
    lc = cache_k.shape[1]
    row = lambda b: (b, 0)
    per_seq = lambda b: (b, 0, 0)
    return pl.pallas_call(
        functools.partial(_attn_sample_kernel, t=t, lc=lc),
        grid=(batch,),
        in_specs=[pl.BlockSpec(memory_space=pltpu.SMEM),
                  pl.BlockSpec((t, D_ATT), row), pl.BlockSpec((t, D_ATT), row),
                  pl.BlockSpec((t, D_ATT), row),
                  pl.BlockSpec((1, lc, D_ATT), per_seq), pl.BlockSpec((1, lc, D_ATT), per_seq),
                  pl.BlockSpec((t, 3 * D_CONV), row),
                  pl.BlockSpec((1, CONV_WIDTH - 1, D_CONV), per_seq),
                  _resident(convw.shape)],
        out_specs=[pl.BlockSpec((t, D_ATT), row), pl.BlockSpec((t, D_CONV), row),
                   pl.BlockSpec((1, CONV_WIDTH - 1, D_CONV), per_seq)],
        out_shape=[jax.ShapeDtypeStruct((batch * t, D_ATT), F32),
                   jax.ShapeDtypeStruct((batch * t, D_CONV), F32),
                   jax.ShapeDtypeStruct((batch, CONV_WIDTH - 1, D_CONV), F32)],
        scratch_shapes=[pltpu.VMEM((N_HEADS, t, lc), F32),
                        pltpu.VMEM((N_HEADS, t, t), F32),
                        pltpu.VMEM((CONV_PAD + t, D_CONV), F32)],
        compiler_params=_params(1),
        name="attn_sample",
    )(tab, q, k, v, cache_k, cache_v, xbc, cache_conv, convw)


def _mix_kernel(h1_ref, o_ref, y_ref, gates_ref, g_ref, wao_ref, wco_ref, wo_ref, h2_ref):
    h2_ref[...] = _mix_out(h1_ref[...], o_ref[...], y_ref[...], gates_ref[...], g_ref[3:4, :],
                           wao_ref, wco_ref, wo_ref)


def _mix_call(h1, o_att, y_conv, gates, norm_g, wao, wco, wo):
    ins = (h1, o_att, y_conv, gates, norm_g, wao, wco, wo)
    return pl.pallas_call(
        _mix_kernel,
        grid=(1,),
        in_specs=[_resident(a.shape) for a in ins],
        out_specs=pl.BlockSpec(h1.shape, lambda i: (0, 0)),
        out_shape=jax.ShapeDtypeStruct(h1.shape, F32),
        compiler_params=_params(1),
        name="mix_sample",
    )(*ins)


def kernel(x_prompt, x_sample, cache_k, cache_v, cache_conv, p_prompt, p_sample, norm_g,
           w1_gate, w1_up, w1_down, w_in, conv_w, rel_bias, w_att_out, w_conv_out, w_out,
           w2_gate, w2_up, w2_down, w_ple_gate, w_ple_proj):
    depth = norm_g.shape[0]
    assert depth == 1, "one layer per step"
    batch, seq, _ = x_prompt.shape
    dec_batch, dec_seq, _ = x_sample.shape
    assert seq % TOKEN_TILE == 0 and TOKEN_TILE % PAIR == 0 and TOKEN_TILE >= ATT_WINDOW
    l = 0

    g = norm_g[l]
    w1g = w1_gate[l].astype(BF16)
    w1u = w1_up[l].astype(BF16)
    w2g = w2_gate[l].astype(BF16)
    w2u = w2_up[l].astype(BF16)
    w1d = w1_down[l].astype(BF16)
    w2d = w2_down[l].astype(BF16)
    win = w_in[l].astype(BF16)
    wq = win[:, :D_ATT]
    wk = win[:, D_ATT:2 * D_ATT]
    wv = win[:, 2 * D_ATT:3 * D_ATT]
    wxbc = win[:, 3 * D_ATT:3 * D_ATT + 3 * D_CONV]
    wg = win[:, 3 * D_ATT + 3 * D_CONV:]
    wkt = wk.T
    wao = w_att_out[l].astype(BF16)
    wco = w_conv_out[l].astype(BF16)
    wo = w_out[l].astype(BF16)
    wpg = w_ple_gate[l].astype(BF16)
    wpp = w_ple_proj[l].astype(BF16)
    tab = rel_bias[l]
    convw = conv_w[l]

    xp = x_prompt.reshape(batch * seq, D_MODEL)
    h1p = _ffn_call(xp, g, w1g, w1u, w1d, 0, 1)
    h2p, k_p, v_p, c_p = _mixer_prompt_call(tab, h1p, g, wq, wkt, wk, wv, wxbc, wg, convw,
                                            wao, wco, wo, batch, seq)
    y_p = _ffn_ple_call(h2p, p_prompt[l].reshape(batch * seq, D_PLE), g, w2g, w2u, w2d, wpg, wpp,
                        4, 5, 6)

    lc = cache_k.shape[2]
    xs = x_sample.reshape(dec_batch * dec_seq, D_MODEL)
    h1s = _ffn_call(xs, g, w1g, w1u, w1d, 0, 1)
    q_s, k_s, v_s, xbc_s, gates_s = _inproj_call(h1s, g, wq, wk, wv, wxbc, wg)
    o_s, yc_s, c_s = _attn_sample_call(tab, q_s, k_s, v_s,
                                       cache_k[l].reshape(dec_batch, lc, D_ATT),
                                       cache_v[l].reshape(dec_batch, lc, D_ATT),
                                       xbc_s, cache_conv[l], convw, dec_batch, dec_seq)
    h2s = _mix_call(h1s, o_s, yc_s, gates_s, g, wao, wco, wo)
    y_s = _ffn_ple_call(h2s, p_sample[l].reshape(dec_batch * dec_seq, D_PLE), g, w2g, w2u, w2d,
                        wpg, wpp, 4, 5, 6)

    kv_p = (1, batch, ATT_WINDOW, N_HEADS, HEAD_DIM)
    kv_s = (1, dec_batch, dec_seq, N_HEADS, HEAD_DIM)
    return (y_p.reshape(batch, seq, D_MODEL), y_s.reshape(dec_batch, dec_seq, D_MODEL),
            k_p.reshape(kv_p), v_p.reshape(kv_p), c_p[None],
            k_s.reshape(kv_s), v_s.reshape(kv_s), c_s[None])
```

```python
import functools

import jax
import jax.numpy as jnp
from jax import lax
from jax.experimental import pallas as pl
from jax.experimental.pallas import tpu as pltpu

F32 = jnp.float32
BF16 = jnp.bfloat16

D_MODEL = 1024
CHUNK = 64
LEFT_CHUNKS = 8
ATT_WINDOW = LEFT_CHUNKS * CHUNK
BAND = ATT_WINDOW + CHUNK
N_HEADS = 8
HEAD_DIM = 64
D_ATT = N_HEADS * HEAD_DIM
D_CONV = D_MODEL // 2
CONV_WIDTH = 3
MAX_REL = 128
D_FF = 2816
D_PLE = 256
EPS = 1e-6
NEG_INF = -1e30

FF_CHUNK = 256
N_FF_CHUNKS = D_FF // FF_CHUNK
MIX_CHUNK = 256
PAIR = 2 * CHUNK
PAIR_KEYS = ATT_WINDOW + PAIR
HEADS_PER_STEP = 8
ATT_LOOKAHEAD = 3
SCORE_SLOTS = ATT_LOOKAHEAD + 2
TOKEN_TILE = 512
CONV_PAD = 8
VMEM_LIMIT_BYTES = 56 * 1024 * 1024


def _dot(a, b):
    return jnp.dot(a, b, preferred_element_type=F32)


def _dot_nt(a, b):
    return lax.dot_general(a, b, (((1,), (1,)), ((), ())), preferred_element_type=F32)


def _rmsnorm(x, g):
    return x * lax.rsqrt(jnp.mean(x * x, axis=-1, keepdims=True) + EPS) * g


def _sigmoid(x):
    return 1.0 / (1.0 + jnp.exp(-x))


def _resident(shape):
    return pl.BlockSpec(shape, lambda *_: (0,) * len(shape), pipeline_mode=pl.Buffered(1))


def _params(n_grid_dims):
    return pltpu.CompilerParams(
        dimension_semantics=("arbitrary",) * n_grid_dims,
        vmem_limit_bytes=VMEM_LIMIT_BYTES,
    )


def _ffn_body(x, g_pre, g_post, wg_ref, wu_ref, wd_ref, act_scr):
    u = _rmsnorm(x, g_pre).astype(BF16)
    for c in range(N_FF_CHUNKS):
        cs = slice(c * FF_CHUNK, (c + 1) * FF_CHUNK)
        gate = _dot(u, wg_ref[:, cs])
        up = _dot(u, wu_ref[:, cs])
        act_scr[:, cs] = (gate * _sigmoid(gate) * up).astype(BF16)
    y = _dot(act_scr[...], wd_ref[...])
    return x + 0.5 * _rmsnorm(y, g_post)


def _ffn_kernel(x_ref, g_ref, wg_ref, wu_ref, wd_ref, o_ref, act_scr, *, pre, post):
    o_ref[...] = _ffn_body(x_ref[...], g_ref[pre:pre + 1, :], g_ref[post:post + 1, :],
                           wg_ref, wu_ref, wd_ref, act_scr)


def _ffn_ple_kernel(x_ref, p_ref, g_ref, wg_ref, wu_ref, wd_ref, wpg_ref, wpp_ref, o_ref, act_scr,
                    *, pre, post, ple):
    h = _ffn_body(x_ref[...], g_ref[pre:pre + 1, :], g_ref[post:post + 1, :],
                  wg_ref, wu_ref, wd_ref, act_scr)
    gate = _sigmoid(_dot(h.astype(BF16), wpg_ref[...]))
    proj = _dot(p_ref[...].astype(BF16), wpp_ref[...])
    o_ref[...] = h + _rmsnorm(gate * proj, g_ref[ple:ple + 1, :])


def _ffn_call(x, norm_g, wg, wu, wd, pre, post):
    n = x.shape[0]
    tm = min(TOKEN_TILE, n)
    row = lambda i: (i, 0)
    return pl.pallas_call(
        functools.partial(_ffn_kernel, pre=pre, post=post),
        grid=(n // tm,),
        in_specs=[pl.BlockSpec((tm, D_MODEL), row), _resident(norm_g.shape),
                  _resident(wg.shape), _resident(wu.shape), _resident(wd.shape)],
        out_specs=pl.BlockSpec((tm, D_MODEL), row),
        out_shape=jax.ShapeDtypeStruct((n, D_MODEL), F32),
        scratch_shapes=[pltpu.VMEM((tm, D_FF), BF16)],
        compiler_params=_params(1),
        name="ffn",
    )(x, norm_g, wg, wu, wd)


def _ffn_ple_call(x, p, norm_g, wg, wu, wd, wpg, wpp, pre, post, ple):
    n = x.shape[0]
    tm = min(TOKEN_TILE, n)
    row = lambda i: (i, 0)
    return pl.pallas_call(
        functools.partial(_ffn_ple_kernel, pre=pre, post=post, ple=ple),
        grid=(n // tm,),
        in_specs=[pl.BlockSpec((tm, D_MODEL), row), pl.BlockSpec((tm, D_PLE), row),
                  _resident(norm_g.shape), _resident(wg.shape), _resident(wu.shape),
                  _resident(wd.shape), _resident(wpg.shape), _resident(wpp.shape)],
        out_specs=pl.BlockSpec((tm, D_MODEL), row),
        out_shape=jax.ShapeDtypeStruct((n, D_MODEL), F32),
        scratch_shapes=[pltpu.VMEM((tm, D_FF), BF16)],
        compiler_params=_params(1),
        name="ffn_ple",
    )(x, p, norm_g, wg, wu, wd, wpg, wpp)


def _build_bias(tab_ref, head, dist, lo, hi):
    def body(j, acc):
        v = lo + j
        return jnp.where(dist == v, tab_ref[head, v + MAX_REL], acc)
    return lax.fori_loop(0, hi - lo + 1, body, jnp.full(dist.shape, NEG_INF, F32))


def _short_conv(s, bgate, convw_ref, conv_scr):
    t = s.shape[0]
    conv_scr[CONV_PAD:CONV_PAD + t, :] = s
    y = (convw_ref[0:1, :] * conv_scr[CONV_PAD - 2:CONV_PAD - 2 + t, :]
         + convw_ref[1:2, :] * conv_scr[CONV_PAD - 1:CONV_PAD - 1 + t, :]
         + convw_ref[2:3, :] * s)
    return bgate * y


def _mix_out(h1, o_att, y_conv, gates, g_post, wao_ref, wco_ref, wo_ref):
    m = (_sigmoid(gates[:, :D_MODEL]) * _dot(o_att.astype(BF16), wao_ref[...])
         + _sigmoid(gates[:, D_MODEL:]) * _dot(y_conv.astype(BF16), wco_ref[...]))
    return h1 + _rmsnorm(_dot(m.astype(BF16), wo_ref[...]), g_post)


def _mixer_prompt_kernel(tab_ref, h1_ref, g_ref, wq_ref, wkt_ref, wk_ref, wv_ref, wxbc_ref, wg_ref,
                         convw_ref, wao_ref, wco_ref, wo_ref,
                         h2_ref, kout_ref, vout_ref, cout_ref,
                         q_scr, kt_scr, v_scr, bias_scr, s_scr, o_scr, conv_scr, yc_scr, u_scr, m_scr,
                         *, tm, tiles_per_seq):
    b = pl.program_id(0)
    i = pl.program_id(1)
    n_pairs = tm // PAIR

    @pl.when((b == 0) & (i == 0))
    def _():
        rows = 32
        for rb in range(PAIR // rows):
            r = lax.broadcasted_iota(jnp.int32, (rows, PAIR_KEYS), 0) + rb * rows
            c = lax.broadcasted_iota(jnp.int32, (rows, PAIR_KEYS), 1)
            n = c - (r & CHUNK)
            d = jnp.minimum((r & (CHUNK - 1)) + ATT_WINDOW - n, MAX_REL)
            d = jnp.where((n >= 0) & (n < BAND), d, MAX_REL + 1)

            def per_head(h, carry):
                bias_scr[h, rb * rows:(rb + 1) * rows, :] = _build_bias(
                    tab_ref, h, d, -(CHUNK - 1), MAX_REL)
                return carry
            lax.fori_loop(0, N_HEADS, per_head, 0)

    @pl.when(i == 0)
    def _():
        kt_scr[:, :, :ATT_WINDOW] = jnp.zeros((N_HEADS, HEAD_DIM, ATT_WINDOW), BF16)
        v_scr[:, :ATT_WINDOW, :] = jnp.zeros((N_HEADS, ATT_WINDOW, 2 * HEAD_DIM), BF16)
        conv_scr[0:CONV_PAD, :] = jnp.zeros((CONV_PAD, D_CONV), F32)

    u = _rmsnorm(h1_ref[...], g_ref[2:3, :]).astype(BF16)
    u_scr[...] = u

    for c in range(D_CONV // MIX_CHUNK):
        cs = slice(c * MIX_CHUNK, (c + 1) * MIX_CHUNK)
        x_in = _dot(u, wxbc_ref[:, cs])
        b_gate = _dot(u, wxbc_ref[:, D_CONV + c * MIX_CHUNK:D_CONV + (c + 1) * MIX_CHUNK])
        c_gate = _dot(u, wxbc_ref[:, 2 * D_CONV + c * MIX_CHUNK:2 * D_CONV + (c + 1) * MIX_CHUNK])
        s_conv = c_gate * x_in
        yc_scr[:, cs] = _short_conv(s_conv, b_gate, convw_ref.at[:, cs],
                                    conv_scr.at[:, cs]).astype(BF16)
    new_prefix = conv_scr[CONV_PAD + tm - 2:CONV_PAD + tm, :]
    conv_scr[CONV_PAD - 2:CONV_PAD, :] = new_prefix

    q = _dot(u, wq_ref[...]) * (HEAD_DIM ** -0.5)
    kt = _dot_nt(wkt_ref[...], u)
    v = _dot(u, wv_ref[...])
    ones = jnp.ones((tm, HEAD_DIM), F32)
    for h in range(N_HEADS):
        hs = slice(h * HEAD_DIM, (h + 1) * HEAD_DIM)
        q_scr[h] = q[:, hs].astype(BF16)
        kt_scr[h, :, ATT_WINDOW:] = kt[hs, :].astype(BF16)
        v_scr[h, ATT_WINDOW:, :] = jnp.concatenate([v[:, hs], ones], axis=-1).astype(BF16)

    @pl.when(i == tiles_per_seq - 1)
    def _():
        kout_ref[0] = _dot(u, wk_ref[...])[tm - ATT_WINDOW:, :]
        vout_ref[0] = v[tm - ATT_WINDOW:, :]
        cout_ref[0] = new_prefix

    def attend(first_tile):
        col = lax.broadcasted_iota(jnp.int32, (1, PAIR_KEYS), 1)

        def scores(h, jj, slot):
            k0 = jj * PAIR
            s = _dot(q_scr[h, k0:k0 + PAIR, :], kt_scr[h, :, k0:k0 + PAIR_KEYS])
            s = s + bias_scr[h]
            if first_tile:
                s = s + jnp.where(col + k0 < ATT_WINDOW, NEG_INF, 0.0)
            s_scr[slot] = s
            return jnp.max(s, axis=-1, keepdims=True)

        def weighted_values(h, jj, slot, m):
            k0 = jj * PAIR
            p = jnp.exp(s_scr[slot] - m).astype(BF16)
            o = _dot(p, v_scr[h, k0:k0 + PAIR_KEYS, :])
            o_scr[h, k0:k0 + PAIR, :] = o[:, :HEAD_DIM] / o[:, HEAD_DIM:]

        def per_head_group(t, carry):
            steps = [(t * HEADS_PER_STEP + hh, jj)
                     for hh in range(HEADS_PER_STEP) for jj in range(n_pairs)]
            maxes = {}
            for k in range(len(steps) + ATT_LOOKAHEAD):
                if k < len(steps):
                    maxes[k] = scores(*steps[k], k % SCORE_SLOTS)
                if k >= ATT_LOOKAHEAD:
                    j = k - ATT_LOOKAHEAD
                    weighted_values(*steps[j], j % SCORE_SLOTS, maxes.pop(j))
            return carry
        lax.fori_loop(0, N_HEADS // HEADS_PER_STEP, per_head_group, 0)

    @pl.when(i == 0)
    def _():
        attend(True)

    @pl.when(i > 0)
    def _():
        attend(False)

    u = u_scr[...]
    y_conv = yc_scr[...]
    o_att = jnp.concatenate([o_scr[h] for h in range(N_HEADS)], axis=-1).astype(BF16)
    for c in range(D_MODEL // MIX_CHUNK):
        cs = slice(c * MIX_CHUNK, (c + 1) * MIX_CHUNK)
        g_conv = _dot(u, wg_ref[:, D_MODEL + c * MIX_CHUNK:D_MODEL + (c + 1) * MIX_CHUNK])
        m_conv = _sigmoid(g_conv) * _dot(y_conv, wco_ref[:, cs])
        g_att = _dot(u, wg_ref[:, cs])
        m_scr[:, cs] = (_sigmoid(g_att) * _dot(o_att, wao_ref[:, cs]) + m_conv).astype(BF16)
    kt_scr[:, :, :ATT_WINDOW] = kt_scr[:, :, tm:]
    v_scr[:, :ATT_WINDOW, :] = v_scr[:, tm:, :]
    h2_ref[...] = h1_ref[...] + _rmsnorm(_dot(m_scr[...], wo_ref[...]), g_ref[3:4, :])


def _mixer_prompt_call(tab, h1, norm_g, wq, wkt, wk, wv, wxbc, wg, convw, wao, wco, wo, batch, seq):
    tm = TOKEN_TILE
    tiles = seq // tm
    row = lambda b, i: (b * tiles + i, 0)
    per_seq = lambda b, i: (b, 0, 0)
    weights = (wq, wkt, wk, wv, wxbc, wg, convw, wao, wco, wo)
    return pl.pallas_call(
        functools.partial(_mixer_prompt_kernel, tm=tm, tiles_per_seq=tiles),
        grid=(batch, tiles),
        in_specs=[pl.BlockSpec(memory_space=pltpu.SMEM),
                  pl.BlockSpec((tm, D_MODEL), row), _resident(norm_g.shape)]
                 + [_resident(w.shape) for w in weights],
        out_specs=[pl.BlockSpec((tm, D_MODEL), row),
                   pl.BlockSpec((1, ATT_WINDOW, D_ATT), per_seq),
                   pl.BlockSpec((1, ATT_WINDOW, D_ATT), per_seq),
                   pl.BlockSpec((1, CONV_WIDTH - 1, D_CONV), per_seq)],
        out_shape=[jax.ShapeDtypeStruct((batch * seq, D_MODEL), F32),
                   jax.ShapeDtypeStruct((batch, ATT_WINDOW, D_ATT), F32),
                   jax.ShapeDtypeStruct((batch, ATT_WINDOW, D_ATT), F32),
                   jax.ShapeDtypeStruct((batch, CONV_WIDTH - 1, D_CONV), F32)],
        scratch_shapes=[pltpu.VMEM((N_HEADS, tm, HEAD_DIM), BF16),
                        pltpu.VMEM((N_HEADS, HEAD_DIM, ATT_WINDOW + tm), BF16),
                        pltpu.VMEM((N_HEADS, ATT_WINDOW + tm, 2 * HEAD_DIM), BF16),
                        pltpu.VMEM((N_HEADS, PAIR, PAIR_KEYS), F32),
                        pltpu.VMEM((SCORE_SLOTS, PAIR, PAIR_KEYS), F32),
                        pltpu.VMEM((N_HEADS, tm, HEAD_DIM), F32),
                        pltpu.VMEM((CONV_PAD + tm, D_CONV), F32),
                        pltpu.VMEM((tm, D_CONV), BF16),
                        pltpu.VMEM((tm, D_MODEL), BF16),
                        pltpu.VMEM((tm, D_MODEL), BF16)],
        compiler_params=_params(2),
        name="mixer_prompt",
    )(tab, h1, norm_g, *weights)


def _inproj_kernel(h1_ref, g_ref, wq_ref, wk_ref, wv_ref, wxbc_ref, wg_ref,
                   q_ref, k_ref, v_ref, xbc_ref, gates_ref):
    u = _rmsnorm(h1_ref[...], g_ref[2:3, :]).astype(BF16)
    q_ref[...] = _dot(u, wq_ref[...]) * (HEAD_DIM ** -0.5)
    k_ref[...] = _dot(u, wk_ref[...])
    v_ref[...] = _dot(u, wv_ref[...])
    xbc_ref[...] = _dot(u, wxbc_ref[...])
    gates_ref[...] = _dot(u, wg_ref[...])


def _inproj_call(h1, norm_g, wq, wk, wv, wxbc, wg):
    n = h1.shape[0]
    ins = (h1, norm_g, wq, wk, wv, wxbc, wg)
    widths = (D_ATT, D_ATT, D_ATT, 3 * D_CONV, 2 * D_MODEL)
    return pl.pallas_call(
        _inproj_kernel,
        grid=(1,),
        in_specs=[_resident(a.shape) for a in ins],
        out_specs=[pl.BlockSpec((n, w), lambda i: (0, 0)) for w in widths],
        out_shape=[jax.ShapeDtypeStruct((n, w), F32) for w in widths],
        compiler_params=_params(1),
        name="inproj_sample",
    )(*ins)


def _attn_sample_kernel(tab_ref, q_ref, k_ref, v_ref, ck_ref, cv_ref, xbc_ref, cconv_ref, convw_ref,
                        o_ref, y_ref, cout_ref, bias_c_scr, bias_n_scr, conv_scr, *, t, lc):
    @pl.when(pl.program_id(0) == 0)
    def _():
        r = lax.broadcasted_iota(jnp.int32, (t, lc), 0)
        c = lax.broadcasted_iota(jnp.int32, (t, lc), 1)
        d_c = jnp.minimum(lc + r - c, MAX_REL)
        r = lax.broadcasted_iota(jnp.int32, (t, t), 0)
        c = lax.broadcasted_iota(jnp.int32, (t, t), 1)
        d_n = jnp.clip(r - c, -MAX_REL, MAX_REL)

        def per_head(h, carry):
            bias_c_scr[h] = _build_bias(tab_ref, h, d_c, 1, MAX_REL)
            bias_n_scr[h] = _build_bias(tab_ref, h, d_n, -min(t - 1, MAX_REL), min(t - 1, MAX_REL))
            return carry
        lax.fori_loop(0, N_HEADS, per_head, 0)

    q = q_ref[...]
    k = k_ref[...]
    v = v_ref[...]
    outs = []
    for h in range(N_HEADS):
        hs = slice(h * HEAD_DIM, (h + 1) * HEAD_DIM)
        qh = q[:, hs].astype(BF16)
        s_c = _dot_nt(qh, ck_ref[0, :, hs].astype(BF16)) + bias_c_scr[h]
        s_n = _dot_nt(qh, k[:, hs].astype(BF16)) + bias_n_scr[h]
        m = jnp.maximum(jnp.max(s_c, axis=-1, keepdims=True), jnp.max(s_n, axis=-1, keepdims=True))
        p_c = jnp.exp(s_c - m)
        p_n = jnp.exp(s_n - m)
        l = jnp.sum(p_c, axis=-1, keepdims=True) + jnp.sum(p_n, axis=-1, keepdims=True)
        o = (_dot(p_c.astype(BF16), cv_ref[0, :, hs].astype(BF16))
             + _dot(p_n.astype(BF16), v[:, hs].astype(BF16)))
        outs.append(o * (1.0 / l))
    o_ref[...] = jnp.concatenate(outs, axis=-1)

    xbc = xbc_ref[...]
    s_conv = xbc[:, 2 * D_CONV:] * xbc[:, :D_CONV]
    conv_scr[0:CONV_PAD, :] = jnp.zeros((CONV_PAD, D_CONV), F32)
    conv_scr[CONV_PAD - 2:CONV_PAD, :] = cconv_ref[0]
    y_ref[...] = _short_conv(s_conv, xbc[:, D_CONV:2 * D_CONV], convw_ref, conv_scr)
    cout_ref[0] = conv_scr[CONV_PAD + t - 2:CONV_PAD + t, :]


def _attn_sample_call(tab, q, k, v, cache_k, cache_v, xbc, cache_conv, convw, batch, t):
    lc = cache_k.shape[1]
    row = lambda b: (b, 0)
    per_seq = lambda b: (b, 0, 0)
    return pl.pallas_call(
        functools.partial(_attn_sample_kernel, t=t, lc=lc),
        grid=(batch,),
        in_specs=[pl.BlockSpec(memory_space=pltpu.SMEM),
                  pl.BlockSpec((t, D_ATT), row), pl.BlockSpec((t, D_ATT), row),
                  pl.BlockSpec((t, D_ATT), row),
                  pl.BlockSpec((1, lc, D_ATT), per_seq), pl.BlockSpec((1, lc, D_ATT), per_seq),
                  pl.BlockSpec((t, 3 * D_CONV), row),
                  pl.BlockSpec((1, CONV_WIDTH - 1, D_CONV), per_seq),
                  _resident(convw.shape)],
        out_specs=[pl.BlockSpec((t, D_ATT), row), pl.BlockSpec((t, D_CONV), row),
                   pl.BlockSpec((1, CONV_WIDTH - 1, D_CONV), per_seq)],
        out_shape=[jax.ShapeDtypeStruct((batch * t, D_ATT), F32),
                   jax.ShapeDtypeStruct((batch * t, D_CONV), F32),
                   jax.ShapeDtypeStruct((batch, CONV_WIDTH - 1, D_CONV), F32)],
        scratch_shapes=[pltpu.VMEM((N_HEADS, t, lc), F32),
                        pltpu.VMEM((N_HEADS, t, t), F32),
                        pltpu.VMEM((CONV_PAD + t, D_CONV), F32)],
        compiler_params=_params(1),
        name="attn_sample",
    )(tab, q, k, v, cache_k, cache_v, xbc, cache_conv, convw)


def _mix_kernel(h1_ref, o_ref, y_ref, gates_ref, g_ref, wao_ref, wco_ref, wo_ref, h2_ref):
    h2_ref[...] = _mix_out(h1_ref[...], o_ref[...], y_ref[...], gates_ref[...], g_ref[3:4, :],
                           wao_ref, wco_ref, wo_ref)


def _mix_call(h1, o_att, y_conv, gates, norm_g, wao, wco, wo):
    ins = (h1, o_att, y_conv, gates, norm_g, wao, wco, wo)
    return pl.pallas_call(
        _mix_kernel,
        grid=(1,),
        in_specs=[_resident(a.shape) for a in ins],
        out_specs=pl.BlockSpec(h1.shape, lambda i: (0, 0)),
        out_shape=jax.ShapeDtypeStruct(h1.shape, F32),
        compiler_params=_params(1),
        name="mix_sample",
    )(*ins)


def kernel(x_prompt, x_sample, cache_k, cache_v, cache_conv, p_prompt, p_sample, norm_g,
           w1_gate, w1_up, w1_down, w_in, conv_w, rel_bias, w_att_out, w_conv_out, w_out,
           w2_gate, w2_up, w2_down, w_ple_gate, w_ple_proj):
    depth = norm_g.shape[0]
    assert depth == 1, "one layer per step"
    batch, seq, _ = x_prompt.shape
    dec_batch, dec_seq, _ = x_sample.shape
    assert seq % TOKEN_TILE == 0 and TOKEN_TILE % PAIR == 0 and TOKEN_TILE >= ATT_WINDOW
    l = 0

    g = norm_g[l]
    w1g = w1_gate[l].astype(BF16)
    w1u = w1_up[l].astype(BF16)
    w2g = w2_gate[l].astype(BF16)
    w2u = w2_up[l].astype(BF16)
    w1d = w1_down[l].astype(BF16)
    w2d = w2_down[l].astype(BF16)
    win = w_in[l].astype(BF16)
    wq = win[:, :D_ATT]
    wk = win[:, D_ATT:2 * D_ATT]
    wv = win[:, 2 * D_ATT:3 * D_ATT]
    wxbc = win[:, 3 * D_ATT:3 * D_ATT + 3 * D_CONV]
    wg = win[:, 3 * D_ATT + 3 * D_CONV:]
    wkt = wk.T
    wao = w_att_out[l].astype(BF16)
    wco = w_conv_out[l].astype(BF16)
    wo = w_out[l].astype(BF16)
    wpg = w_ple_gate[l].astype(BF16)
    wpp = w_ple_proj[l].astype(BF16)
    tab = rel_bias[l]
    convw = conv_w[l]

    xp = x_prompt.reshape(batch * seq, D_MODEL)
    h1p = _ffn_call(xp, g, w1g, w1u, w1d, 0, 1)
    h2p, k_p, v_p, c_p = _mixer_prompt_call(tab, h1p, g, wq, wkt, wk, wv, wxbc, wg, convw,
                                            wao, wco, wo, batch, seq)
    y_p = _ffn_ple_call(h2p, p_prompt[l].reshape(batch * seq, D_PLE), g, w2g, w2u, w2d, wpg, wpp,
                        4, 5, 6)

    lc = cache_k.shape[2]
    xs = x_sample.reshape(dec_batch * dec_seq, D_MODEL)
    h1s = _ffn_call(xs, g, w1g, w1u, w1d, 0, 1)
    q_s, k_s, v_s, xbc_s, gates_s = _inproj_call(h1s, g, wq, wk, wv, wxbc, wg)
    o_s, yc_s, c_s = _attn_sample_call(tab, q_s, k_s, v_s,
                                       cache_k[l].reshape(dec_batch, lc, D_ATT),
                                       cache_v[l].reshape(dec_batch, lc, D_ATT),
                                       xbc_s, cache_conv[l], convw, dec_batch, dec_seq)
    h2s = _mix_call(h1s, o_s, yc_s, gates_s, g, wao, wco, wo)
    y_s = _ffn_ple_call(h2s, p_sample[l].reshape(dec_batch * dec_seq, D_PLE), g, w2g, w2u, w2d,
                        wpg, wpp, 4, 5, 6)

    kv_p = (1, batch, ATT_WINDOW, N_HEADS, HEAD_DIM)
    kv_s = (1, dec_batch, dec_seq, N_HEADS, HEAD_DIM)
    return (y_p.reshape(batch, seq, D_MODEL), y_s.reshape(dec_batch, dec_seq, D_MODEL),
            k_p.reshape(kv_p), v_p.reshape(kv_p), c_p[None],
            k_s.reshape(kv_s), v_s.reshape(kv_s), c_s[None])
```

```python
import functools

import jax
import jax.numpy as jnp
from jax import lax
from jax.experimental import pallas as pl
from jax.experimental.pallas import tpu as pltpu

F32 = jnp.float32
BF16 = jnp.bfloat16

D_MODEL = 1024
CHUNK = 64
LEFT_CHUNKS = 8
ATT_WINDOW = LEFT_CHUNKS * CHUNK
BAND = ATT_WINDOW + CHUNK
N_HEADS = 8
HEAD_DIM = 64
D_ATT = N_HEADS * HEAD_DIM
D_CONV = D_MODEL // 2
CONV_WIDTH = 3
MAX_REL = 128
D_FF = 2816
D_PLE = 256
EPS = 1e-6
NEG_INF = -1e30

FF_CHUNK = 256
N_FF_CHUNKS = D_FF // FF_CHUNK
MIX_CHUNK = 256
PAIR = 2 * CHUNK
PAIR_KEYS = ATT_WINDOW + PAIR
HEADS_PER_STEP = 8
ATT_LOOKAHEAD = 3
SCORE_SLOTS = ATT_LOOKAHEAD + 2
TOKEN_TILE = 512
CONV_PAD = 8
VMEM_LIMIT_BYTES = 56 * 1024 * 1024


def _dot(a, b):
    return jnp.dot(a, b, preferred_element_type=F32)


def _dot_nt(a, b):
    return lax.dot_general(a, b, (((1,), (1,)), ((), ())), preferred_element_type=F32)


def _rmsnorm(x, g):
    return x * lax.rsqrt(jnp.mean(x * x, axis=-1, keepdims=True) + EPS) * g


def _sigmoid(x):
    return 1.0 / (1.0 + jnp.exp(-x))


def _resident(shape):
    return pl.BlockSpec(shape, lambda *_: (0,) * len(shape), pipeline_mode=pl.Buffered(1))


def _params(n_grid_dims):
    return pltpu.CompilerParams(
        dimension_semantics=("arbitrary",) * n_grid_dims,
        vmem_limit_bytes=VMEM_LIMIT_BYTES,
    )


def _swiglu_hidden(u, wg_ref, wu_ref, act_ref, chunks):
    for c in chunks:
        cs = slice(c * FF_CHUNK, (c + 1) * FF_CHUNK)
        gate = _dot(u, wg_ref[:, cs])
        up = _dot(u, wu_ref[:, cs])
        act_ref[:, cs] = (gate * _sigmoid(gate) * up).astype(BF16)


def _ple(h, p, g_ple, wpg_ref, wpp_ref):
    gate = _sigmoid(_dot(h.astype(BF16), wpg_ref[...]))
    proj = _dot(p.astype(BF16), wpp_ref[...])
    return h + _rmsnorm(gate * proj, g_ple)


def _ffn_body(x, g_pre, g_post, wg_ref, wu_ref, wd_ref, act_scr):
    u = _rmsnorm(x, g_pre).astype(BF16)
    _swiglu_hidden(u, wg_ref, wu_ref, act_scr, range(N_FF_CHUNKS))
    y = _dot(act_scr[...], wd_ref[...])
    return x + 0.5 * _rmsnorm(y, g_post)


def _ffn_kernel(x_ref, g_ref, wg_ref, wu_ref, wd_ref, o_ref, act_scr, *, pre, post):
    o_ref[...] = _ffn_body(x_ref[...], g_ref[pre:pre + 1, :], g_ref[post:post + 1, :],
                           wg_ref, wu_ref, wd_ref, act_scr)


def _ffn_ple_kernel(x_ref, p_ref, g_ref, wg_ref, wu_ref, wd_ref, wpg_ref, wpp_ref, o_ref, act_scr,
                    *, pre, post, ple):
    h = _ffn_body(x_ref[...], g_ref[pre:pre + 1, :], g_ref[post:post + 1, :],
                  wg_ref, wu_ref, wd_ref, act_scr)
    o_ref[...] = _ple(h, p_ref[...], g_ref[ple:ple + 1, :], wpg_ref, wpp_ref)


def _ffn_call(x, norm_g, wg, wu, wd, pre, post):
    n = x.shape[0]
    tm = min(TOKEN_TILE, n)
    row = lambda i: (i, 0)
    return pl.pallas_call(
        functools.partial(_ffn_kernel, pre=pre, post=post),
        grid=(n // tm,),
        in_specs=[pl.BlockSpec((tm, D_MODEL), row), _resident(norm_g.shape),
                  _resident(wg.shape), _resident(wu.shape), _resident(wd.shape)],
        out_specs=pl.BlockSpec((tm, D_MODEL), row),
        out_shape=jax.ShapeDtypeStruct((n, D_MODEL), F32),
        scratch_shapes=[pltpu.VMEM((tm, D_FF), BF16)],
        compiler_params=_params(1),
        name="ffn",
    )(x, norm_g, wg, wu, wd)


def _ffn_ple_call(x, p, norm_g, wg, wu, wd, wpg, wpp, pre, post, ple):
    n = x.shape[0]
    tm = min(TOKEN_TILE, n)
    row = lambda i: (i, 0)
    return pl.pallas_call(
        functools.partial(_ffn_ple_kernel, pre=pre, post=post, ple=ple),
        grid=(n // tm,),
        in_specs=[pl.BlockSpec((tm, D_MODEL), row), pl.BlockSpec((tm, D_PLE), row),
                  _resident(norm_g.shape), _resident(wg.shape), _resident(wu.shape),
                  _resident(wd.shape), _resident(wpg.shape), _resident(wpp.shape)],
        out_specs=pl.BlockSpec((tm, D_MODEL), row),
        out_shape=jax.ShapeDtypeStruct((n, D_MODEL), F32),
        scratch_shapes=[pltpu.VMEM((tm, D_FF), BF16)],
        compiler_params=_params(1),
        name="ffn_ple",
    )(x, p, norm_g, wg, wu, wd, wpg, wpp)


def _build_bias(tab_ref, head, dist, lo, hi):
    def body(j, acc):
        v = lo + j
        return jnp.where(dist == v, tab_ref[head, v + MAX_REL], acc)
    return lax.fori_loop(0, hi - lo + 1, body, jnp.full(dist.shape, NEG_INF, F32))


def _short_conv(s, bgate, convw_ref, conv_scr):
    t = s.shape[0]
    conv_scr[CONV_PAD:CONV_PAD + t, :] = s
    y = (convw_ref[0:1, :] * conv_scr[CONV_PAD - 2:CONV_PAD - 2 + t, :]
         + convw_ref[1:2, :] * conv_scr[CONV_PAD - 1:CONV_PAD - 1 + t, :]
         + convw_ref[2:3, :] * s)
    return bgate * y


def _mix_out(h1, o_att, y_conv, gates, g_post, wao_ref, wco_ref, wo_ref):
    m = (_sigmoid(gates[:, :D_MODEL]) * _dot(o_att.astype(BF16), wao_ref[...])
         + _sigmoid(gates[:, D_MODEL:]) * _dot(y_conv.astype(BF16), wco_ref[...]))
    return h1 + _rmsnorm(_dot(m.astype(BF16), wo_ref[...]), g_post)


def _mixer_prompt_kernel(tab_ref, h1_ref, g_ref, wq_ref, wkt_ref, wk_ref, wv_ref, wxbc_ref, wg_ref,
                         convw_ref, wao_ref, wco_ref, wo_ref,
                         h2_ref, kout_ref, vout_ref, cout_ref,
                         q_scr, kt_scr, v_scr, bias_scr, s_scr, o_scr, conv_scr, yc_scr, u_scr, m_scr,
                         *, tm, tiles_per_seq):
    b = pl.program_id(0)
    i = pl.program_id(1)
    n_pairs = tm // PAIR

    @pl.when((b == 0) & (i == 0))
    def _():
        rows = 32
        for rb in range(PAIR // rows):
            r = lax.broadcasted_iota(jnp.int32, (rows, PAIR_KEYS), 0) + rb * rows
            c = lax.broadcasted_iota(jnp.int32, (rows, PAIR_KEYS), 1)
            n = c - (r & CHUNK)
            d = jnp.minimum((r & (CHUNK - 1)) + ATT_WINDOW - n, MAX_REL)
            d = jnp.where((n >= 0) & (n < BAND), d, MAX_REL + 1)

            def per_head(h, carry):
                bias_scr[h, rb * rows:(rb + 1) * rows, :] = _build_bias(
                    tab_ref, h, d, -(CHUNK - 1), MAX_REL)
                return carry
            lax.fori_loop(0, N_HEADS, per_head, 0)

    @pl.when(i == 0)
    def _():
        kt_scr[:, :, :ATT_WINDOW] = jnp.zeros((N_HEADS, HEAD_DIM, ATT_WINDOW), BF16)
        v_scr[:, :ATT_WINDOW, :] = jnp.zeros((N_HEADS, ATT_WINDOW, 2 * HEAD_DIM), BF16)
        conv_scr[0:CONV_PAD, :] = jnp.zeros((CONV_PAD, D_CONV), F32)

    u = _rmsnorm(h1_ref[...], g_ref[2:3, :]).astype(BF16)
    u_scr[...] = u

    for c in range(D_CONV // MIX_CHUNK):
        cs = slice(c * MIX_CHUNK, (c + 1) * MIX_CHUNK)
        x_in = _dot(u, wxbc_ref[:, cs])
        b_gate = _dot(u, wxbc_ref[:, D_CONV + c * MIX_CHUNK:D_CONV + (c + 1) * MIX_CHUNK])
        c_gate = _dot(u, wxbc_ref[:, 2 * D_CONV + c * MIX_CHUNK:2 * D_CONV + (c + 1) * MIX_CHUNK])
        s_conv = c_gate * x_in
        yc_scr[:, cs] = _short_conv(s_conv, b_gate, convw_ref.at[:, cs],
                                    conv_scr.at[:, cs]).astype(BF16)
    new_prefix = conv_scr[CONV_PAD + tm - 2:CONV_PAD + tm, :]
    conv_scr[CONV_PAD - 2:CONV_PAD, :] = new_prefix

    q = _dot(u, wq_ref[...]) * (HEAD_DIM ** -0.5)
    kt = _dot_nt(wkt_ref[...], u)
    v = _dot(u, wv_ref[...])
    ones = jnp.ones((tm, HEAD_DIM), F32)
    for h in range(N_HEADS):
        hs = slice(h * HEAD_DIM, (h + 1) * HEAD_DIM)
        q_scr[h] = q[:, hs].astype(BF16)
        kt_scr[h, :, ATT_WINDOW:] = kt[hs, :].astype(BF16)
        v_scr[h, ATT_WINDOW:, :] = jnp.concatenate([v[:, hs], ones], axis=-1).astype(BF16)

    @pl.when(i == tiles_per_seq - 1)
    def _():
        kout_ref[0] = _dot(u, wk_ref[...])[tm - ATT_WINDOW:, :]
        vout_ref[0] = v[tm - ATT_WINDOW:, :]
        cout_ref[0] = new_prefix

    def attend(first_tile):
        col = lax.broadcasted_iota(jnp.int32, (1, PAIR_KEYS), 1)

        def scores(h, jj, slot):
            k0 = jj * PAIR
            s = _dot(q_scr[h, k0:k0 + PAIR, :], kt_scr[h, :, k0:k0 + PAIR_KEYS])
            s = s + bias_scr[h]
            if first_tile:
                s = s + jnp.where(col + k0 < ATT_WINDOW, NEG_INF, 0.0)
            s_scr[slot] = s
            return jnp.max(s, axis=-1, keepdims=True)

        def weighted_values(h, jj, slot, m):
            k0 = jj * PAIR
            p = jnp.exp(s_scr[slot] - m).astype(BF16)
            o = _dot(p, v_scr[h, k0:k0 + PAIR_KEYS, :])
            o_scr[h, k0:k0 + PAIR, :] = o[:, :HEAD_DIM] / o[:, HEAD_DIM:]

        def per_head_group(t, carry):
            steps = [(t * HEADS_PER_STEP + hh, jj)
                     for hh in range(HEADS_PER_STEP) for jj in range(n_pairs)]
            maxes = {}
            for k in range(len(steps) + ATT_LOOKAHEAD):
                if k < len(steps):
                    maxes[k] = scores(*steps[k], k % SCORE_SLOTS)
                if k >= ATT_LOOKAHEAD:
                    j = k - ATT_LOOKAHEAD
                    weighted_values(*steps[j], j % SCORE_SLOTS, maxes.pop(j))
            return carry
        lax.fori_loop(0, N_HEADS // HEADS_PER_STEP, per_head_group, 0)

    @pl.when(i == 0)
    def _():
        attend(True)

    @pl.when(i > 0)
    def _():
        attend(False)

    u = u_scr[...]
    y_conv = yc_scr[...]
    o_att = jnp.concatenate([o_scr[h] for h in range(N_HEADS)], axis=-1).astype(BF16)
    for c in range(D_MODEL // MIX_CHUNK):
        cs = slice(c * MIX_CHUNK, (c + 1) * MIX_CHUNK)
        g_conv = _dot(u, wg_ref[:, D_MODEL + c * MIX_CHUNK:D_MODEL + (c + 1) * MIX_CHUNK])
        m_conv = _sigmoid(g_conv) * _dot(y_conv, wco_ref[:, cs])
        g_att = _dot(u, wg_ref[:, cs])
        m_scr[:, cs] = (_sigmoid(g_att) * _dot(o_att, wao_ref[:, cs]) + m_conv).astype(BF16)
    kt_scr[:, :, :ATT_WINDOW] = kt_scr[:, :, tm:]
    v_scr[:, :ATT_WINDOW, :] = v_scr[:, tm:, :]
    h2_ref[...] = h1_ref[...] + _rmsnorm(_dot(m_scr[...], wo_ref[...]), g_ref[3:4, :])


def _mixer_prompt_call(tab, h1, norm_g, wq, wkt, wk, wv, wxbc, wg, convw, wao, wco, wo, batch, seq):
    tm = TOKEN_TILE
    tiles = seq // tm
    row = lambda b, i: (b * tiles + i, 0)
    per_seq = lambda b, i: (b, 0, 0)
    weights = (wq, wkt, wk, wv, wxbc, wg, convw, wao, wco, wo)
    return pl.pallas_call(
        functools.partial(_mixer_prompt_kernel, tm=tm, tiles_per_seq=tiles),
        grid=(batch, tiles),
        in_specs=[pl.BlockSpec(memory_space=pltpu.SMEM),
                  pl.BlockSpec((tm, D_MODEL), row), _resident(norm_g.shape)]
                 + [_resident(w.shape) for w in weights],
        out_specs=[pl.BlockSpec((tm, D_MODEL), row),
                   pl.BlockSpec((1, ATT_WINDOW, D_ATT), per_seq),
                   pl.BlockSpec((1, ATT_WINDOW, D_ATT), per_seq),
                   pl.BlockSpec((1, CONV_WIDTH - 1, D_CONV), per_seq)],
        out_shape=[jax.ShapeDtypeStruct((batch * seq, D_MODEL), F32),
                   jax.ShapeDtypeStruct((batch, ATT_WINDOW, D_ATT), F32),
                   jax.ShapeDtypeStruct((batch, ATT_WINDOW, D_ATT), F32),
                   jax.ShapeDtypeStruct((batch, CONV_WIDTH - 1, D_CONV), F32)],
        scratch_shapes=[pltpu.VMEM((N_HEADS, tm, HEAD_DIM), BF16),
                        pltpu.VMEM((N_HEADS, HEAD_DIM, ATT_WINDOW + tm), BF16),
                        pltpu.VMEM((N_HEADS, ATT_WINDOW + tm, 2 * HEAD_DIM), BF16),
                        pltpu.VMEM((N_HEADS, PAIR, PAIR_KEYS), F32),
                        pltpu.VMEM((SCORE_SLOTS, PAIR, PAIR_KEYS), F32),
                        pltpu.VMEM((N_HEADS, tm, HEAD_DIM), F32),
                        pltpu.VMEM((CONV_PAD + tm, D_CONV), F32),
                        pltpu.VMEM((tm, D_CONV), BF16),
                        pltpu.VMEM((tm, D_MODEL), BF16),
                        pltpu.VMEM((tm, D_MODEL), BF16)],
        compiler_params=_params(2),
        name="mixer_prompt",
    )(tab, h1, norm_g, *weights)


def _inproj_kernel(h1_ref, g_ref, wq_ref, wk_ref, wv_ref, wxbc_ref, wg_ref,
                   q_ref, k_ref, v_ref, xbc_ref, gates_ref):
    u = _rmsnorm(h1_ref[...], g_ref[2:3, :]).astype(BF16)
    q_ref[...] = _dot(u, wq_ref[...]) * (HEAD_DIM ** -0.5)
    k_ref[...] = _dot(u, wk_ref[...])
    v_ref[...] = _dot(u, wv_ref[...])
    xbc_ref[...] = _dot(u, wxbc_ref[...])
    gates_ref[...] = _dot(u, wg_ref[...])


def _inproj_call(h1, norm_g, wq, wk, wv, wxbc, wg):
    n = h1.shape[0]
    ins = (h1, norm_g, wq, wk, wv, wxbc, wg)
    widths = (D_ATT, D_ATT, D_ATT, 3 * D_CONV, 2 * D_MODEL)
    return pl.pallas_call(
        _inproj_kernel,
        grid=(1,),
        in_specs=[_resident(a.shape) for a in ins],
        out_specs=[pl.BlockSpec((n, w), lambda i: (0, 0)) for w in widths],
        out_shape=[jax.ShapeDtypeStruct((n, w), F32) for w in widths],
        compiler_params=_params(1),
        name="inproj_sample",
    )(*ins)


def _attn_sample_kernel(tab_ref, q_ref, k_ref, v_ref, ck_ref, cv_ref, xbc_ref, cconv_ref, convw_ref,
                        o_ref, y_ref, cout_ref, bias_c_scr, bias_n_scr, conv_scr, *, t, lc):
    @pl.when(pl.program_id(0) == 0)
    def _():
        r = lax.broadcasted_iota(jnp.int32, (t, lc), 0)
        c = lax.broadcasted_iota(jnp.int32, (t, lc), 1)
        d_c = jnp.minimum(lc + r - c, MAX_REL)
        r = lax.broadcasted_iota(jnp.int32, (t, t), 0)
        c = lax.broadcasted_iota(jnp.int32, (t, t), 1)
        d_n = jnp.clip(r - c, -MAX_REL, MAX_REL)

        def per_head(h, carry):
            bias_c_scr[h] = _build_bias(tab_ref, h, d_c, 1, MAX_REL)
            bias_n_scr[h] = _build_bias(tab_ref, h, d_n, -min(t - 1, MAX_REL), min(t - 1, MAX_REL))
            return carry
        lax.fori_loop(0, N_HEADS, per_head, 0)

    rows = N_HEADS * t
    row_head = lax.broadcasted_iota(jnp.int32, (N_HEADS, t, D_ATT), 0).reshape(rows, D_ATT)
    col = lax.broadcasted_iota(jnp.int32, (rows, D_ATT), 1)
    own_head = (col >= row_head * HEAD_DIM) & (col < (row_head + 1) * HEAD_DIM)
    q_heads = jnp.where(own_head, jnp.concatenate([q_ref[...]] * N_HEADS, axis=0), 0.0).astype(BF16)
    k_new = k_ref[...].astype(BF16)
    v_new = v_ref[...].astype(BF16)
    s_c = _dot_nt(q_heads, ck_ref[0]) + bias_c_scr[...].reshape(rows, lc)
    s_n = _dot_nt(q_heads, k_new) + bias_n_scr[...].reshape(rows, t)
    m = jnp.maximum(jnp.max(s_c, axis=-1, keepdims=True), jnp.max(s_n, axis=-1, keepdims=True))
    p_c = jnp.exp(s_c - m)
    p_n = jnp.exp(s_n - m)
    l = jnp.sum(p_c, axis=-1, keepdims=True) + jnp.sum(p_n, axis=-1, keepdims=True)
    o_all = (_dot(p_c.astype(BF16), cv_ref[0]) + _dot(p_n.astype(BF16), v_new)) * (1.0 / l)
    o_all = jnp.where(own_head, o_all, 0.0)
    o = o_all[0:t]
    for h in range(1, N_HEADS):
        o = o + o_all[h * t:(h + 1) * t]
    o_ref[...] = o

    xbc = xbc_ref[...]
    s_conv = xbc[:, 2 * D_CONV:] * xbc[:, :D_CONV]
    conv_scr[0:CONV_PAD, :] = jnp.zeros((CONV_PAD, D_CONV), F32)
    conv_scr[CONV_PAD - 2:CONV_PAD, :] = cconv_ref[0]
    y_ref[...] = _short_conv(s_conv, xbc[:, D_CONV:2 * D_CONV], convw_ref, conv_scr)
    cout_ref[0] = conv_scr[CONV_PAD + t - 2:CONV_PAD + t, :]


def _attn_sample_call(tab, q, k, v, cache_k, cache_v, xbc, cache_conv, convw, batch, t):
    lc = cache_k.shape[1]
    row = lambda b: (b, 0)
    per_seq = lambda b: (b, 0, 0)
    return pl.pallas_call(
        functools.partial(_attn_sample_kernel, t=t, lc=lc),
        grid=(batch,),
        in_specs=[pl.BlockSpec(memory_space=pltpu.SMEM),
                  pl.BlockSpec((t, D_ATT), row), pl.BlockSpec((t, D_ATT), row),
                  pl.BlockSpec((t, D_ATT), row),
                  pl.BlockSpec((1, lc, D_ATT), per_seq), pl.BlockSpec((1, lc, D_ATT), per_seq),
                  pl.BlockSpec((t, 3 * D_CONV), row),
                  pl.BlockSpec((1, CONV_WIDTH - 1, D_CONV), per_seq),
                  _resident(convw.shape)],
        out_specs=[pl.BlockSpec((t, D_ATT), row), pl.BlockSpec((t, D_CONV), row),
                   pl.BlockSpec((1, CONV_WIDTH - 1, D_CONV), per_seq)],
        out_shape=[jax.ShapeDtypeStruct((batch * t, D_ATT), F32),
                   jax.ShapeDtypeStruct((batch * t, D_CONV), F32),
                   jax.ShapeDtypeStruct((batch, CONV_WIDTH - 1, D_CONV), F32)],
        scratch_shapes=[pltpu.VMEM((N_HEADS, t, lc), F32),
                        pltpu.VMEM((N_HEADS, t, t), F32),
                        pltpu.VMEM((CONV_PAD + t, D_CONV), F32)],
        compiler_params=_params(1),
        name="attn_sample",
    )(tab, q, k, v, cache_k, cache_v, xbc, cache_conv, convw)


def _mix_kernel(h1_ref, o_ref, y_ref, gates_ref, g_ref, wao_ref, wco_ref, wo_ref, h2_ref):
    h2_ref[...] = _mix_out(h1_ref[...], o_ref[...], y_ref[...], gates_ref[...], g_ref[3:4, :],
                           wao_ref, wco_ref, wo_ref)


def _mix_call(h1, o_att, y_conv, gates, norm_g, wao, wco, wo):
    ins = (h1, o_att, y_conv, gates, norm_g, wao, wco, wo)
    return pl.pallas_call(
        _mix_kernel,
        grid=(1,),
        in_specs=[_resident(a.shape) for a in ins],
        out_specs=pl.BlockSpec(h1.shape, lambda i: (0, 0)),
        out_shape=jax.ShapeDtypeStruct(h1.shape, F32),
        compiler_params=_params(1),
        name="mix_sample",
    )(*ins)


def kernel(x_prompt, x_sample, cache_k, cache_v, cache_conv, p_prompt, p_sample, norm_g,
           w1_gate, w1_up, w1_down, w_in, conv_w, rel_bias, w_att_out, w_conv_out, w_out,
           w2_gate, w2_up, w2_down, w_ple_gate, w_ple_proj):
    depth = norm_g.shape[0]
    assert depth == 1, "one layer per step"
    batch, seq, _ = x_prompt.shape
    dec_batch, dec_seq, _ = x_sample.shape
    assert seq % TOKEN_TILE == 0 and TOKEN_TILE % PAIR == 0 and TOKEN_TILE >= ATT_WINDOW
    l = 0

    g = norm_g[l]
    w1g = w1_gate[l].astype(BF16)
    w1u = w1_up[l].astype(BF16)
    w2g = w2_gate[l].astype(BF16)
    w2u = w2_up[l].astype(BF16)
    w1d = w1_down[l].astype(BF16)
    w2d = w2_down[l].astype(BF16)
    win = w_in[l].astype(BF16)
    wq = win[:, :D_ATT]
    wk = win[:, D_ATT:2 * D_ATT]
    wv = win[:, 2 * D_ATT:3 * D_ATT]
    wxbc = win[:, 3 * D_ATT:3 * D_ATT + 3 * D_CONV]
    wg = win[:, 3 * D_ATT + 3 * D_CONV:]
    wkt = wk.T
    wao = w_att_out[l].astype(BF16)
    wco = w_conv_out[l].astype(BF16)
    wo = w_out[l].astype(BF16)
    wpg = w_ple_gate[l].astype(BF16)
    wpp = w_ple_proj[l].astype(BF16)
    tab = rel_bias[l]
    convw = conv_w[l]

    xp = x_prompt.reshape(batch * seq, D_MODEL)
    h1p = _ffn_call(xp, g, w1g, w1u, w1d, 0, 1)
    h2p, k_p, v_p, c_p = _mixer_prompt_call(tab, h1p, g, wq, wkt, wk, wv, wxbc, wg, convw,
                                            wao, wco, wo, batch, seq)
    y_p = _ffn_ple_call(h2p, p_prompt[l].reshape(batch * seq, D_PLE), g, w2g, w2u, w2d, wpg, wpp,
                        4, 5, 6)

    lc = cache_k.shape[2]
    xs = x_sample.reshape(dec_batch * dec_seq, D_MODEL)
    h1s = _ffn_call(xs, g, w1g, w1u, w1d, 0, 1)
    q_s, k_s, v_s, xbc_s, gates_s = _inproj_call(h1s, g, wq, wk, wv, wxbc, wg)
    o_s, yc_s, c_s = _attn_sample_call(tab, q_s, k_s, v_s,
                                       cache_k[l].reshape(dec_batch, lc, D_ATT).astype(BF16),
                                       cache_v[l].reshape(dec_batch, lc, D_ATT).astype(BF16),
                                       xbc_s, cache_conv[l], convw, dec_batch, dec_seq)
    h2s = _mix_call(h1s, o_s, yc_s, gates_s, g, wao, wco, wo)
    y_s = _ffn_ple_call(h2s, p_sample[l].reshape(dec_batch * dec_seq, D_PLE), g, w2g, w2u, w2d,
                        wpg, wpp, 4, 5, 6)

    kv_p = (1, batch, ATT_WINDOW, N_HEADS, HEAD_DIM)
    kv_s = (1, dec_batch, dec_seq, N_HEADS, HEAD_DIM)
    return (y_p.reshape(batch, seq, D_MODEL), y_s.reshape(dec_batch, dec_seq, D_MODEL),
            k_p.reshape(kv_p), v_p.reshape(kv_p), c_p[None],
            k_s.reshape(kv_s), v_s.reshape(kv_s), c_s[None])
```

```python
import functools

import jax
import jax.numpy as jnp
from jax import lax
from jax.experimental import pallas as pl
from jax.experimental.pallas import tpu as pltpu

F32 = jnp.float32
BF16 = jnp.bfloat16

D_MODEL = 1024
CHUNK = 64
LEFT_CHUNKS = 8
ATT_WINDOW = LEFT_CHUNKS * CHUNK
BAND = ATT_WINDOW + CHUNK
N_HEADS = 8
HEAD_DIM = 64
D_ATT = N_HEADS * HEAD_DIM
D_CONV = D_MODEL // 2
CONV_WIDTH = 3
MAX_REL = 128
D_FF = 2816
D_PLE = 256
EPS = 1e-6
NEG_INF = -1e30

FF_CHUNK = 256
N_FF_CHUNKS = D_FF // FF_CHUNK
MIX_CHUNK = 256
PAIR = 2 * CHUNK
PAIR_KEYS = ATT_WINDOW + PAIR
HEADS_PER_STEP = 8
ATT_LOOKAHEAD = 3
SCORE_SLOTS = ATT_LOOKAHEAD + 2
TOKEN_TILE = 512
CONV_PAD = 8
VMEM_LIMIT_BYTES = 56 * 1024 * 1024


def _dot(a, b):
    return jnp.dot(a, b, preferred_element_type=F32)


def _dot_nt(a, b):
    return lax.dot_general(a, b, (((1,), (1,)), ((), ())), preferred_element_type=F32)


def _rmsnorm(x, g):
    return x * lax.rsqrt(jnp.mean(x * x, axis=-1, keepdims=True) + EPS) * g


def _sigmoid(x):
    return 1.0 / (1.0 + jnp.exp(-x))


def _resident(shape):
    return pl.BlockSpec(shape, lambda *_: (0,) * len(shape), pipeline_mode=pl.Buffered(1))


def _params(n_grid_dims):
    return pltpu.CompilerParams(
        dimension_semantics=("arbitrary",) * n_grid_dims,
        vmem_limit_bytes=VMEM_LIMIT_BYTES,
    )


def _swiglu_hidden(u, wg_ref, wu_ref, act_ref, chunks):
    for c in chunks:
        cs = slice(c * FF_CHUNK, (c + 1) * FF_CHUNK)
        gate = _dot(u, wg_ref[:, cs])
        up = _dot(u, wu_ref[:, cs])
        act_ref[:, cs] = (gate * _sigmoid(gate) * up).astype(BF16)


def _ple(h, p, g_ple, wpg_ref, wpp_ref):
    gate = _sigmoid(_dot(h.astype(BF16), wpg_ref[...]))
    proj = _dot(p.astype(BF16), wpp_ref[...])
    return h + _rmsnorm(gate * proj, g_ple)


def _ffn_body(x, g_pre, g_post, wg_ref, wu_ref, wd_ref, act_scr):
    u = _rmsnorm(x, g_pre).astype(BF16)
    _swiglu_hidden(u, wg_ref, wu_ref, act_scr, range(N_FF_CHUNKS))
    y = _dot(act_scr[...], wd_ref[...])
    return x + 0.5 * _rmsnorm(y, g_post)


def _ffn_kernel(x_ref, g_ref, wg_ref, wu_ref, wd_ref, o_ref, act_scr, *, pre, post):
    o_ref[...] = _ffn_body(x_ref[...], g_ref[pre:pre + 1, :], g_ref[post:post + 1, :],
                           wg_ref, wu_ref, wd_ref, act_scr)


def _ffn_ple_kernel(x_ref, p_ref, g_ref, wg_ref, wu_ref, wd_ref, wpg_ref, wpp_ref, o_ref, act_scr,
                    *, pre, post, ple):
    h = _ffn_body(x_ref[...], g_ref[pre:pre + 1, :], g_ref[post:post + 1, :],
                  wg_ref, wu_ref, wd_ref, act_scr)
    o_ref[...] = _ple(h, p_ref[...], g_ref[ple:ple + 1, :], wpg_ref, wpp_ref)


def _ffn_call(x, norm_g, wg, wu, wd, pre, post):
    n = x.shape[0]
    tm = min(TOKEN_TILE, n)
    row = lambda i: (i, 0)
    return pl.pallas_call(
        functools.partial(_ffn_kernel, pre=pre, post=post),
        grid=(n // tm,),
        in_specs=[pl.BlockSpec((tm, D_MODEL), row), _resident(norm_g.shape),
                  _resident(wg.shape), _resident(wu.shape), _resident(wd.shape)],
        out_specs=pl.BlockSpec((tm, D_MODEL), row),
        out_shape=jax.ShapeDtypeStruct((n, D_MODEL), F32),
        scratch_shapes=[pltpu.VMEM((tm, D_FF), BF16)],
        compiler_params=_params(1),
        name="ffn",
    )(x, norm_g, wg, wu, wd)


def _ffn_ple_call(x, p, norm_g, wg, wu, wd, wpg, wpp, pre, post, ple):
    n = x.shape[0]
    tm = min(TOKEN_TILE, n)
    row = lambda i: (i, 0)
    return pl.pallas_call(
        functools.partial(_ffn_ple_kernel, pre=pre, post=post, ple=ple),
        grid=(n // tm,),
        in_specs=[pl.BlockSpec((tm, D_MODEL), row), pl.BlockSpec((tm, D_PLE), row),
                  _resident(norm_g.shape), _resident(wg.shape), _resident(wu.shape),
                  _resident(wd.shape), _resident(wpg.shape), _resident(wpp.shape)],
        out_specs=pl.BlockSpec((tm, D_MODEL), row),
        out_shape=jax.ShapeDtypeStruct((n, D_MODEL), F32),
        scratch_shapes=[pltpu.VMEM((tm, D_FF), BF16)],
        compiler_params=_params(1),
        name="ffn_ple",
    )(x, p, norm_g, wg, wu, wd, wpg, wpp)


def _build_bias(tab_ref, head, dist, lo, hi):
    def body(j, acc):
        v = lo + j
        return jnp.where(dist == v, tab_ref[head, v + MAX_REL], acc)
    return lax.fori_loop(0, hi - lo + 1, body, jnp.full(dist.shape, NEG_INF, F32))


def _short_conv(s, bgate, convw_ref, conv_scr):
    t = s.shape[0]
    conv_scr[CONV_PAD:CONV_PAD + t, :] = s
    y = (convw_ref[0:1, :] * conv_scr[CONV_PAD - 2:CONV_PAD - 2 + t, :]
         + convw_ref[1:2, :] * conv_scr[CONV_PAD - 1:CONV_PAD - 1 + t, :]
         + convw_ref[2:3, :] * s)
    return bgate * y


def _mix_out(h1, o_att, y_conv, gates, g_post, wao_ref, wco_ref, wo_ref):
    m = (_sigmoid(gates[:, :D_MODEL]) * _dot(o_att.astype(BF16), wao_ref[...])
         + _sigmoid(gates[:, D_MODEL:]) * _dot(y_conv.astype(BF16), wco_ref[...]))
    return h1 + _rmsnorm(_dot(m.astype(BF16), wo_ref[...]), g_post)


def _mixer_prompt_kernel(tab_ref, h1_ref, g_ref, wq_ref, wkt_ref, wk_ref, wv_ref, wxbc_ref, wg_ref,
                         convw_ref, wao_ref, wco_ref, wo_ref,
                         h2_ref, kout_ref, vout_ref, cout_ref,
                         q_scr, kt_scr, v_scr, bias_scr, s_scr, o_scr, conv_scr, yc_scr, u_scr, m_scr,
                         *, tm, tiles_per_seq):
    b = pl.program_id(0)
    i = pl.program_id(1)
    n_pairs = tm // PAIR

    @pl.when((b == 0) & (i == 0))
    def _():
        rows = 32
        for rb in range(PAIR // rows):
            r = lax.broadcasted_iota(jnp.int32, (rows, PAIR_KEYS), 0) + rb * rows
            c = lax.broadcasted_iota(jnp.int32, (rows, PAIR_KEYS), 1)
            n = c - (r & CHUNK)
            d = jnp.minimum((r & (CHUNK - 1)) + ATT_WINDOW - n, MAX_REL)
            d = jnp.where((n >= 0) & (n < BAND), d, MAX_REL + 1)

            def per_head(h, carry):
                bias_scr[h, rb * rows:(rb + 1) * rows, :] = _build_bias(
                    tab_ref, h, d, -(CHUNK - 1), MAX_REL)
                return carry
            lax.fori_loop(0, N_HEADS, per_head, 0)

    @pl.when(i == 0)
    def _():
        kt_scr[:, :, :ATT_WINDOW] = jnp.zeros((N_HEADS, HEAD_DIM, ATT_WINDOW), BF16)
        v_scr[:, :ATT_WINDOW, :] = jnp.zeros((N_HEADS, ATT_WINDOW, 2 * HEAD_DIM), BF16)
        conv_scr[0:CONV_PAD, :] = jnp.zeros((CONV_PAD, D_CONV), F32)

    u = _rmsnorm(h1_ref[...], g_ref[2:3, :]).astype(BF16)
    u_scr[...] = u

    for c in range(D_CONV // MIX_CHUNK):
        cs = slice(c * MIX_CHUNK, (c + 1) * MIX_CHUNK)
        x_in = _dot(u, wxbc_ref[:, cs])
        b_gate = _dot(u, wxbc_ref[:, D_CONV + c * MIX_CHUNK:D_CONV + (c + 1) * MIX_CHUNK])
        c_gate = _dot(u, wxbc_ref[:, 2 * D_CONV + c * MIX_CHUNK:2 * D_CONV + (c + 1) * MIX_CHUNK])
        s_conv = c_gate * x_in
        yc_scr[:, cs] = _short_conv(s_conv, b_gate, convw_ref.at[:, cs],
                                    conv_scr.at[:, cs]).astype(BF16)
    new_prefix = conv_scr[CONV_PAD + tm - 2:CONV_PAD + tm, :]
    conv_scr[CONV_PAD - 2:CONV_PAD, :] = new_prefix

    q = _dot(u, wq_ref[...]) * (HEAD_DIM ** -0.5)
    kt = _dot_nt(wkt_ref[...], u)
    v = _dot(u, wv_ref[...])
    ones = jnp.ones((tm, HEAD_DIM), F32)
    for h in range(N_HEADS):
        hs = slice(h * HEAD_DIM, (h + 1) * HEAD_DIM)
        q_scr[h] = q[:, hs].astype(BF16)
        kt_scr[h, :, ATT_WINDOW:] = kt[hs, :].astype(BF16)
        v_scr[h, ATT_WINDOW:, :] = jnp.concatenate([v[:, hs], ones], axis=-1).astype(BF16)

    @pl.when(i == tiles_per_seq - 1)
    def _():
        kout_ref[0] = _dot(u, wk_ref[...])[tm - ATT_WINDOW:, :]
        vout_ref[0] = v[tm - ATT_WINDOW:, :]
        cout_ref[0] = new_prefix

    def attend(first_tile):
        col = lax.broadcasted_iota(jnp.int32, (1, PAIR_KEYS), 1)

        def scores(h, jj, slot):
            k0 = jj * PAIR
            s = _dot(q_scr[h, k0:k0 + PAIR, :], kt_scr[h, :, k0:k0 + PAIR_KEYS])
            s = s + bias_scr[h]
            if first_tile:
                s = s + jnp.where(col + k0 < ATT_WINDOW, NEG_INF, 0.0)
            s_scr[slot] = s
            return jnp.max(s, axis=-1, keepdims=True)

        def weighted_values(h, jj, slot, m):
            k0 = jj * PAIR
            p = jnp.exp(s_scr[slot] - m).astype(BF16)
            o = _dot(p, v_scr[h, k0:k0 + PAIR_KEYS, :])
            o_scr[h, k0:k0 + PAIR, :] = o[:, :HEAD_DIM] / o[:, HEAD_DIM:]

        def per_head_group(t, carry):
            steps = [(t * HEADS_PER_STEP + hh, jj)
                     for hh in range(HEADS_PER_STEP) for jj in range(n_pairs)]
            maxes = {}
            for k in range(len(steps) + ATT_LOOKAHEAD):
                if k < len(steps):
                    maxes[k] = scores(*steps[k], k % SCORE_SLOTS)
                if k >= ATT_LOOKAHEAD:
                    j = k - ATT_LOOKAHEAD
                    weighted_values(*steps[j], j % SCORE_SLOTS, maxes.pop(j))
            return carry
        lax.fori_loop(0, N_HEADS // HEADS_PER_STEP, per_head_group, 0)

    @pl.when(i == 0)
    def _():
        attend(True)

    @pl.when(i > 0)
    def _():
        attend(False)

    u = u_scr[...]
    y_conv = yc_scr[...]
    o_att = jnp.concatenate([o_scr[h] for h in range(N_HEADS)], axis=-1).astype(BF16)
    for c in range(D_MODEL // MIX_CHUNK):
        cs = slice(c * MIX_CHUNK, (c + 1) * MIX_CHUNK)
        g_conv = _dot(u, wg_ref[:, D_MODEL + c * MIX_CHUNK:D_MODEL + (c + 1) * MIX_CHUNK])
        m_conv = _sigmoid(g_conv) * _dot(y_conv, wco_ref[:, cs])
        g_att = _dot(u, wg_ref[:, cs])
        m_scr[:, cs] = (_sigmoid(g_att) * _dot(o_att, wao_ref[:, cs]) + m_conv).astype(BF16)
    kt_scr[:, :, :ATT_WINDOW] = kt_scr[:, :, tm:]
    v_scr[:, :ATT_WINDOW, :] = v_scr[:, tm:, :]
    h2_ref[...] = h1_ref[...] + _rmsnorm(_dot(m_scr[...], wo_ref[...]), g_ref[3:4, :])


def _mixer_prompt_call(tab, h1, norm_g, wq, wkt, wk, wv, wxbc, wg, convw, wao, wco, wo, batch, seq):
    tm = TOKEN_TILE
    tiles = seq // tm
    row = lambda b, i: (b * tiles + i, 0)
    per_seq = lambda b, i: (b, 0, 0)
    weights = (wq, wkt, wk, wv, wxbc, wg, convw, wao, wco, wo)
    return pl.pallas_call(
        functools.partial(_mixer_prompt_kernel, tm=tm, tiles_per_seq=tiles),
        grid=(batch, tiles),
        in_specs=[pl.BlockSpec(memory_space=pltpu.SMEM),
                  pl.BlockSpec((tm, D_MODEL), row), _resident(norm_g.shape)]
                 + [_resident(w.shape) for w in weights],
        out_specs=[pl.BlockSpec((tm, D_MODEL), row),
                   pl.BlockSpec((1, ATT_WINDOW, D_ATT), per_seq),
                   pl.BlockSpec((1, ATT_WINDOW, D_ATT), per_seq),
                   pl.BlockSpec((1, CONV_WIDTH - 1, D_CONV), per_seq)],
        out_shape=[jax.ShapeDtypeStruct((batch * seq, D_MODEL), F32),
                   jax.ShapeDtypeStruct((batch, ATT_WINDOW, D_ATT), F32),
                   jax.ShapeDtypeStruct((batch, ATT_WINDOW, D_ATT), F32),
                   jax.ShapeDtypeStruct((batch, CONV_WIDTH - 1, D_CONV), F32)],
        scratch_shapes=[pltpu.VMEM((N_HEADS, tm, HEAD_DIM), BF16),
                        pltpu.VMEM((N_HEADS, HEAD_DIM, ATT_WINDOW + tm), BF16),
                        pltpu.VMEM((N_HEADS, ATT_WINDOW + tm, 2 * HEAD_DIM), BF16),
                        pltpu.VMEM((N_HEADS, PAIR, PAIR_KEYS), F32),
                        pltpu.VMEM((SCORE_SLOTS, PAIR, PAIR_KEYS), F32),
                        pltpu.VMEM((N_HEADS, tm, HEAD_DIM), F32),
                        pltpu.VMEM((CONV_PAD + tm, D_CONV), F32),
                        pltpu.VMEM((tm, D_CONV), BF16),
                        pltpu.VMEM((tm, D_MODEL), BF16),
                        pltpu.VMEM((tm, D_MODEL), BF16)],
        compiler_params=_params(2),
        name="mixer_prompt",
    )(tab, h1, norm_g, *weights)


def _inproj_kernel(h1_ref, g_ref, wq_ref, wk_ref, wv_ref, wxbc_ref, wg_ref,
                   q_ref, k_ref, v_ref, xbc_ref, gates_ref):
    u = _rmsnorm(h1_ref[...], g_ref[2:3, :]).astype(BF16)
    q_ref[...] = _dot(u, wq_ref[...]) * (HEAD_DIM ** -0.5)
    k_ref[...] = _dot(u, wk_ref[...])
    v_ref[...] = _dot(u, wv_ref[...])
    xbc_ref[...] = _dot(u, wxbc_ref[...])
    gates_ref[...] = _dot(u, wg_ref[...])


def _inproj_call(h1, norm_g, wq, wk, wv, wxbc, wg):
    n = h1.shape[0]
    ins = (h1, norm_g, wq, wk, wv, wxbc, wg)
    widths = (D_ATT, D_ATT, D_ATT, 3 * D_CONV, 2 * D_MODEL)
    return pl.pallas_call(
        _inproj_kernel,
        grid=(1,),
        in_specs=[_resident(a.shape) for a in ins],
        out_specs=[pl.BlockSpec((n, w), lambda i: (0, 0)) for w in widths],
        out_shape=[jax.ShapeDtypeStruct((n, w), F32) for w in widths],
        compiler_params=_params(1),
        name="inproj_sample",
    )(*ins)


def _attn_sample_kernel(tab_ref, q_ref, k_ref, v_ref, ck_ref, cv_ref, xbc_ref, cconv_ref, convw_ref,
                        o_ref, y_ref, cout_ref, bias_c_scr, bias_n_scr, conv_scr, *, t, lc):
    @pl.when(pl.program_id(0) == 0)
    def _():
        r = lax.broadcasted_iota(jnp.int32, (t, lc), 0)
        c = lax.broadcasted_iota(jnp.int32, (t, lc), 1)
        d_c = jnp.minimum(lc + r - c, MAX_REL)
        r = lax.broadcasted_iota(jnp.int32, (t, t), 0)
        c = lax.broadcasted_iota(jnp.int32, (t, t), 1)
        d_n = jnp.clip(r - c, -MAX_REL, MAX_REL)

        def per_head(h, carry):
            bias_c_scr[h] = _build_bias(tab_ref, h, d_c, 1, MAX_REL)
            bias_n_scr[h] = _build_bias(tab_ref, h, d_n, -min(t - 1, MAX_REL), min(t - 1, MAX_REL))
            return carry
        lax.fori_loop(0, N_HEADS, per_head, 0)

    rows = N_HEADS * t
    row_head = lax.broadcasted_iota(jnp.int32, (N_HEADS, t, D_ATT), 0).reshape(rows, D_ATT)
    col = lax.broadcasted_iota(jnp.int32, (rows, D_ATT), 1)
    own_head = (col >= row_head * HEAD_DIM) & (col < (row_head + 1) * HEAD_DIM)
    q_heads = jnp.where(own_head, jnp.concatenate([q_ref[...]] * N_HEADS, axis=0), 0.0).astype(BF16)
    k_new = k_ref[...].astype(BF16)
    v_new = v_ref[...].astype(BF16)
    s_c = _dot(q_heads, ck_ref[0].astype(BF16)) + bias_c_scr[...].reshape(rows, lc)
    s_n = _dot_nt(q_heads, k_new) + bias_n_scr[...].reshape(rows, t)
    m = jnp.maximum(jnp.max(s_c, axis=-1, keepdims=True), jnp.max(s_n, axis=-1, keepdims=True))
    p_c = jnp.exp(s_c - m)
    p_n = jnp.exp(s_n - m)
    l = jnp.sum(p_c, axis=-1, keepdims=True) + jnp.sum(p_n, axis=-1, keepdims=True)
    o_all = (_dot_nt(p_c.astype(BF16), cv_ref[0].astype(BF16))
             + _dot(p_n.astype(BF16), v_new)) * (1.0 / l)
    o_all = jnp.where(own_head, o_all, 0.0)
    o = o_all[0:t]
    for h in range(1, N_HEADS):
        o = o + o_all[h * t:(h + 1) * t]
    o_ref[...] = o

    xbc = xbc_ref[...]
    s_conv = xbc[:, 2 * D_CONV:] * xbc[:, :D_CONV]
    conv_scr[0:CONV_PAD, :] = jnp.zeros((CONV_PAD, D_CONV), F32)
    conv_scr[CONV_PAD - 2:CONV_PAD, :] = cconv_ref[0]
    y_ref[...] = _short_conv(s_conv, xbc[:, D_CONV:2 * D_CONV], convw_ref, conv_scr)
    cout_ref[0] = conv_scr[CONV_PAD + t - 2:CONV_PAD + t, :]


def _attn_sample_call(tab, q, k, v, cache_k, cache_v, xbc, cache_conv, convw, batch, t):
    lc = cache_k.shape[1]
    row = lambda b: (b, 0)
    per_seq = lambda b: (b, 0, 0)
    cache_k = cache_k.transpose(0, 2, 3, 1).reshape(batch, D_ATT, lc)
    cache_v = cache_v.transpose(0, 2, 3, 1).reshape(batch, D_ATT, lc)
    cache_spec = pl.BlockSpec((1, D_ATT, lc), per_seq)
    return pl.pallas_call(
        functools.partial(_attn_sample_kernel, t=t, lc=lc),
        grid=(batch,),
        in_specs=[pl.BlockSpec(memory_space=pltpu.SMEM),
                  pl.BlockSpec((t, D_ATT), row), pl.BlockSpec((t, D_ATT), row),
                  pl.BlockSpec((t, D_ATT), row),
                  cache_spec, cache_spec,
                  pl.BlockSpec((t, 3 * D_CONV), row),
                  pl.BlockSpec((1, CONV_WIDTH - 1, D_CONV), per_seq),
                  _resident(convw.shape)],
        out_specs=[pl.BlockSpec((t, D_ATT), row), pl.BlockSpec((t, D_CONV), row),
                   pl.BlockSpec((1, CONV_WIDTH - 1, D_CONV), per_seq)],
        out_shape=[jax.ShapeDtypeStruct((batch * t, D_ATT), F32),
                   jax.ShapeDtypeStruct((batch * t, D_CONV), F32),
                   jax.ShapeDtypeStruct((batch, CONV_WIDTH - 1, D_CONV), F32)],
        scratch_shapes=[pltpu.VMEM((N_HEADS, t, lc), F32),
                        pltpu.VMEM((N_HEADS, t, t), F32),
                        pltpu.VMEM((CONV_PAD + t, D_CONV), F32)],
        compiler_params=_params(1),
        name="attn_sample",
    )(tab, q, k, v, cache_k, cache_v, xbc, cache_conv, convw)


def _mix_kernel(h1_ref, o_ref, y_ref, gates_ref, g_ref, wao_ref, wco_ref, wo_ref, h2_ref):
    h2_ref[...] = _mix_out(h1_ref[...], o_ref[...], y_ref[...], gates_ref[...], g_ref[3:4, :],
                           wao_ref, wco_ref, wo_ref)


def _mix_call(h1, o_att, y_conv, gates, norm_g, wao, wco, wo):
    ins = (h1, o_att, y_conv, gates, norm_g, wao, wco, wo)
    return pl.pallas_call(
        _mix_kernel,
        grid=(1,),
        in_specs=[_resident(a.shape) for a in ins],
        out_specs=pl.BlockSpec(h1.shape, lambda i: (0, 0)),
        out_shape=jax.ShapeDtypeStruct(h1.shape, F32),
        compiler_params=_params(1),
        name="mix_sample",
    )(*ins)


def kernel(x_prompt, x_sample, cache_k, cache_v, cache_conv, p_prompt, p_sample, norm_g,
           w1_gate, w1_up, w1_down, w_in, conv_w, rel_bias, w_att_out, w_conv_out, w_out,
           w2_gate, w2_up, w2_down, w_ple_gate, w_ple_proj):
    depth = norm_g.shape[0]
    assert depth == 1, "one layer per step"
    batch, seq, _ = x_prompt.shape
    dec_batch, dec_seq, _ = x_sample.shape
    assert seq % TOKEN_TILE == 0 and TOKEN_TILE % PAIR == 0 and TOKEN_TILE >= ATT_WINDOW
    l = 0

    g = norm_g[l]
    w1g = w1_gate[l].astype(BF16)
    w1u = w1_up[l].astype(BF16)
    w2g = w2_gate[l].astype(BF16)
    w2u = w2_up[l].astype(BF16)
    w1d = w1_down[l].astype(BF16)
    w2d = w2_down[l].astype(BF16)
    win = w_in[l].astype(BF16)
    wq = win[:, :D_ATT]
    wk = win[:, D_ATT:2 * D_ATT]
    wv = win[:, 2 * D_ATT:3 * D_ATT]
    wxbc = win[:, 3 * D_ATT:3 * D_ATT + 3 * D_CONV]
    wg = win[:, 3 * D_ATT + 3 * D_CONV:]
    wkt = wk.T
    wao = w_att_out[l].astype(BF16)
    wco = w_conv_out[l].astype(BF16)
    wo = w_out[l].astype(BF16)
    wpg = w_ple_gate[l].astype(BF16)
    wpp = w_ple_proj[l].astype(BF16)
    tab = rel_bias[l]
    convw = conv_w[l]

    xp = x_prompt.reshape(batch * seq, D_MODEL)
    h1p = _ffn_call(xp, g, w1g, w1u, w1d, 0, 1)
    h2p, k_p, v_p, c_p = _mixer_prompt_call(tab, h1p, g, wq, wkt, wk, wv, wxbc, wg, convw,
                                            wao, wco, wo, batch, seq)
    y_p = _ffn_ple_call(h2p, p_prompt[l].reshape(batch * seq, D_PLE), g, w2g, w2u, w2d, wpg, wpp,
                        4, 5, 6)

    lc = cache_k.shape[2]
    xs = x_sample.reshape(dec_batch * dec_seq, D_MODEL)
    h1s = _ffn_call(xs, g, w1g, w1u, w1d, 0, 1)
    q_s, k_s, v_s, xbc_s, gates_s = _inproj_call(h1s, g, wq, wk, wv, wxbc, wg)
    o_s, yc_s, c_s = _attn_sample_call(tab, q_s, k_s, v_s,
                                       cache_k[l], cache_v[l],
                                       xbc_s, cache_conv[l], convw, dec_batch, dec_seq)
    h2s = _mix_call(h1s, o_s, yc_s, gates_s, g, wao, wco, wo)
    y_s = _ffn_ple_call(h2s, p_sample[l].reshape(dec_batch * dec_seq, D_PLE), g, w2g, w2u, w2d,
                        wpg, wpp, 4, 5, 6)

    kv_p = (1, batch, ATT_WINDOW, N_HEADS, HEAD_DIM)
    kv_s = (1, dec_batch, dec_seq, N_HEADS, HEAD_DIM)
    return (y_p.reshape(batch, seq, D_MODEL), y_s.reshape(dec_batch, dec_seq, D_MODEL),
            k_p.reshape(kv_p), v_p.reshape(kv_p), c_p[None],
            k_s.reshape(kv_s), v_s.reshape(kv_s), c_s[None])
```

```python
import functools

import jax
import jax.numpy as jnp
from jax import lax
from jax.experimental import pallas as pl
from jax.experimental.pallas import tpu as pltpu

F32 = jnp.float32
BF16 = jnp.bfloat16

D_MODEL = 1024
CHUNK = 64
LEFT_CHUNKS = 8
ATT_WINDOW = LEFT_CHUNKS * CHUNK
BAND = ATT_WINDOW + CHUNK
N_HEADS = 8
HEAD_DIM = 64
D_ATT = N_HEADS * HEAD_DIM
D_CONV = D_MODEL // 2
CONV_WIDTH = 3
MAX_REL = 128
D_FF = 2816
D_PLE = 256
EPS = 1e-6
NEG_INF = -1e30

FF_CHUNK = 256
N_FF_CHUNKS = D_FF // FF_CHUNK
MIX_CHUNK = 256
PAIR = 2 * CHUNK
PAIR_KEYS = ATT_WINDOW + PAIR
HEADS_PER_STEP = 8
ATT_LOOKAHEAD = 3
SCORE_SLOTS = ATT_LOOKAHEAD + 2
TOKEN_TILE = 512
FFN_TOKEN_TILE = 1024
CONV_PAD = 8
VMEM_LIMIT_BYTES = 56 * 1024 * 1024


def _dot(a, b):
    return jnp.dot(a, b, preferred_element_type=F32)


def _dot_nt(a, b):
    return lax.dot_general(a, b, (((1,), (1,)), ((), ())), preferred_element_type=F32)


def _rmsnorm(x, g):
    return x * lax.rsqrt(jnp.mean(x * x, axis=-1, keepdims=True) + EPS) * g


def _sigmoid(x):
    return 1.0 / (1.0 + jnp.exp(-x))


def _resident(shape):
    return pl.BlockSpec(shape, lambda *_: (0,) * len(shape), pipeline_mode=pl.Buffered(1))


def _params(n_grid_dims):
    return pltpu.CompilerParams(
        dimension_semantics=("arbitrary",) * n_grid_dims,
        vmem_limit_bytes=VMEM_LIMIT_BYTES,
    )


def _swiglu_hidden(u, wg_ref, wu_ref, act_ref, chunks):
    for c in chunks:
        cs = slice(c * FF_CHUNK, (c + 1) * FF_CHUNK)
        gate = _dot(u, wg_ref[:, cs])
        up = _dot(u, wu_ref[:, cs])
        act_ref[:, cs] = (gate * _sigmoid(gate) * up).astype(BF16)


def _ple(h, p, g_ple, wpg_ref, wpp_ref):
    gate = _sigmoid(_dot(h.astype(BF16), wpg_ref[...]))
    proj = _dot(p.astype(BF16), wpp_ref[...])
    return h + _rmsnorm(gate * proj, g_ple)


def _ffn_body(x, g_pre, g_post, wg_ref, wu_ref, wd_ref, act_scr):
    u = _rmsnorm(x, g_pre).astype(BF16)
    _swiglu_hidden(u, wg_ref, wu_ref, act_scr, range(N_FF_CHUNKS))
    y = _dot(act_scr[...], wd_ref[...])
    return x + 0.5 * _rmsnorm(y, g_post)


def _ffn_kernel(x_ref, g_ref, wg_ref, wu_ref, wd_ref, o_ref, act_scr, *, pre, post):
    o_ref[...] = _ffn_body(x_ref[...], g_ref[pre:pre + 1, :], g_ref[post:post + 1, :],
                           wg_ref, wu_ref, wd_ref, act_scr)


def _ffn_ple_kernel(x_ref, p_ref, g_ref, wg_ref, wu_ref, wd_ref, wpg_ref, wpp_ref, o_ref, act_scr,
                    *, pre, post, ple):
    h = _ffn_body(x_ref[...], g_ref[pre:pre + 1, :], g_ref[post:post + 1, :],
                  wg_ref, wu_ref, wd_ref, act_scr)
    o_ref[...] = _ple(h, p_ref[...], g_ref[ple:ple + 1, :], wpg_ref, wpp_ref)


def _ffn_call(x, norm_g, wg, wu, wd, pre, post):
    n = x.shape[0]
    tm = min(FFN_TOKEN_TILE, n)
    row = lambda i: (i, 0)
    return pl.pallas_call(
        functools.partial(_ffn_kernel, pre=pre, post=post),
        grid=(n // tm,),
        in_specs=[pl.BlockSpec((tm, D_MODEL), row), _resident(norm_g.shape),
                  _resident(wg.shape), _resident(wu.shape), _resident(wd.shape)],
        out_specs=pl.BlockSpec((tm, D_MODEL), row),
        out_shape=jax.ShapeDtypeStruct((n, D_MODEL), F32),
        scratch_shapes=[pltpu.VMEM((tm, D_FF), BF16)],
        compiler_params=_params(1),
        name="ffn",
    )(x, norm_g, wg, wu, wd)


def _ffn_ple_call(x, p, norm_g, wg, wu, wd, wpg, wpp, pre, post, ple):
    n = x.shape[0]
    tm = min(FFN_TOKEN_TILE, n)
    row = lambda i: (i, 0)
    return pl.pallas_call(
        functools.partial(_ffn_ple_kernel, pre=pre, post=post, ple=ple),
        grid=(n // tm,),
        in_specs=[pl.BlockSpec((tm, D_MODEL), row), pl.BlockSpec((tm, D_PLE), row),
                  _resident(norm_g.shape), _resident(wg.shape), _resident(wu.shape),
                  _resident(wd.shape), _resident(wpg.shape), _resident(wpp.shape)],
        out_specs=pl.BlockSpec((tm, D_MODEL), row),
        out_shape=jax.ShapeDtypeStruct((n, D_MODEL), F32),
        scratch_shapes=[pltpu.VMEM((tm, D_FF), BF16)],
        compiler_params=_params(1),
        name="ffn_ple",
    )(x, p, norm_g, wg, wu, wd, wpg, wpp)


def _build_bias(tab_ref, head, dist, lo, hi):
    def body(j, acc):
        v = lo + j
        return jnp.where(dist == v, tab_ref[head, v + MAX_REL], acc)
    return lax.fori_loop(0, hi - lo + 1, body, jnp.full(dist.shape, NEG_INF, F32))


def _short_conv(s, bgate, convw_ref, conv_scr):
    t = s.shape[0]
    conv_scr[CONV_PAD:CONV_PAD + t, :] = s
    y = (convw_ref[0:1, :] * conv_scr[CONV_PAD - 2:CONV_PAD - 2 + t, :]
         + convw_ref[1:2, :] * conv_scr[CONV_PAD - 1:CONV_PAD - 1 + t, :]
         + convw_ref[2:3, :] * s)
    return bgate * y


def _mix_out(h1, o_att, y_conv, gates, g_post, wao_ref, wco_ref, wo_ref):
    m = (_sigmoid(gates[:, :D_MODEL]) * _dot(o_att.astype(BF16), wao_ref[...])
         + _sigmoid(gates[:, D_MODEL:]) * _dot(y_conv.astype(BF16), wco_ref[...]))
    return h1 + _rmsnorm(_dot(m.astype(BF16), wo_ref[...]), g_post)


def _mixer_prompt_kernel(tab_ref, h1_ref, g_ref, wq_ref, wkt_ref, wk_ref, wv_ref, wxbc_ref, wg_ref,
                         convw_ref, wao_ref, wco_ref, wo_ref,
                         h2_ref, kout_ref, vout_ref, cout_ref,
                         q_scr, kt_scr, v_scr, bias_scr, s_scr, o_scr, conv_scr, yc_scr, u_scr, m_scr,
                         *, tm, tiles_per_seq):
    b = pl.program_id(0)
    i = pl.program_id(1)
    n_pairs = tm // PAIR

    @pl.when((b == 0) & (i == 0))
    def _():
        rows = 32
        for rb in range(PAIR // rows):
            r = lax.broadcasted_iota(jnp.int32, (rows, PAIR_KEYS), 0) + rb * rows
            c = lax.broadcasted_iota(jnp.int32, (rows, PAIR_KEYS), 1)
            n = c - (r & CHUNK)
            d = jnp.minimum((r & (CHUNK - 1)) + ATT_WINDOW - n, MAX_REL)
            d = jnp.where((n >= 0) & (n < BAND), d, MAX_REL + 1)

            def per_head(h, carry):
                bias_scr[h, rb * rows:(rb + 1) * rows, :] = _build_bias(
                    tab_ref, h, d, -(CHUNK - 1), MAX_REL)
                return carry
            lax.fori_loop(0, N_HEADS, per_head, 0)

    @pl.when(i == 0)
    def _():
        kt_scr[:, :, :ATT_WINDOW] = jnp.zeros((N_HEADS, HEAD_DIM, ATT_WINDOW), BF16)
        v_scr[:, :ATT_WINDOW, :] = jnp.zeros((N_HEADS, ATT_WINDOW, 2 * HEAD_DIM), BF16)
        conv_scr[0:CONV_PAD, :] = jnp.zeros((CONV_PAD, D_CONV), F32)

    u = _rmsnorm(h1_ref[...], g_ref[2:3, :]).astype(BF16)
    u_scr[...] = u

    for c in range(D_CONV // MIX_CHUNK):
        cs = slice(c * MIX_CHUNK, (c + 1) * MIX_CHUNK)
        x_in = _dot(u, wxbc_ref[:, cs])
        b_gate = _dot(u, wxbc_ref[:, D_CONV + c * MIX_CHUNK:D_CONV + (c + 1) * MIX_CHUNK])
        c_gate = _dot(u, wxbc_ref[:, 2 * D_CONV + c * MIX_CHUNK:2 * D_CONV + (c + 1) * MIX_CHUNK])
        s_conv = c_gate * x_in
        yc_scr[:, cs] = _short_conv(s_conv, b_gate, convw_ref.at[:, cs],
                                    conv_scr.at[:, cs]).astype(BF16)
    new_prefix = conv_scr[CONV_PAD + tm - 2:CONV_PAD + tm, :]
    conv_scr[CONV_PAD - 2:CONV_PAD, :] = new_prefix

    q = _dot(u, wq_ref[...]) * (HEAD_DIM ** -0.5)
    kt = _dot_nt(wkt_ref[...], u)
    v = _dot(u, wv_ref[...])
    ones = jnp.ones((tm, HEAD_DIM), F32)
    for h in range(N_HEADS):
        hs = slice(h * HEAD_DIM, (h + 1) * HEAD_DIM)
        q_scr[h] = q[:, hs].astype(BF16)
        kt_scr[h, :, ATT_WINDOW:] = kt[hs, :].astype(BF16)
        v_scr[h, ATT_WINDOW:, :] = jnp.concatenate([v[:, hs], ones], axis=-1).astype(BF16)

    @pl.when(i == tiles_per_seq - 1)
    def _():
        kout_ref[0] = _dot(u, wk_ref[...])[tm - ATT_WINDOW:, :]
        vout_ref[0] = v[tm - ATT_WINDOW:, :]
        cout_ref[0] = new_prefix

    def attend(first_tile):
        col = lax.broadcasted_iota(jnp.int32, (1, PAIR_KEYS), 1)

        def scores(h, jj, slot):
            k0 = jj * PAIR
            s = _dot(q_scr[h, k0:k0 + PAIR, :], kt_scr[h, :, k0:k0 + PAIR_KEYS])
            s = s + bias_scr[h]
            if first_tile:
                s = s + jnp.where(col + k0 < ATT_WINDOW, NEG_INF, 0.0)
            s_scr[slot] = s
            return jnp.max(s, axis=-1, keepdims=True)

        def weighted_values(h, jj, slot, m):
            k0 = jj * PAIR
            p = jnp.exp(s_scr[slot] - m).astype(BF16)
            o = _dot(p, v_scr[h, k0:k0 + PAIR_KEYS, :])
            o_scr[h, k0:k0 + PAIR, :] = o[:, :HEAD_DIM] / o[:, HEAD_DIM:]

        def per_head_group(t, carry):
            steps = [(t * HEADS_PER_STEP + hh, jj)
                     for hh in range(HEADS_PER_STEP) for jj in range(n_pairs)]
            maxes = {}
            for k in range(len(steps) + ATT_LOOKAHEAD):
                if k < len(steps):
                    maxes[k] = scores(*steps[k], k % SCORE_SLOTS)
                if k >= ATT_LOOKAHEAD:
                    j = k - ATT_LOOKAHEAD
                    weighted_values(*steps[j], j % SCORE_SLOTS, maxes.pop(j))
            return carry
        lax.fori_loop(0, N_HEADS // HEADS_PER_STEP, per_head_group, 0)

    @pl.when(i == 0)
    def _():
        attend(True)

    @pl.when(i > 0)
    def _():
        attend(False)

    u = u_scr[...]
    y_conv = yc_scr[...]
    o_att = jnp.concatenate([o_scr[h] for h in range(N_HEADS)], axis=-1).astype(BF16)
    for c in range(D_MODEL // MIX_CHUNK):
        cs = slice(c * MIX_CHUNK, (c + 1) * MIX_CHUNK)
        g_conv = _dot(u, wg_ref[:, D_MODEL + c * MIX_CHUNK:D_MODEL + (c + 1) * MIX_CHUNK])
        m_conv = _sigmoid(g_conv) * _dot(y_conv, wco_ref[:, cs])
        g_att = _dot(u, wg_ref[:, cs])
        m_scr[:, cs] = (_sigmoid(g_att) * _dot(o_att, wao_ref[:, cs]) + m_conv).astype(BF16)
    kt_scr[:, :, :ATT_WINDOW] = kt_scr[:, :, tm:]
    v_scr[:, :ATT_WINDOW, :] = v_scr[:, tm:, :]
    h2_ref[...] = h1_ref[...] + _rmsnorm(_dot(m_scr[...], wo_ref[...]), g_ref[3:4, :])


def _mixer_prompt_call(tab, h1, norm_g, wq, wkt, wk, wv, wxbc, wg, convw, wao, wco, wo, batch, seq):
    tm = TOKEN_TILE
    tiles = seq // tm
    row = lambda b, i: (b * tiles + i, 0)
    per_seq = lambda b, i: (b, 0, 0)
    weights = (wq, wkt, wk, wv, wxbc, wg, convw, wao, wco, wo)
    return pl.pallas_call(
        functools.partial(_mixer_prompt_kernel, tm=tm, tiles_per_seq=tiles),
        grid=(batch, tiles),
        in_specs=[pl.BlockSpec(memory_space=pltpu.SMEM),
                  pl.BlockSpec((tm, D_MODEL), row), _resident(norm_g.shape)]
                 + [_resident(w.shape) for w in weights],
        out_specs=[pl.BlockSpec((tm, D_MODEL), row),
                   pl.BlockSpec((1, ATT_WINDOW, D_ATT), per_seq),
                   pl.BlockSpec((1, ATT_WINDOW, D_ATT), per_seq),
                   pl.BlockSpec((1, CONV_WIDTH - 1, D_CONV), per_seq)],
        out_shape=[jax.ShapeDtypeStruct((batch * seq, D_MODEL), F32),
                   jax.ShapeDtypeStruct((batch, ATT_WINDOW, D_ATT), F32),
                   jax.ShapeDtypeStruct((batch, ATT_WINDOW, D_ATT), F32),
                   jax.ShapeDtypeStruct((batch, CONV_WIDTH - 1, D_CONV), F32)],
        scratch_shapes=[pltpu.VMEM((N_HEADS, tm, HEAD_DIM), BF16),
                        pltpu.VMEM((N_HEADS, HEAD_DIM, ATT_WINDOW + tm), BF16),
                        pltpu.VMEM((N_HEADS, ATT_WINDOW + tm, 2 * HEAD_DIM), BF16),
                        pltpu.VMEM((N_HEADS, PAIR, PAIR_KEYS), F32),
                        pltpu.VMEM((SCORE_SLOTS, PAIR, PAIR_KEYS), F32),
                        pltpu.VMEM((N_HEADS, tm, HEAD_DIM), F32),
                        pltpu.VMEM((CONV_PAD + tm, D_CONV), F32),
                        pltpu.VMEM((tm, D_CONV), BF16),
                        pltpu.VMEM((tm, D_MODEL), BF16),
                        pltpu.VMEM((tm, D_MODEL), BF16)],
        compiler_params=_params(2),
        name="mixer_prompt",
    )(tab, h1, norm_g, *weights)


def _inproj_kernel(h1_ref, g_ref, wq_ref, wk_ref, wv_ref, wxbc_ref, wg_ref,
                   q_ref, k_ref, v_ref, xbc_ref, gates_ref):
    u = _rmsnorm(h1_ref[...], g_ref[2:3, :]).astype(BF16)
    q_ref[...] = _dot(u, wq_ref[...]) * (HEAD_DIM ** -0.5)
    k_ref[...] = _dot(u, wk_ref[...])
    v_ref[...] = _dot(u, wv_ref[...])
    xbc_ref[...] = _dot(u, wxbc_ref[...])
    gates_ref[...] = _dot(u, wg_ref[...])


def _inproj_call(h1, norm_g, wq, wk, wv, wxbc, wg):
    n = h1.shape[0]
    ins = (h1, norm_g, wq, wk, wv, wxbc, wg)
    widths = (D_ATT, D_ATT, D_ATT, 3 * D_CONV, 2 * D_MODEL)
    return pl.pallas_call(
        _inproj_kernel,
        grid=(1,),
        in_specs=[_resident(a.shape) for a in ins],
        out_specs=[pl.BlockSpec((n, w), lambda i: (0, 0)) for w in widths],
        out_shape=[jax.ShapeDtypeStruct((n, w), F32) for w in widths],
        compiler_params=_params(1),
        name="inproj_sample",
    )(*ins)


def _attn_sample_kernel(tab_ref, q_ref, k_ref, v_ref, ck_ref, cv_ref, xbc_ref, cconv_ref, convw_ref,
                        o_ref, y_ref, cout_ref, bias_c_scr, bias_n_scr, conv_scr, *, t, lc):
    @pl.when(pl.program_id(0) == 0)
    def _():
        r = lax.broadcasted_iota(jnp.int32, (t, lc), 0)
        c = lax.broadcasted_iota(jnp.int32, (t, lc), 1)
        d_c = jnp.minimum(lc + r - c, MAX_REL)
        r = lax.broadcasted_iota(jnp.int32, (t, t), 0)
        c = lax.broadcasted_iota(jnp.int32, (t, t), 1)
        d_n = jnp.clip(r - c, -MAX_REL, MAX_REL)

        def per_head(h, carry):
            bias_c_scr[h] = _build_bias(tab_ref, h, d_c, 1, MAX_REL)
            bias_n_scr[h] = _build_bias(tab_ref, h, d_n, -min(t - 1, MAX_REL), min(t - 1, MAX_REL))
            return carry
        lax.fori_loop(0, N_HEADS, per_head, 0)

    rows = N_HEADS * t
    row_head = lax.broadcasted_iota(jnp.int32, (N_HEADS, t, D_ATT), 0).reshape(rows, D_ATT)
    col = lax.broadcasted_iota(jnp.int32, (rows, D_ATT), 1)
    own_head = (col >= row_head * HEAD_DIM) & (col < (row_head + 1) * HEAD_DIM)
    q_heads = jnp.where(own_head, jnp.concatenate([q_ref[...]] * N_HEADS, axis=0), 0.0).astype(BF16)
    k_new = k_ref[...].astype(BF16)
    v_new = v_ref[...].astype(BF16)
    s_c = _dot(q_heads, ck_ref[0].astype(BF16)) + bias_c_scr[...].reshape(rows, lc)
    s_n = _dot_nt(q_heads, k_new) + bias_n_scr[...].reshape(rows, t)
    m = jnp.maximum(jnp.max(s_c, axis=-1, keepdims=True), jnp.max(s_n, axis=-1, keepdims=True))
    p_c = jnp.exp(s_c - m)
    p_n = jnp.exp(s_n - m)
    l = jnp.sum(p_c, axis=-1, keepdims=True) + jnp.sum(p_n, axis=-1, keepdims=True)
    o_all = (_dot_nt(p_c.astype(BF16), cv_ref[0].astype(BF16))
             + _dot(p_n.astype(BF16), v_new)) * (1.0 / l)
    o_all = jnp.where(own_head, o_all, 0.0)
    o = o_all[0:t]
    for h in range(1, N_HEADS):
        o = o + o_all[h * t:(h + 1) * t]
    o_ref[...] = o

    xbc = xbc_ref[...]
    s_conv = xbc[:, 2 * D_CONV:] * xbc[:, :D_CONV]
    conv_scr[0:CONV_PAD, :] = jnp.zeros((CONV_PAD, D_CONV), F32)
    conv_scr[CONV_PAD - 2:CONV_PAD, :] = cconv_ref[0]
    y_ref[...] = _short_conv(s_conv, xbc[:, D_CONV:2 * D_CONV], convw_ref, conv_scr)
    cout_ref[0] = conv_scr[CONV_PAD + t - 2:CONV_PAD + t, :]


def _attn_sample_call(tab, q, k, v, cache_k, cache_v, xbc, cache_conv, convw, batch, t):
    lc = cache_k.shape[1]
    row = lambda b: (b, 0)
    per_seq = lambda b: (b, 0, 0)
    cache_k = cache_k.transpose(0, 2, 3, 1).reshape(batch, D_ATT, lc)
    cache_v = cache_v.transpose(0, 2, 3, 1).reshape(batch, D_ATT, lc)
    cache_spec = pl.BlockSpec((1, D_ATT, lc), per_seq)
    return pl.pallas_call(
        functools.partial(_attn_sample_kernel, t=t, lc=lc),
        grid=(batch,),
        in_specs=[pl.BlockSpec(memory_space=pltpu.SMEM),
                  pl.BlockSpec((t, D_ATT), row), pl.BlockSpec((t, D_ATT), row),
                  pl.BlockSpec((t, D_ATT), row),
                  cache_spec, cache_spec,
                  pl.BlockSpec((t, 3 * D_CONV), row),
                  pl.BlockSpec((1, CONV_WIDTH - 1, D_CONV), per_seq),
                  _resident(convw.shape)],
        out_specs=[pl.BlockSpec((t, D_ATT), row), pl.BlockSpec((t, D_CONV), row),
                   pl.BlockSpec((1, CONV_WIDTH - 1, D_CONV), per_seq)],
        out_shape=[jax.ShapeDtypeStruct((batch * t, D_ATT), F32),
                   jax.ShapeDtypeStruct((batch * t, D_CONV), F32),
                   jax.ShapeDtypeStruct((batch, CONV_WIDTH - 1, D_CONV), F32)],
        scratch_shapes=[pltpu.VMEM((N_HEADS, t, lc), F32),
                        pltpu.VMEM((N_HEADS, t, t), F32),
                        pltpu.VMEM((CONV_PAD + t, D_CONV), F32)],
        compiler_params=_params(1),
        name="attn_sample",
    )(tab, q, k, v, cache_k, cache_v, xbc, cache_conv, convw)


def _mix_kernel(h1_ref, o_ref, y_ref, gates_ref, g_ref, wao_ref, wco_ref, wo_ref, h2_ref):
    h2_ref[...] = _mix_out(h1_ref[...], o_ref[...], y_ref[...], gates_ref[...], g_ref[3:4, :],
                           wao_ref, wco_ref, wo_ref)


def _mix_call(h1, o_att, y_conv, gates, norm_g, wao, wco, wo):
    ins = (h1, o_att, y_conv, gates, norm_g, wao, wco, wo)
    return pl.pallas_call(
        _mix_kernel,
        grid=(1,),
        in_specs=[_resident(a.shape) for a in ins],
        out_specs=pl.BlockSpec(h1.shape, lambda i: (0, 0)),
        out_shape=jax.ShapeDtypeStruct(h1.shape, F32),
        compiler_params=_params(1),
        name="mix_sample",
    )(*ins)


def kernel(x_prompt, x_sample, cache_k, cache_v, cache_conv, p_prompt, p_sample, norm_g,
           w1_gate, w1_up, w1_down, w_in, conv_w, rel_bias, w_att_out, w_conv_out, w_out,
           w2_gate, w2_up, w2_down, w_ple_gate, w_ple_proj):
    depth = norm_g.shape[0]
    assert depth == 1, "one layer per step"
    batch, seq, _ = x_prompt.shape
    dec_batch, dec_seq, _ = x_sample.shape
    assert seq % TOKEN_TILE == 0 and TOKEN_TILE % PAIR == 0 and TOKEN_TILE >= ATT_WINDOW
    l = 0

    g = norm_g[l]
    w1g = w1_gate[l].astype(BF16)
    w1u = w1_up[l].astype(BF16)
    w2g = w2_gate[l].astype(BF16)
    w2u = w2_up[l].astype(BF16)
    w1d = w1_down[l].astype(BF16)
    w2d = w2_down[l].astype(BF16)
    win = w_in[l].astype(BF16)
    wq = win[:, :D_ATT]
    wk = win[:, D_ATT:2 * D_ATT]
    wv = win[:, 2 * D_ATT:3 * D_ATT]
    wxbc = win[:, 3 * D_ATT:3 * D_ATT + 3 * D_CONV]
    wg = win[:, 3 * D_ATT + 3 * D_CONV:]
    wkt = wk.T
    wao = w_att_out[l].astype(BF16)
    wco = w_conv_out[l].astype(BF16)
    wo = w_out[l].astype(BF16)
    wpg = w_ple_gate[l].astype(BF16)
    wpp = w_ple_proj[l].astype(BF16)
    tab = rel_bias[l]
    convw = conv_w[l]

    xp = x_prompt.reshape(batch * seq, D_MODEL)
    h1p = _ffn_call(xp, g, w1g, w1u, w1d, 0, 1)
    h2p, k_p, v_p, c_p = _mixer_prompt_call(tab, h1p, g, wq, wkt, wk, wv, wxbc, wg, convw,
                                            wao, wco, wo, batch, seq)
    y_p = _ffn_ple_call(h2p, p_prompt[l].reshape(batch * seq, D_PLE), g, w2g, w2u, w2d, wpg, wpp,
                        4, 5, 6)

    lc = cache_k.shape[2]
    xs = x_sample.reshape(dec_batch * dec_seq, D_MODEL)
    h1s = _ffn_call(xs, g, w1g, w1u, w1d, 0, 1)
    q_s, k_s, v_s, xbc_s, gates_s = _inproj_call(h1s, g, wq, wk, wv, wxbc, wg)
    o_s, yc_s, c_s = _attn_sample_call(tab, q_s, k_s, v_s,
                                       cache_k[l], cache_v[l],
                                       xbc_s, cache_conv[l], convw, dec_batch, dec_seq)
    h2s = _mix_call(h1s, o_s, yc_s, gates_s, g, wao, wco, wo)
    y_s = _ffn_ple_call(h2s, p_sample[l].reshape(dec_batch * dec_seq, D_PLE), g, w2g, w2u, w2d,
                        wpg, wpp, 4, 5, 6)

    kv_p = (1, batch, ATT_WINDOW, N_HEADS, HEAD_DIM)
    kv_s = (1, dec_batch, dec_seq, N_HEADS, HEAD_DIM)
    return (y_p.reshape(batch, seq, D_MODEL), y_s.reshape(dec_batch, dec_seq, D_MODEL),
            k_p.reshape(kv_p), v_p.reshape(kv_p), c_p[None],
            k_s.reshape(kv_s), v_s.reshape(kv_s), c_s[None])
```

```python
import functools

import jax
import jax.numpy as jnp
from jax import lax
from jax.experimental import pallas as pl
from jax.experimental.pallas import tpu as pltpu

F32 = jnp.float32
BF16 = jnp.bfloat16

D_MODEL = 1024
CHUNK = 64
LEFT_CHUNKS = 8
ATT_WINDOW = LEFT_CHUNKS * CHUNK
BAND = ATT_WINDOW + CHUNK
N_HEADS = 8
HEAD_DIM = 64
D_ATT = N_HEADS * HEAD_DIM
D_CONV = D_MODEL // 2
CONV_WIDTH = 3
MAX_REL = 128
D_FF = 2816
D_PLE = 256
EPS = 1e-6
NEG_INF = -1e30

FF_CHUNK = 256
N_FF_CHUNKS = D_FF // FF_CHUNK
MIX_CHUNK = 256
PAIR = 2 * CHUNK
PAIR_KEYS = ATT_WINDOW + PAIR
HEADS_PER_STEP = 8
ATT_LOOKAHEAD = 3
SCORE_SLOTS = ATT_LOOKAHEAD + 2
TOKEN_TILE = 512
FFN_TOKEN_TILE = 1024
CONV_PAD = 8
VMEM_LIMIT_BYTES = 56 * 1024 * 1024


def _dot(a, b):
    return jnp.dot(a, b, preferred_element_type=F32)


def _dot_nt(a, b):
    return lax.dot_general(a, b, (((1,), (1,)), ((), ())), preferred_element_type=F32)


def _rmsnorm(x, g):
    return x * lax.rsqrt(jnp.mean(x * x, axis=-1, keepdims=True) + EPS) * g


def _sigmoid(x):
    return 1.0 / (1.0 + jnp.exp(-x))


def _resident(shape):
    return pl.BlockSpec(shape, lambda *_: (0,) * len(shape), pipeline_mode=pl.Buffered(1))


def _params(n_grid_dims):
    return pltpu.CompilerParams(
        dimension_semantics=("arbitrary",) * n_grid_dims,
        vmem_limit_bytes=VMEM_LIMIT_BYTES,
    )


def _swiglu_hidden(u, wg_ref, wu_ref, act_ref, chunks):
    for c in chunks:
        cs = slice(c * FF_CHUNK, (c + 1) * FF_CHUNK)
        gate = _dot(u, wg_ref[:, cs])
        up = _dot(u, wu_ref[:, cs])
        act_ref[:, cs] = (gate * _sigmoid(gate) * up).astype(BF16)


def _ple(h, p, g_ple, wpg_ref, wpp_ref):
    gate = _sigmoid(_dot(h.astype(BF16), wpg_ref[...]))
    proj = _dot(p.astype(BF16), wpp_ref[...])
    return h + _rmsnorm(gate * proj, g_ple)


def _ffn_body(x, g_pre, g_post, wg_ref, wu_ref, wd_ref, act_scr):
    u = _rmsnorm(x, g_pre).astype(BF16)
    _swiglu_hidden(u, wg_ref, wu_ref, act_scr, range(N_FF_CHUNKS))
    y = _dot(act_scr[...], wd_ref[...])
    return x + 0.5 * _rmsnorm(y, g_post)


def _ffn_kernel(x_ref, g_ref, wg_ref, wu_ref, wd_ref, o_ref, act_scr, *, pre, post):
    o_ref[...] = _ffn_body(x_ref[...], g_ref[pre:pre + 1, :], g_ref[post:post + 1, :],
                           wg_ref, wu_ref, wd_ref, act_scr)


def _ffn_ple_kernel(x_ref, p_ref, g_ref, wg_ref, wu_ref, wd_ref, wpg_ref, wpp_ref, o_ref, act_scr,
                    *, pre, post, ple):
    h = _ffn_body(x_ref[...], g_ref[pre:pre + 1, :], g_ref[post:post + 1, :],
                  wg_ref, wu_ref, wd_ref, act_scr)
    o_ref[...] = _ple(h, p_ref[...], g_ref[ple:ple + 1, :], wpg_ref, wpp_ref)


def _ffn_call(x, norm_g, wg, wu, wd, pre, post):
    n = x.shape[0]
    tm = min(FFN_TOKEN_TILE, n)
    row = lambda i: (i, 0)
    return pl.pallas_call(
        functools.partial(_ffn_kernel, pre=pre, post=post),
        grid=(n // tm,),
        in_specs=[pl.BlockSpec((tm, D_MODEL), row), _resident(norm_g.shape),
                  _resident(wg.shape), _resident(wu.shape), _resident(wd.shape)],
        out_specs=pl.BlockSpec((tm, D_MODEL), row),
        out_shape=jax.ShapeDtypeStruct((n, D_MODEL), F32),
        scratch_shapes=[pltpu.VMEM((tm, D_FF), BF16)],
        compiler_params=_params(1),
        name="ffn",
    )(x, norm_g, wg, wu, wd)


def _ffn_ple_call(x, p, norm_g, wg, wu, wd, wpg, wpp, pre, post, ple):
    n = x.shape[0]
    tm = min(FFN_TOKEN_TILE, n)
    row = lambda i: (i, 0)
    return pl.pallas_call(
        functools.partial(_ffn_ple_kernel, pre=pre, post=post, ple=ple),
        grid=(n // tm,),
        in_specs=[pl.BlockSpec((tm, D_MODEL), row), pl.BlockSpec((tm, D_PLE), row),
                  _resident(norm_g.shape), _resident(wg.shape), _resident(wu.shape),
                  _resident(wd.shape), _resident(wpg.shape), _resident(wpp.shape)],
        out_specs=pl.BlockSpec((tm, D_MODEL), row),
        out_shape=jax.ShapeDtypeStruct((n, D_MODEL), F32),
        scratch_shapes=[pltpu.VMEM((tm, D_FF), BF16)],
        compiler_params=_params(1),
        name="ffn_ple",
    )(x, p, norm_g, wg, wu, wd, wpg, wpp)


def _build_bias(tab_ref, head, dist, lo, hi):
    def body(j, acc):
        v = lo + j
        return jnp.where(dist == v, tab_ref[head, v + MAX_REL], acc)
    return lax.fori_loop(0, hi - lo + 1, body, jnp.full(dist.shape, NEG_INF, F32))


def _short_conv(s, bgate, convw_ref, conv_scr):
    t = s.shape[0]
    conv_scr[CONV_PAD:CONV_PAD + t, :] = s
    y = (convw_ref[0:1, :] * conv_scr[CONV_PAD - 2:CONV_PAD - 2 + t, :]
         + convw_ref[1:2, :] * conv_scr[CONV_PAD - 1:CONV_PAD - 1 + t, :]
         + convw_ref[2:3, :] * s)
    return bgate * y


def _mix_out(h1, o_att, y_conv, gates, g_post, wao_ref, wco_ref, wo_ref):
    m = (_sigmoid(gates[:, :D_MODEL]) * _dot(o_att.astype(BF16), wao_ref[...])
         + _sigmoid(gates[:, D_MODEL:]) * _dot(y_conv.astype(BF16), wco_ref[...]))
    return h1 + _rmsnorm(_dot(m.astype(BF16), wo_ref[...]), g_post)


def _mixer_prompt_kernel(tab_ref, h1_ref, g_ref, wq_ref, wkt_ref, wk_ref, wv_ref, wxbc_ref, wg_ref,
                         convw_ref, wao_ref, wco_ref, wo_ref,
                         h2_ref, kout_ref, vout_ref, cout_ref,
                         q_scr, kt_scr, v_scr, bias_scr, s_scr, o_scr, conv_scr, yc_scr, u_scr, m_scr,
                         *, tm, tiles_per_seq):
    b = pl.program_id(0)
    i = pl.program_id(1)
    n_pairs = tm // PAIR

    @pl.when((b == 0) & (i == 0))
    def _():
        rows = 32
        for rb in range(PAIR // rows):
            r = lax.broadcasted_iota(jnp.int32, (rows, PAIR_KEYS), 0) + rb * rows
            c = lax.broadcasted_iota(jnp.int32, (rows, PAIR_KEYS), 1)
            n = c - (r & CHUNK)
            d = jnp.minimum((r & (CHUNK - 1)) + ATT_WINDOW - n, MAX_REL)
            d = jnp.where((n >= 0) & (n < BAND), d, MAX_REL + 1)

            def per_head(h, carry):
                bias_scr[h, rb * rows:(rb + 1) * rows, :] = _build_bias(
                    tab_ref, h, d, -(CHUNK - 1), MAX_REL)
                return carry
            lax.fori_loop(0, N_HEADS, per_head, 0)

    @pl.when(i == 0)
    def _():
        kt_scr[:, :, :ATT_WINDOW] = jnp.zeros((N_HEADS, HEAD_DIM, ATT_WINDOW), BF16)
        v_scr[:, :ATT_WINDOW, :] = jnp.zeros((N_HEADS, ATT_WINDOW, 2 * HEAD_DIM), BF16)
        conv_scr[0:CONV_PAD, :] = jnp.zeros((CONV_PAD, D_CONV), F32)

    u = _rmsnorm(h1_ref[...], g_ref[2:3, :]).astype(BF16)
    u_scr[...] = u

    for c in range(D_CONV // MIX_CHUNK):
        cs = slice(c * MIX_CHUNK, (c + 1) * MIX_CHUNK)
        x_in = _dot(u, wxbc_ref[:, cs])
        b_gate = _dot(u, wxbc_ref[:, D_CONV + c * MIX_CHUNK:D_CONV + (c + 1) * MIX_CHUNK])
        c_gate = _dot(u, wxbc_ref[:, 2 * D_CONV + c * MIX_CHUNK:2 * D_CONV + (c + 1) * MIX_CHUNK])
        s_conv = c_gate * x_in
        yc_scr[:, cs] = _short_conv(s_conv, b_gate, convw_ref.at[:, cs],
                                    conv_scr.at[:, cs]).astype(BF16)
    new_prefix = conv_scr[CONV_PAD + tm - 2:CONV_PAD + tm, :]
    conv_scr[CONV_PAD - 2:CONV_PAD, :] = new_prefix

    q = _dot(u, wq_ref[...]) * (HEAD_DIM ** -0.5)
    kt = _dot_nt(wkt_ref[...], u)
    v = _dot(u, wv_ref[...])
    ones = jnp.ones((tm, HEAD_DIM), F32)
    for h in range(N_HEADS):
        hs = slice(h * HEAD_DIM, (h + 1) * HEAD_DIM)
        q_scr[h] = q[:, hs].astype(BF16)
        kt_scr[h, :, ATT_WINDOW:] = kt[hs, :].astype(BF16)
        v_scr[h, ATT_WINDOW:, :] = jnp.concatenate([v[:, hs], ones], axis=-1).astype(BF16)

    @pl.when(i == tiles_per_seq - 1)
    def _():
        kout_ref[0] = _dot(u, wk_ref[...])[tm - ATT_WINDOW:, :]
        vout_ref[0] = v[tm - ATT_WINDOW:, :]
        cout_ref[0] = new_prefix

    def attend(first_tile):
        col = lax.broadcasted_iota(jnp.int32, (1, PAIR_KEYS), 1)

        def scores(h, jj, slot):
            k0 = jj * PAIR
            s = _dot(q_scr[h, k0:k0 + PAIR, :], kt_scr[h, :, k0:k0 + PAIR_KEYS])
            s = s + bias_scr[h]
            if first_tile:
                s = s + jnp.where(col + k0 < ATT_WINDOW, NEG_INF, 0.0)
            s_scr[slot] = s
            return jnp.max(s, axis=-1, keepdims=True)

        def weighted_values(h, jj, slot, m):
            k0 = jj * PAIR
            p = jnp.exp(s_scr[slot] - m).astype(BF16)
            o_scr[h, k0:k0 + PAIR, :] = _dot(p, v_scr[h, k0:k0 + PAIR_KEYS, :])

        def per_head_group(t, carry):
            steps = [(t * HEADS_PER_STEP + hh, jj)
                     for hh in range(HEADS_PER_STEP) for jj in range(n_pairs)]
            maxes = {}
            for k in range(len(steps) + ATT_LOOKAHEAD):
                if k < len(steps):
                    maxes[k] = scores(*steps[k], k % SCORE_SLOTS)
                if k >= ATT_LOOKAHEAD:
                    j = k - ATT_LOOKAHEAD
                    weighted_values(*steps[j], j % SCORE_SLOTS, maxes.pop(j))
            return carry
        lax.fori_loop(0, N_HEADS // HEADS_PER_STEP, per_head_group, 0)

    @pl.when(i == 0)
    def _():
        attend(True)

    @pl.when(i > 0)
    def _():
        attend(False)

    u = u_scr[...]
    y_conv = yc_scr[...]
    low_half = lax.broadcasted_iota(jnp.int32, (tm, 2 * HEAD_DIM), 1) < HEAD_DIM
    pairs = []
    for j in range(N_HEADS // 2):
        even = o_scr[2 * j]
        odd = o_scr[2 * j + 1]
        pairs.append(jnp.where(low_half,
                               even * pltpu.roll(1.0 / even, HEAD_DIM, 1),
                               pltpu.roll(odd, HEAD_DIM, 1) * (1.0 / odd)))
    o_att = jnp.concatenate(pairs, axis=-1).astype(BF16)
    for c in range(D_MODEL // MIX_CHUNK):
        cs = slice(c * MIX_CHUNK, (c + 1) * MIX_CHUNK)
        g_conv = _dot(u, wg_ref[:, D_MODEL + c * MIX_CHUNK:D_MODEL + (c + 1) * MIX_CHUNK])
        m_conv = _sigmoid(g_conv) * _dot(y_conv, wco_ref[:, cs])
        g_att = _dot(u, wg_ref[:, cs])
        m_scr[:, cs] = (_sigmoid(g_att) * _dot(o_att, wao_ref[:, cs]) + m_conv).astype(BF16)
    kt_scr[:, :, :ATT_WINDOW] = kt_scr[:, :, tm:]
    v_scr[:, :ATT_WINDOW, :] = v_scr[:, tm:, :]
    h2_ref[...] = h1_ref[...] + _rmsnorm(_dot(m_scr[...], wo_ref[...]), g_ref[3:4, :])


def _mixer_prompt_call(tab, h1, norm_g, wq, wkt, wk, wv, wxbc, wg, convw, wao, wco, wo, batch, seq):
    tm = TOKEN_TILE
    tiles = seq // tm
    row = lambda b, i: (b * tiles + i, 0)
    per_seq = lambda b, i: (b, 0, 0)
    weights = (wq, wkt, wk, wv, wxbc, wg, convw, wao, wco, wo)
    return pl.pallas_call(
        functools.partial(_mixer_prompt_kernel, tm=tm, tiles_per_seq=tiles),
        grid=(batch, tiles),
        in_specs=[pl.BlockSpec(memory_space=pltpu.SMEM),
                  pl.BlockSpec((tm, D_MODEL), row), _resident(norm_g.shape)]
                 + [_resident(w.shape) for w in weights],
        out_specs=[pl.BlockSpec((tm, D_MODEL), row),
                   pl.BlockSpec((1, ATT_WINDOW, D_ATT), per_seq),
                   pl.BlockSpec((1, ATT_WINDOW, D_ATT), per_seq),
                   pl.BlockSpec((1, CONV_WIDTH - 1, D_CONV), per_seq)],
        out_shape=[jax.ShapeDtypeStruct((batch * seq, D_MODEL), F32),
                   jax.ShapeDtypeStruct((batch, ATT_WINDOW, D_ATT), F32),
                   jax.ShapeDtypeStruct((batch, ATT_WINDOW, D_ATT), F32),
                   jax.ShapeDtypeStruct((batch, CONV_WIDTH - 1, D_CONV), F32)],
        scratch_shapes=[pltpu.VMEM((N_HEADS, tm, HEAD_DIM), BF16),
                        pltpu.VMEM((N_HEADS, HEAD_DIM, ATT_WINDOW + tm), BF16),
                        pltpu.VMEM((N_HEADS, ATT_WINDOW + tm, 2 * HEAD_DIM), BF16),
                        pltpu.VMEM((N_HEADS, PAIR, PAIR_KEYS), F32),
                        pltpu.VMEM((SCORE_SLOTS, PAIR, PAIR_KEYS), F32),
                        pltpu.VMEM((N_HEADS, tm, 2 * HEAD_DIM), F32),
                        pltpu.VMEM((CONV_PAD + tm, D_CONV), F32),
                        pltpu.VMEM((tm, D_CONV), BF16),
                        pltpu.VMEM((tm, D_MODEL), BF16),
                        pltpu.VMEM((tm, D_MODEL), BF16)],
        compiler_params=_params(2),
        name="mixer_prompt",
    )(tab, h1, norm_g, *weights)


def _inproj_kernel(h1_ref, g_ref, wq_ref, wk_ref, wv_ref, wxbc_ref, wg_ref,
                   q_ref, k_ref, v_ref, xbc_ref, gates_ref):
    u = _rmsnorm(h1_ref[...], g_ref[2:3, :]).astype(BF16)
    q_ref[...] = _dot(u, wq_ref[...]) * (HEAD_DIM ** -0.5)
    k_ref[...] = _dot(u, wk_ref[...])
    v_ref[...] = _dot(u, wv_ref[...])
    xbc_ref[...] = _dot(u, wxbc_ref[...])
    gates_ref[...] = _dot(u, wg_ref[...])


def _inproj_call(h1, norm_g, wq, wk, wv, wxbc, wg):
    n = h1.shape[0]
    ins = (h1, norm_g, wq, wk, wv, wxbc, wg)
    widths = (D_ATT, D_ATT, D_ATT, 3 * D_CONV, 2 * D_MODEL)
    return pl.pallas_call(
        _inproj_kernel,
        grid=(1,),
        in_specs=[_resident(a.shape) for a in ins],
        out_specs=[pl.BlockSpec((n, w), lambda i: (0, 0)) for w in widths],
        out_shape=[jax.ShapeDtypeStruct((n, w), F32) for w in widths],
        compiler_params=_params(1),
        name="inproj_sample",
    )(*ins)


def _attn_sample_kernel(tab_ref, q_ref, k_ref, v_ref, ck_ref, cv_ref, xbc_ref, cconv_ref, convw_ref,
                        o_ref, y_ref, cout_ref, bias_c_scr, bias_n_scr, conv_scr, *, t, lc):
    @pl.when(pl.program_id(0) == 0)
    def _():
        r = lax.broadcasted_iota(jnp.int32, (t, lc), 0)
        c = lax.broadcasted_iota(jnp.int32, (t, lc), 1)
        d_c = jnp.minimum(lc + r - c, MAX_REL)
        r = lax.broadcasted_iota(jnp.int32, (t, t), 0)
        c = lax.broadcasted_iota(jnp.int32, (t, t), 1)
        d_n = jnp.clip(r - c, -MAX_REL, MAX_REL)

        def per_head(h, carry):
            bias_c_scr[h] = _build_bias(tab_ref, h, d_c, 1, MAX_REL)
            bias_n_scr[h] = _build_bias(tab_ref, h, d_n, -min(t - 1, MAX_REL), min(t - 1, MAX_REL))
            return carry
        lax.fori_loop(0, N_HEADS, per_head, 0)

    rows = N_HEADS * t
    row_head = lax.broadcasted_iota(jnp.int32, (N_HEADS, t, D_ATT), 0).reshape(rows, D_ATT)
    col = lax.broadcasted_iota(jnp.int32, (rows, D_ATT), 1)
    own_head = (col >= row_head * HEAD_DIM) & (col < (row_head + 1) * HEAD_DIM)
    q_heads = jnp.where(own_head, jnp.concatenate([q_ref[...]] * N_HEADS, axis=0), 0.0).astype(BF16)
    k_new = k_ref[...].astype(BF16)
    v_new = v_ref[...].astype(BF16)
    s_c = _dot(q_heads, ck_ref[0].astype(BF16)) + bias_c_scr[...].reshape(rows, lc)
    s_n = _dot_nt(q_heads, k_new) + bias_n_scr[...].reshape(rows, t)
    m = jnp.maximum(jnp.max(s_c, axis=-1, keepdims=True), jnp.max(s_n, axis=-1, keepdims=True))
    p_c = jnp.exp(s_c - m)
    p_n = jnp.exp(s_n - m)
    l = jnp.sum(p_c, axis=-1, keepdims=True) + jnp.sum(p_n, axis=-1, keepdims=True)
    o_all = (_dot_nt(p_c.astype(BF16), cv_ref[0].astype(BF16))
             + _dot(p_n.astype(BF16), v_new)) * (1.0 / l)
    o_all = jnp.where(own_head, o_all, 0.0)
    o = o_all[0:t]
    for h in range(1, N_HEADS):
        o = o + o_all[h * t:(h + 1) * t]
    o_ref[...] = o

    xbc = xbc_ref[...]
    s_conv = xbc[:, 2 * D_CONV:] * xbc[:, :D_CONV]
    conv_scr[0:CONV_PAD, :] = jnp.zeros((CONV_PAD, D_CONV), F32)
    conv_scr[CONV_PAD - 2:CONV_PAD, :] = cconv_ref[0]
    y_ref[...] = _short_conv(s_conv, xbc[:, D_CONV:2 * D_CONV], convw_ref, conv_scr)
    cout_ref[0] = conv_scr[CONV_PAD + t - 2:CONV_PAD + t, :]


def _attn_sample_call(tab, q, k, v, cache_k, cache_v, xbc, cache_conv, convw, batch, t):
    lc = cache_k.shape[1]
    row = lambda b: (b, 0)
    per_seq = lambda b: (b, 0, 0)
    cache_k = cache_k.transpose(0, 2, 3, 1).reshape(batch, D_ATT, lc)
    cache_v = cache_v.transpose(0, 2, 3, 1).reshape(batch, D_ATT, lc)
    cache_spec = pl.BlockSpec((1, D_ATT, lc), per_seq)
    return pl.pallas_call(
        functools.partial(_attn_sample_kernel, t=t, lc=lc),
        grid=(batch,),
        in_specs=[pl.BlockSpec(memory_space=pltpu.SMEM),
                  pl.BlockSpec((t, D_ATT), row), pl.BlockSpec((t, D_ATT), row),
                  pl.BlockSpec((t, D_ATT), row),
                  cache_spec, cache_spec,
                  pl.BlockSpec((t, 3 * D_CONV), row),
                  pl.BlockSpec((1, CONV_WIDTH - 1, D_CONV), per_seq),
                  _resident(convw.shape)],
        out_specs=[pl.BlockSpec((t, D_ATT), row), pl.BlockSpec((t, D_CONV), row),
                   pl.BlockSpec((1, CONV_WIDTH - 1, D_CONV), per_seq)],
        out_shape=[jax.ShapeDtypeStruct((batch * t, D_ATT), F32),
                   jax.ShapeDtypeStruct((batch * t, D_CONV), F32),
                   jax.ShapeDtypeStruct((batch, CONV_WIDTH - 1, D_CONV), F32)],
        scratch_shapes=[pltpu.VMEM((N_HEADS, t, lc), F32),
                        pltpu.VMEM((N_HEADS, t, t), F32),
                        pltpu.VMEM((CONV_PAD + t, D_CONV), F32)],
        compiler_params=_params(1),
        name="attn_sample",
    )(tab, q, k, v, cache_k, cache_v, xbc, cache_conv, convw)


def _mix_kernel(h1_ref, o_ref, y_ref, gates_ref, g_ref, wao_ref, wco_ref, wo_ref, h2_ref):
    h2_ref[...] = _mix_out(h1_ref[...], o_ref[...], y_ref[...], gates_ref[...], g_ref[3:4, :],
                           wao_ref, wco_ref, wo_ref)


def _mix_call(h1, o_att, y_conv, gates, norm_g, wao, wco, wo):
    ins = (h1, o_att, y_conv, gates, norm_g, wao, wco, wo)
    return pl.pallas_call(
        _mix_kernel,
        grid=(1,),
        in_specs=[_resident(a.shape) for a in ins],
        out_specs=pl.BlockSpec(h1.shape, lambda i: (0, 0)),
        out_shape=jax.ShapeDtypeStruct(h1.shape, F32),
        compiler_params=_params(1),
        name="mix_sample",
    )(*ins)


def kernel(x_prompt, x_sample, cache_k, cache_v, cache_conv, p_prompt, p_sample, norm_g,
           w1_gate, w1_up, w1_down, w_in, conv_w, rel_bias, w_att_out, w_conv_out, w_out,
           w2_gate, w2_up, w2_down, w_ple_gate, w_ple_proj):
    depth = norm_g.shape[0]
    assert depth == 1, "one layer per step"
    batch, seq, _ = x_prompt.shape
    dec_batch, dec_seq, _ = x_sample.shape
    assert seq % TOKEN_TILE == 0 and TOKEN_TILE % PAIR == 0 and TOKEN_TILE >= ATT_WINDOW
    l = 0

    g = norm_g[l]
    w1g = w1_gate[l].astype(BF16)
    w1u = w1_up[l].astype(BF16)
    w2g = w2_gate[l].astype(BF16)
    w2u = w2_up[l].astype(BF16)
    w1d = w1_down[l].astype(BF16)
    w2d = w2_down[l].astype(BF16)
    win = w_in[l].astype(BF16)
    wq = win[:, :D_ATT]
    wk = win[:, D_ATT:2 * D_ATT]
    wv = win[:, 2 * D_ATT:3 * D_ATT]
    wxbc = win[:, 3 * D_ATT:3 * D_ATT + 3 * D_CONV]
    wg = win[:, 3 * D_ATT + 3 * D_CONV:]
    wkt = wk.T
    wao = w_att_out[l].astype(BF16)
    wco = w_conv_out[l].astype(BF16)
    wo = w_out[l].astype(BF16)
    wpg = w_ple_gate[l].astype(BF16)
    wpp = w_ple_proj[l].astype(BF16)
    tab = rel_bias[l]
    convw = conv_w[l]

    xp = x_prompt.reshape(batch * seq, D_MODEL)
    h1p = _ffn_call(xp, g, w1g, w1u, w1d, 0, 1)
    h2p, k_p, v_p, c_p = _mixer_prompt_call(tab, h1p, g, wq, wkt, wk, wv, wxbc, wg, convw,
                                            wao, wco, wo, batch, seq)
    y_p = _ffn_ple_call(h2p, p_prompt[l].reshape(batch * seq, D_PLE), g, w2g, w2u, w2d, wpg, wpp,
                        4, 5, 6)

    lc = cache_k.shape[2]
    xs = x_sample.reshape(dec_batch * dec_seq, D_MODEL)
    h1s = _ffn_call(xs, g, w1g, w1u, w1d, 0, 1)
    q_s, k_s, v_s, xbc_s, gates_s = _inproj_call(h1s, g, wq, wk, wv, wxbc, wg)
    o_s, yc_s, c_s = _attn_sample_call(tab, q_s, k_s, v_s,
                                       cache_k[l], cache_v[l],
                                       xbc_s, cache_conv[l], convw, dec_batch, dec_seq)
    h2s = _mix_call(h1s, o_s, yc_s, gates_s, g, wao, wco, wo)
    y_s = _ffn_ple_call(h2s, p_sample[l].reshape(dec_batch * dec_seq, D_PLE), g, w2g, w2u, w2d,
                        wpg, wpp, 4, 5, 6)

    kv_p = (1, batch, ATT_WINDOW, N_HEADS, HEAD_DIM)
    kv_s = (1, dec_batch, dec_seq, N_HEADS, HEAD_DIM)
    return (y_p.reshape(batch, seq, D_MODEL), y_s.reshape(dec_batch, dec_seq, D_MODEL),
            k_p.reshape(kv_p), v_p.reshape(kv_p), c_p[None],
            k_s.reshape(kv_s), v_s.reshape(kv_s), c_s[None])
```

```python
import functools

import jax
import jax.numpy as jnp
from jax import lax
from jax.experimental import pallas as pl
from jax.experimental.pallas import tpu as pltpu

F32 = jnp.float32
BF16 = jnp.bfloat16

D_MODEL = 1024
CHUNK = 64
LEFT_CHUNKS = 8
ATT_WINDOW = LEFT_CHUNKS * CHUNK
BAND = ATT_WINDOW + CHUNK
N_HEADS = 8
HEAD_DIM = 64
D_ATT = N_HEADS * HEAD_DIM
D_CONV = D_MODEL // 2
CONV_WIDTH = 3
MAX_REL = 128
D_FF = 2816
D_PLE = 256
EPS = 1e-6
NEG_INF = -1e30

FF_CHUNK = 256
N_FF_CHUNKS = D_FF // FF_CHUNK
MIX_CHUNK = 256
PAIR = 2 * CHUNK
PAIR_KEYS = ATT_WINDOW + PAIR
HEADS_PER_STEP = 8
ATT_LOOKAHEAD = 3
SCORE_SLOTS = ATT_LOOKAHEAD + 2
TOKEN_TILE = 512
FFN_TOKEN_TILE = 1024
FFN_ROW_GROUPS = 2
CONV_PAD = 8
VMEM_LIMIT_BYTES = 56 * 1024 * 1024


def _dot(a, b):
    return jnp.dot(a, b, preferred_element_type=F32)


def _dot_nt(a, b):
    return lax.dot_general(a, b, (((1,), (1,)), ((), ())), preferred_element_type=F32)


def _rmsnorm(x, g):
    return x * lax.rsqrt(jnp.mean(x * x, axis=-1, keepdims=True) + EPS) * g


def _sigmoid(x):
    return 1.0 / (1.0 + jnp.exp(-x))


def _resident(shape):
    return pl.BlockSpec(shape, lambda *_: (0,) * len(shape), pipeline_mode=pl.Buffered(1))


def _params(n_grid_dims):
    return pltpu.CompilerParams(
        dimension_semantics=("arbitrary",) * n_grid_dims,
        vmem_limit_bytes=VMEM_LIMIT_BYTES,
    )


def _swiglu_hidden(u, wg_ref, wu_ref, act_ref, chunks):
    for c in chunks:
        cs = slice(c * FF_CHUNK, (c + 1) * FF_CHUNK)
        gate = _dot(u, wg_ref[:, cs])
        up = _dot(u, wu_ref[:, cs])
        act_ref[:, cs] = (gate * _sigmoid(gate) * up).astype(BF16)


def _row_groups(n_rows):
    n = FFN_ROW_GROUPS if n_rows % FFN_TOKEN_TILE == 0 else 1
    return [slice(k * n_rows // n, (k + 1) * n_rows // n) for k in range(n)]


def _ffn_hidden_and_down(x_ref, g_pre, g_post, wg_ref, wu_ref, wd_ref, act_scr):
    groups = _row_groups(x_ref.shape[0])
    for r in groups:
        u = _rmsnorm(x_ref[r, :], g_pre).astype(BF16)
        _swiglu_hidden(u, wg_ref, wu_ref, act_scr.at[r, :], range(N_FF_CHUNKS))
    ys = [_dot(act_scr[r, :], wd_ref[...]) for r in groups]
    return groups, [x_ref[r, :] + 0.5 * _rmsnorm(y, g_post) for r, y in zip(groups, ys)]


def _ffn_kernel(x_ref, g_ref, wg_ref, wu_ref, wd_ref, o_ref, act_scr, *, pre, post):
    groups, hs = _ffn_hidden_and_down(x_ref, g_ref[pre:pre + 1, :], g_ref[post:post + 1, :],
                                      wg_ref, wu_ref, wd_ref, act_scr)
    for r, h in zip(groups, hs):
        o_ref[r, :] = h


def _ffn_ple_kernel(x_ref, p_ref, g_ref, wg_ref, wu_ref, wd_ref, wpg_ref, wpp_ref, o_ref, act_scr,
                    *, pre, post, ple):
    groups, hs = _ffn_hidden_and_down(x_ref, g_ref[pre:pre + 1, :], g_ref[post:post + 1, :],
                                      wg_ref, wu_ref, wd_ref, act_scr)
    gated = [_sigmoid(_dot(h.astype(BF16), wpg_ref[...]))
             * _dot(p_ref[r, :].astype(BF16), wpp_ref[...]) for r, h in zip(groups, hs)]
    for r, h, gp in zip(groups, hs, gated):
        o_ref[r, :] = h + _rmsnorm(gp, g_ref[ple:ple + 1, :])


def _ffn_call(x, norm_g, wg, wu, wd, pre, post):
    n = x.shape[0]
    tm = min(FFN_TOKEN_TILE, n)
    row = lambda i: (i, 0)
    return pl.pallas_call(
        functools.partial(_ffn_kernel, pre=pre, post=post),
        grid=(n // tm,),
        in_specs=[pl.BlockSpec((tm, D_MODEL), row), _resident(norm_g.shape),
                  _resident(wg.shape), _resident(wu.shape), _resident(wd.shape)],
        out_specs=pl.BlockSpec((tm, D_MODEL), row),
        out_shape=jax.ShapeDtypeStruct((n, D_MODEL), F32),
        scratch_shapes=[pltpu.VMEM((tm, D_FF), BF16)],
        compiler_params=_params(1),
        name="ffn",
    )(x, norm_g, wg, wu, wd)


def _ffn_ple_call(x, p, norm_g, wg, wu, wd, wpg, wpp, pre, post, ple):
    n = x.shape[0]
    tm = min(FFN_TOKEN_TILE, n)
    row = lambda i: (i, 0)
    return pl.pallas_call(
        functools.partial(_ffn_ple_kernel, pre=pre, post=post, ple=ple),
        grid=(n // tm,),
        in_specs=[pl.BlockSpec((tm, D_MODEL), row), pl.BlockSpec((tm, D_PLE), row),
                  _resident(norm_g.shape), _resident(wg.shape), _resident(wu.shape),
                  _resident(wd.shape), _resident(wpg.shape), _resident(wpp.shape)],
        out_specs=pl.BlockSpec((tm, D_MODEL), row),
        out_shape=jax.ShapeDtypeStruct((n, D_MODEL), F32),
        scratch_shapes=[pltpu.VMEM((tm, D_FF), BF16)],
        compiler_params=_params(1),
        name="ffn_ple",
    )(x, p, norm_g, wg, wu, wd, wpg, wpp)


def _build_bias(tab_ref, head, dist, lo, hi):
    def body(j, acc):
        v = lo + j
        return jnp.where(dist == v, tab_ref[head, v + MAX_REL], acc)
    return lax.fori_loop(0, hi - lo + 1, body, jnp.full(dist.shape, NEG_INF, F32))


def _short_conv(s, bgate, convw_ref, conv_scr):
    t = s.shape[0]
    conv_scr[CONV_PAD:CONV_PAD + t, :] = s
    y = (convw_ref[0:1, :] * conv_scr[CONV_PAD - 2:CONV_PAD - 2 + t, :]
         + convw_ref[1:2, :] * conv_scr[CONV_PAD - 1:CONV_PAD - 1 + t, :]
         + convw_ref[2:3, :] * s)
    return bgate * y


def _mix_out(h1, o_att, y_conv, gates, g_post, wao_ref, wco_ref, wo_ref):
    m = (_sigmoid(gates[:, :D_MODEL]) * _dot(o_att.astype(BF16), wao_ref[...])
         + _sigmoid(gates[:, D_MODEL:]) * _dot(y_conv.astype(BF16), wco_ref[...]))
    return h1 + _rmsnorm(_dot(m.astype(BF16), wo_ref[...]), g_post)


def _mixer_prompt_kernel(tab_ref, h1_ref, g_ref, wq_ref, wkt_ref, wk_ref, wv_ref, wxbc_ref, wg_ref,
                         convw_ref, wao_ref, wco_ref, wo_ref,
                         h2_ref, kout_ref, vout_ref, cout_ref,
                         q_scr, kt_scr, v_scr, bias_scr, s_scr, o_scr, conv_scr, yc_scr, u_scr, m_scr,
                         *, tm, tiles_per_seq):
    b = pl.program_id(0)
    i = pl.program_id(1)
    n_pairs = tm // PAIR

    @pl.when((b == 0) & (i == 0))
    def _():
        rows = 32
        for rb in range(PAIR // rows):
            r = lax.broadcasted_iota(jnp.int32, (rows, PAIR_KEYS), 0) + rb * rows
            c = lax.broadcasted_iota(jnp.int32, (rows, PAIR_KEYS), 1)
            n = c - (r & CHUNK)
            d = jnp.minimum((r & (CHUNK - 1)) + ATT_WINDOW - n, MAX_REL)
            d = jnp.where((n >= 0) & (n < BAND), d, MAX_REL + 1)

            def per_head(h, carry):
                bias_scr[h, rb * rows:(rb + 1) * rows, :] = _build_bias(
                    tab_ref, h, d, -(CHUNK - 1), MAX_REL)
                return carry
            lax.fori_loop(0, N_HEADS, per_head, 0)

    @pl.when(i == 0)
    def _():
        kt_scr[:, :, :ATT_WINDOW] = jnp.zeros((N_HEADS, HEAD_DIM, ATT_WINDOW), BF16)
        v_scr[:, :ATT_WINDOW, :] = jnp.zeros((N_HEADS, ATT_WINDOW, 2 * HEAD_DIM), BF16)
        conv_scr[0:CONV_PAD, :] = jnp.zeros((CONV_PAD, D_CONV), F32)

    u = _rmsnorm(h1_ref[...], g_ref[2:3, :]).astype(BF16)
    u_scr[...] = u

    for c in range(D_CONV // MIX_CHUNK):
        cs = slice(c * MIX_CHUNK, (c + 1) * MIX_CHUNK)
        x_in = _dot(u, wxbc_ref[:, cs])
        b_gate = _dot(u, wxbc_ref[:, D_CONV + c * MIX_CHUNK:D_CONV + (c + 1) * MIX_CHUNK])
        c_gate = _dot(u, wxbc_ref[:, 2 * D_CONV + c * MIX_CHUNK:2 * D_CONV + (c + 1) * MIX_CHUNK])
        s_conv = c_gate * x_in
        yc_scr[:, cs] = _short_conv(s_conv, b_gate, convw_ref.at[:, cs],
                                    conv_scr.at[:, cs]).astype(BF16)
    new_prefix = conv_scr[CONV_PAD + tm - 2:CONV_PAD + tm, :]
    conv_scr[CONV_PAD - 2:CONV_PAD, :] = new_prefix

    q = _dot(u, wq_ref[...]) * (HEAD_DIM ** -0.5)
    kt = _dot_nt(wkt_ref[...], u)
    v = _dot(u, wv_ref[...])
    ones = jnp.ones((tm, HEAD_DIM), F32)
    for h in range(N_HEADS):
        hs = slice(h * HEAD_DIM, (h + 1) * HEAD_DIM)
        q_scr[h] = q[:, hs].astype(BF16)
        kt_scr[h, :, ATT_WINDOW:] = kt[hs, :].astype(BF16)
        v_scr[h, ATT_WINDOW:, :] = jnp.concatenate([v[:, hs], ones], axis=-1).astype(BF16)

    @pl.when(i == tiles_per_seq - 1)
    def _():
        kout_ref[0] = _dot(u, wk_ref[...])[tm - ATT_WINDOW:, :]
        vout_ref[0] = v[tm - ATT_WINDOW:, :]
        cout_ref[0] = new_prefix

    def attend(first_tile):
        col = lax.broadcasted_iota(jnp.int32, (1, PAIR_KEYS), 1)

        def scores(h, jj, slot):
            k0 = jj * PAIR
            s = _dot(q_scr[h, k0:k0 + PAIR, :], kt_scr[h, :, k0:k0 + PAIR_KEYS])
            s = s + bias_scr[h]
            if first_tile:
                s = s + jnp.where(col + k0 < ATT_WINDOW, NEG_INF, 0.0)
            s_scr[slot] = s
            return jnp.max(s, axis=-1, keepdims=True)

        def weighted_values(h, jj, slot, m):
            k0 = jj * PAIR
            p = jnp.exp(s_scr[slot] - m).astype(BF16)
            o_scr[h, k0:k0 + PAIR, :] = _dot(p, v_scr[h, k0:k0 + PAIR_KEYS, :])

        def per_head_group(t, carry):
            steps = [(t * HEADS_PER_STEP + hh, jj)
                     for hh in range(HEADS_PER_STEP) for jj in range(n_pairs)]
            maxes = {}
            for k in range(len(steps) + ATT_LOOKAHEAD):
                if k < len(steps):
                    maxes[k] = scores(*steps[k], k % SCORE_SLOTS)
                if k >= ATT_LOOKAHEAD:
                    j = k - ATT_LOOKAHEAD
                    weighted_values(*steps[j], j % SCORE_SLOTS, maxes.pop(j))
            return carry
        lax.fori_loop(0, N_HEADS // HEADS_PER_STEP, per_head_group, 0)

    @pl.when(i == 0)
    def _():
        attend(True)

    @pl.when(i > 0)
    def _():
        attend(False)

    u = u_scr[...]
    y_conv = yc_scr[...]
    low_half = lax.broadcasted_iota(jnp.int32, (tm, 2 * HEAD_DIM), 1) < HEAD_DIM
    pairs = []
    for j in range(N_HEADS // 2):
        even = o_scr[2 * j]
        odd = o_scr[2 * j + 1]
        pairs.append(jnp.where(low_half,
                               even * pltpu.roll(1.0 / even, HEAD_DIM, 1),
                               pltpu.roll(odd, HEAD_DIM, 1) * (1.0 / odd)))
    o_att = jnp.concatenate(pairs, axis=-1).astype(BF16)
    for c in range(D_MODEL // MIX_CHUNK):
        cs = slice(c * MIX_CHUNK, (c + 1) * MIX_CHUNK)
        g_conv = _dot(u, wg_ref[:, D_MODEL + c * MIX_CHUNK:D_MODEL + (c + 1) * MIX_CHUNK])
        m_conv = _sigmoid(g_conv) * _dot(y_conv, wco_ref[:, cs])
        g_att = _dot(u, wg_ref[:, cs])
        m_scr[:, cs] = (_sigmoid(g_att) * _dot(o_att, wao_ref[:, cs]) + m_conv).astype(BF16)
    kt_scr[:, :, :ATT_WINDOW] = kt_scr[:, :, tm:]
    v_scr[:, :ATT_WINDOW, :] = v_scr[:, tm:, :]
    h2_ref[...] = h1_ref[...] + _rmsnorm(_dot(m_scr[...], wo_ref[...]), g_ref[3:4, :])


def _mixer_prompt_call(tab, h1, norm_g, wq, wkt, wk, wv, wxbc, wg, convw, wao, wco, wo, batch, seq):
    tm = TOKEN_TILE
    tiles = seq // tm
    row = lambda b, i: (b * tiles + i, 0)
    per_seq = lambda b, i: (b, 0, 0)
    weights = (wq, wkt, wk, wv, wxbc, wg, convw, wao, wco, wo)
    return pl.pallas_call(
        functools.partial(_mixer_prompt_kernel, tm=tm, tiles_per_seq=tiles),
        grid=(batch, tiles),
        in_specs=[pl.BlockSpec(memory_space=pltpu.SMEM),
                  pl.BlockSpec((tm, D_MODEL), row), _resident(norm_g.shape)]
                 + [_resident(w.shape) for w in weights],
        out_specs=[pl.BlockSpec((tm, D_MODEL), row),
                   pl.BlockSpec((1, ATT_WINDOW, D_ATT), per_seq),
                   pl.BlockSpec((1, ATT_WINDOW, D_ATT), per_seq),
                   pl.BlockSpec((1, CONV_WIDTH - 1, D_CONV), per_seq)],
        out_shape=[jax.ShapeDtypeStruct((batch * seq, D_MODEL), F32),
                   jax.ShapeDtypeStruct((batch, ATT_WINDOW, D_ATT), F32),
                   jax.ShapeDtypeStruct((batch, ATT_WINDOW, D_ATT), F32),
                   jax.ShapeDtypeStruct((batch, CONV_WIDTH - 1, D_CONV), F32)],
        scratch_shapes=[pltpu.VMEM((N_HEADS, tm, HEAD_DIM), BF16),
                        pltpu.VMEM((N_HEADS, HEAD_DIM, ATT_WINDOW + tm), BF16),
                        pltpu.VMEM((N_HEADS, ATT_WINDOW + tm, 2 * HEAD_DIM), BF16),
                        pltpu.VMEM((N_HEADS, PAIR, PAIR_KEYS), F32),
                        pltpu.VMEM((SCORE_SLOTS, PAIR, PAIR_KEYS), F32),
                        pltpu.VMEM((N_HEADS, tm, 2 * HEAD_DIM), F32),
                        pltpu.VMEM((CONV_PAD + tm, D_CONV), F32),
                        pltpu.VMEM((tm, D_CONV), BF16),
                        pltpu.VMEM((tm, D_MODEL), BF16),
                        pltpu.VMEM((tm, D_MODEL), BF16)],
        compiler_params=_params(2),
        name="mixer_prompt",
    )(tab, h1, norm_g, *weights)


def _inproj_kernel(h1_ref, g_ref, wq_ref, wk_ref, wv_ref, wxbc_ref, wg_ref,
                   q_ref, k_ref, v_ref, xbc_ref, gates_ref):
    u = _rmsnorm(h1_ref[...], g_ref[2:3, :]).astype(BF16)
    q_ref[...] = _dot(u, wq_ref[...]) * (HEAD_DIM ** -0.5)
    k_ref[...] = _dot(u, wk_ref[...])
    v_ref[...] = _dot(u, wv_ref[...])
    xbc_ref[...] = _dot(u, wxbc_ref[...])
    gates_ref[...] = _dot(u, wg_ref[...])


def _inproj_call(h1, norm_g, wq, wk, wv, wxbc, wg):
    n = h1.shape[0]
    ins = (h1, norm_g, wq, wk, wv, wxbc, wg)
    widths = (D_ATT, D_ATT, D_ATT, 3 * D_CONV, 2 * D_MODEL)
    return pl.pallas_call(
        _inproj_kernel,
        grid=(1,),
        in_specs=[_resident(a.shape) for a in ins],
        out_specs=[pl.BlockSpec((n, w), lambda i: (0, 0)) for w in widths],
        out_shape=[jax.ShapeDtypeStruct((n, w), F32) for w in widths],
        compiler_params=_params(1),
        name="inproj_sample",
    )(*ins)


def _attn_sample_kernel(tab_ref, q_ref, k_ref, v_ref, ck_ref, cv_ref, xbc_ref, cconv_ref, convw_ref,
                        o_ref, y_ref, cout_ref, bias_c_scr, bias_n_scr, conv_scr, *, t, lc):
    @pl.when(pl.program_id(0) == 0)
    def _():
        r = lax.broadcasted_iota(jnp.int32, (t, lc), 0)
        c = lax.broadcasted_iota(jnp.int32, (t, lc), 1)
        d_c = jnp.minimum(lc + r - c, MAX_REL)
        r = lax.broadcasted_iota(jnp.int32, (t, t), 0)
        c = lax.broadcasted_iota(jnp.int32, (t, t), 1)
        d_n = jnp.clip(r - c, -MAX_REL, MAX_REL)

        def per_head(h, carry):
            bias_c_scr[h] = _build_bias(tab_ref, h, d_c, 1, MAX_REL)
            bias_n_scr[h] = _build_bias(tab_ref, h, d_n, -min(t - 1, MAX_REL), min(t - 1, MAX_REL))
            return carry
        lax.fori_loop(0, N_HEADS, per_head, 0)

    rows = N_HEADS * t
    row_head = lax.broadcasted_iota(jnp.int32, (N_HEADS, t, D_ATT), 0).reshape(rows, D_ATT)
    col = lax.broadcasted_iota(jnp.int32, (rows, D_ATT), 1)
    own_head = (col >= row_head * HEAD_DIM) & (col < (row_head + 1) * HEAD_DIM)
    q_heads = jnp.where(own_head, jnp.concatenate([q_ref[...]] * N_HEADS, axis=0), 0.0).astype(BF16)
    k_new = k_ref[...].astype(BF16)
    v_new = v_ref[...].astype(BF16)
    s_c = _dot(q_heads, ck_ref[0].astype(BF16)) + bias_c_scr[...].reshape(rows, lc)
    s_n = _dot_nt(q_heads, k_new) + bias_n_scr[...].reshape(rows, t)
    m = jnp.maximum(jnp.max(s_c, axis=-1, keepdims=True), jnp.max(s_n, axis=-1, keepdims=True))
    p_c = jnp.exp(s_c - m)
    p_n = jnp.exp(s_n - m)
    l = jnp.sum(p_c, axis=-1, keepdims=True) + jnp.sum(p_n, axis=-1, keepdims=True)
    o_all = (_dot_nt(p_c.astype(BF16), cv_ref[0].astype(BF16))
             + _dot(p_n.astype(BF16), v_new)) * (1.0 / l)
    o_all = jnp.where(own_head, o_all, 0.0)
    o = o_all[0:t]
    for h in range(1, N_HEADS):
        o = o + o_all[h * t:(h + 1) * t]
    o_ref[...] = o

    xbc = xbc_ref[...]
    s_conv = xbc[:, 2 * D_CONV:] * xbc[:, :D_CONV]
    conv_scr[0:CONV_PAD, :] = jnp.zeros((CONV_PAD, D_CONV), F32)
    conv_scr[CONV_PAD - 2:CONV_PAD, :] = cconv_ref[0]
    y_ref[...] = _short_conv(s_conv, xbc[:, D_CONV:2 * D_CONV], convw_ref, conv_scr)
    cout_ref[0] = conv_scr[CONV_PAD + t - 2:CONV_PAD + t, :]


def _attn_sample_call(tab, q, k, v, cache_k, cache_v, xbc, cache_conv, convw, batch, t):
    lc = cache_k.shape[1]
    row = lambda b: (b, 0)
    per_seq = lambda b: (b, 0, 0)
    cache_k = cache_k.transpose(0, 2, 3, 1).reshape(batch, D_ATT, lc)
    cache_v = cache_v.transpose(0, 2, 3, 1).reshape(batch, D_ATT, lc)
    cache_spec = pl.BlockSpec((1, D_ATT, lc), per_seq)
    return pl.pallas_call(
        functools.partial(_attn_sample_kernel, t=t, lc=lc),
        grid=(batch,),
        in_specs=[pl.BlockSpec(memory_space=pltpu.SMEM),
                  pl.BlockSpec((t, D_ATT), row), pl.BlockSpec((t, D_ATT), row),
                  pl.BlockSpec((t, D_ATT), row),
                  cache_spec, cache_spec,
                  pl.BlockSpec((t, 3 * D_CONV), row),
                  pl.BlockSpec((1, CONV_WIDTH - 1, D_CONV), per_seq),
                  _resident(convw.shape)],
        out_specs=[pl.BlockSpec((t, D_ATT), row), pl.BlockSpec((t, D_CONV), row),
                   pl.BlockSpec((1, CONV_WIDTH - 1, D_CONV), per_seq)],
        out_shape=[jax.ShapeDtypeStruct((batch * t, D_ATT), F32),
                   jax.ShapeDtypeStruct((batch * t, D_CONV), F32),
                   jax.ShapeDtypeStruct((batch, CONV_WIDTH - 1, D_CONV), F32)],
        scratch_shapes=[pltpu.VMEM((N_HEADS, t, lc), F32),
                        pltpu.VMEM((N_HEADS, t, t), F32),
                        pltpu.VMEM((CONV_PAD + t, D_CONV), F32)],
        compiler_params=_params(1),
        name="attn_sample",
    )(tab, q, k, v, cache_k, cache_v, xbc, cache_conv, convw)


def _mix_kernel(h1_ref, o_ref, y_ref, gates_ref, g_ref, wao_ref, wco_ref, wo_ref, h2_ref):
    h2_ref[...] = _mix_out(h1_ref[...], o_ref[...], y_ref[...], gates_ref[...], g_ref[3:4, :],
                           wao_ref, wco_ref, wo_ref)


def _mix_call(h1, o_att, y_conv, gates, norm_g, wao, wco, wo):
    ins = (h1, o_att, y_conv, gates, norm_g, wao, wco, wo)
    return pl.pallas_call(
        _mix_kernel,
        grid=(1,),
        in_specs=[_resident(a.shape) for a in ins],
        out_specs=pl.BlockSpec(h1.shape, lambda i: (0, 0)),
        out_shape=jax.ShapeDtypeStruct(h1.shape, F32),
        compiler_params=_params(1),
        name="mix_sample",
    )(*ins)


def kernel(x_prompt, x_sample, cache_k, cache_v, cache_conv, p_prompt, p_sample, norm_g,
           w1_gate, w1_up, w1_down, w_in, conv_w, rel_bias, w_att_out, w_conv_out, w_out,
           w2_gate, w2_up, w2_down, w_ple_gate, w_ple_proj):
    depth = norm_g.shape[0]
    assert depth == 1, "one layer per step"
    batch, seq, _ = x_prompt.shape
    dec_batch, dec_seq, _ = x_sample.shape
    assert seq % TOKEN_TILE == 0 and TOKEN_TILE % PAIR == 0 and TOKEN_TILE >= ATT_WINDOW
    l = 0

    g = norm_g[l]
    w1g = w1_gate[l].astype(BF16)
    w1u = w1_up[l].astype(BF16)
    w2g = w2_gate[l].astype(BF16)
    w2u = w2_up[l].astype(BF16)
    w1d = w1_down[l].astype(BF16)
    w2d = w2_down[l].astype(BF16)
    win = w_in[l].astype(BF16)
    wq = win[:, :D_ATT]
    wk = win[:, D_ATT:2 * D_ATT]
    wv = win[:, 2 * D_ATT:3 * D_ATT]
    wxbc = win[:, 3 * D_ATT:3 * D_ATT + 3 * D_CONV]
    wg = win[:, 3 * D_ATT + 3 * D_CONV:]
    wkt = wk.T
    wao = w_att_out[l].astype(BF16)
    wco = w_conv_out[l].astype(BF16)
    wo = w_out[l].astype(BF16)
    wpg = w_ple_gate[l].astype(BF16)
    wpp = w_ple_proj[l].astype(BF16)
    tab = rel_bias[l]
    convw = conv_w[l]

    xp = x_prompt.reshape(batch * seq, D_MODEL)
    h1p = _ffn_call(xp, g, w1g, w1u, w1d, 0, 1)
    h2p, k_p, v_p, c_p = _mixer_prompt_call(tab, h1p, g, wq, wkt, wk, wv, wxbc, wg, convw,
                                            wao, wco, wo, batch, seq)
    y_p = _ffn_ple_call(h2p, p_prompt[l].reshape(batch * seq, D_PLE), g, w2g, w2u, w2d, wpg, wpp,
                        4, 5, 6)

    lc = cache_k.shape[2]
    xs = x_sample.reshape(dec_batch * dec_seq, D_MODEL)
    h1s = _ffn_call(xs, g, w1g, w1u, w1d, 0, 1)
    q_s, k_s, v_s, xbc_s, gates_s = _inproj_call(h1s, g, wq, wk, wv, wxbc, wg)
    o_s, yc_s, c_s = _attn_sample_call(tab, q_s, k_s, v_s,
                                       cache_k[l], cache_v[l],
                                       xbc_s, cache_conv[l], convw, dec_batch, dec_seq)
    h2s = _mix_call(h1s, o_s, yc_s, gates_s, g, wao, wco, wo)
    y_s = _ffn_ple_call(h2s, p_sample[l].reshape(dec_batch * dec_seq, D_PLE), g, w2g, w2u, w2d,
                        wpg, wpp, 4, 5, 6)

    kv_p = (1, batch, ATT_WINDOW, N_HEADS, HEAD_DIM)
    kv_s = (1, dec_batch, dec_seq, N_HEADS, HEAD_DIM)
    return (y_p.reshape(batch, seq, D_MODEL), y_s.reshape(dec_batch, dec_seq, D_MODEL),
            k_p.reshape(kv_p), v_p.reshape(kv_p), c_p[None],
            k_s.reshape(kv_s), v_s.reshape(kv_s), c_s[None])
```

```python
import functools

import jax
import jax.numpy as jnp
from jax import lax
from jax.experimental import pallas as pl
from jax.experimental.pallas import tpu as pltpu

F32 = jnp.float32
BF16 = jnp.bfloat16

D_MODEL = 1024
CHUNK = 64
LEFT_CHUNKS = 8
ATT_WINDOW = LEFT_CHUNKS * CHUNK
BAND = ATT_WINDOW + CHUNK
N_HEADS = 8
HEAD_DIM = 64
D_ATT = N_HEADS * HEAD_DIM
D_CONV = D_MODEL // 2
CONV_WIDTH = 3
MAX_REL = 128
D_FF = 2816
D_PLE = 256
EPS = 1e-6
NEG_INF = -1e30

FF_CHUNK = 256
N_FF_CHUNKS = D_FF // FF_CHUNK
MIX_CHUNK = 256
PAIR = 2 * CHUNK
PAIR_KEYS = ATT_WINDOW + PAIR
ATT_LOOKAHEAD = 3
SCORE_SLOTS = ATT_LOOKAHEAD + 2
TOKEN_TILE = 512
FFN_TOKEN_TILE = 1024
FFN_ROW_GROUPS = 2
CONV_PAD = 8
VMEM_LIMIT_BYTES = 56 * 1024 * 1024


def _dot(a, b):
    return jnp.dot(a, b, preferred_element_type=F32)


def _dot_nt(a, b):
    return lax.dot_general(a, b, (((1,), (1,)), ((), ())), preferred_element_type=F32)


def _rmsnorm(x, g):
    return x * lax.rsqrt(jnp.mean(x * x, axis=-1, keepdims=True) + EPS) * g


def _sigmoid(x):
    return 1.0 / (1.0 + jnp.exp(-x))


def _resident(shape):
    return pl.BlockSpec(shape, lambda *_: (0,) * len(shape), pipeline_mode=pl.Buffered(1))


def _params(n_grid_dims):
    return pltpu.CompilerParams(
        dimension_semantics=("arbitrary",) * n_grid_dims,
        vmem_limit_bytes=VMEM_LIMIT_BYTES,
    )


def _swiglu_hidden(u, wg_ref, wu_ref, act_ref, chunks):
    for c in chunks:
        cs = slice(c * FF_CHUNK, (c + 1) * FF_CHUNK)
        gate = _dot(u, wg_ref[:, cs])
        up = _dot(u, wu_ref[:, cs])
        act_ref[:, cs] = (gate * _sigmoid(gate) * up).astype(BF16)


def _row_groups(n_rows):
    n = FFN_ROW_GROUPS if n_rows % FFN_TOKEN_TILE == 0 else 1
    return [slice(k * n_rows // n, (k + 1) * n_rows // n) for k in range(n)]


def _ffn_hidden_and_down(x_ref, g_pre, g_post, wg_ref, wu_ref, wd_ref, act_scr):
    groups = _row_groups(x_ref.shape[0])
    for r in groups:
        u = _rmsnorm(x_ref[r, :], g_pre).astype(BF16)
        _swiglu_hidden(u, wg_ref, wu_ref, act_scr.at[r, :], range(N_FF_CHUNKS))
    ys = [_dot(act_scr[r, :], wd_ref[...]) for r in groups]
    return groups, [x_ref[r, :] + 0.5 * _rmsnorm(y, g_post) for r, y in zip(groups, ys)]


def _ffn_kernel(x_ref, g_ref, wg_ref, wu_ref, wd_ref, o_ref, act_scr, *, pre, post):
    groups, hs = _ffn_hidden_and_down(x_ref, g_ref[pre:pre + 1, :], g_ref[post:post + 1, :],
                                      wg_ref, wu_ref, wd_ref, act_scr)
    for r, h in zip(groups, hs):
        o_ref[r, :] = h


def _ffn_ple_kernel(x_ref, p_ref, g_ref, wg_ref, wu_ref, wd_ref, wpg_ref, wpp_ref, o_ref, act_scr,
                    *, pre, post, ple):
    groups, hs = _ffn_hidden_and_down(x_ref, g_ref[pre:pre + 1, :], g_ref[post:post + 1, :],
                                      wg_ref, wu_ref, wd_ref, act_scr)
    gated = [_sigmoid(_dot(h.astype(BF16), wpg_ref[...]))
             * _dot(p_ref[r, :].astype(BF16), wpp_ref[...]) for r, h in zip(groups, hs)]
    for r, h, gp in zip(groups, hs, gated):
        o_ref[r, :] = h + _rmsnorm(gp, g_ref[ple:ple + 1, :])


def _ffn_call(x, norm_g, wg, wu, wd, pre, post):
    n = x.shape[0]
    tm = min(FFN_TOKEN_TILE, n)
    row = lambda i: (i, 0)
    return pl.pallas_call(
        functools.partial(_ffn_kernel, pre=pre, post=post),
        grid=(n // tm,),
        in_specs=[pl.BlockSpec((tm, D_MODEL), row), _resident(norm_g.shape),
                  _resident(wg.shape), _resident(wu.shape), _resident(wd.shape)],
        out_specs=pl.BlockSpec((tm, D_MODEL), row),
        out_shape=jax.ShapeDtypeStruct((n, D_MODEL), F32),
        scratch_shapes=[pltpu.VMEM((tm, D_FF), BF16)],
        compiler_params=_params(1),
        name="ffn",
    )(x, norm_g, wg, wu, wd)


def _ffn_ple_call(x, p, norm_g, wg, wu, wd, wpg, wpp, pre, post, ple):
    n = x.shape[0]
    tm = min(FFN_TOKEN_TILE, n)
    row = lambda i: (i, 0)
    return pl.pallas_call(
        functools.partial(_ffn_ple_kernel, pre=pre, post=post, ple=ple),
        grid=(n // tm,),
        in_specs=[pl.BlockSpec((tm, D_MODEL), row), pl.BlockSpec((tm, D_PLE), row),
                  _resident(norm_g.shape), _resident(wg.shape), _resident(wu.shape),
                  _resident(wd.shape), _resident(wpg.shape), _resident(wpp.shape)],
        out_specs=pl.BlockSpec((tm, D_MODEL), row),
        out_shape=jax.ShapeDtypeStruct((n, D_MODEL), F32),
        scratch_shapes=[pltpu.VMEM((tm, D_FF), BF16)],
        compiler_params=_params(1),
        name="ffn_ple",
    )(x, p, norm_g, wg, wu, wd, wpg, wpp)


def _build_bias(tab_ref, head, dist, lo, hi):
    def body(j, acc):
        v = lo + j
        return jnp.where(dist == v, tab_ref[head, v + MAX_REL], acc)
    return lax.fori_loop(0, hi - lo + 1, body, jnp.full(dist.shape, NEG_INF, F32))


def _short_conv(s, bgate, convw_ref, conv_scr):
    t = s.shape[0]
    conv_scr[CONV_PAD:CONV_PAD + t, :] = s
    y = (convw_ref[0:1, :] * conv_scr[CONV_PAD - 2:CONV_PAD - 2 + t, :]
         + convw_ref[1:2, :] * conv_scr[CONV_PAD - 1:CONV_PAD - 1 + t, :]
         + convw_ref[2:3, :] * s)
    return bgate * y


def _mix_out(h1, o_att, y_conv, gates, g_post, wao_ref, wco_ref, wo_ref):
    m = (_sigmoid(gates[:, :D_MODEL]) * _dot(o_att.astype(BF16), wao_ref[...])
         + _sigmoid(gates[:, D_MODEL:]) * _dot(y_conv.astype(BF16), wco_ref[...]))
    return h1 + _rmsnorm(_dot(m.astype(BF16), wo_ref[...]), g_post)


def _mixer_prompt_kernel(tab_ref, h1_ref, g_ref, wq_ref, wkt_ref, wk_ref, wv_ref, wxbc_ref, wg_ref,
                         convw_ref, wao_ref, wco_ref, wo_ref,
                         h2_ref, kout_ref, vout_ref, cout_ref,
                         q_scr, kt_scr, v_scr, bias_scr, s_scr, o_scr, conv_scr, yc_scr, u_scr, m_scr,
                         *, tm, tiles_per_seq):
    b = pl.program_id(0)
    i = pl.program_id(1)
    n_pairs = tm // PAIR

    @pl.when((b == 0) & (i == 0))
    def _():
        rows = 32
        for rb in range(PAIR // rows):
            r = lax.broadcasted_iota(jnp.int32, (rows, PAIR_KEYS), 0) + rb * rows
            c = lax.broadcasted_iota(jnp.int32, (rows, PAIR_KEYS), 1)
            n = c - (r & CHUNK)
            d = jnp.minimum((r & (CHUNK - 1)) + ATT_WINDOW - n, MAX_REL)
            d = jnp.where((n >= 0) & (n < BAND), d, MAX_REL + 1)

            def per_head(h, carry):
                bias_scr[h, rb * rows:(rb + 1) * rows, :] = _build_bias(
                    tab_ref, h, d, -(CHUNK - 1), MAX_REL)
                return carry
            lax.fori_loop(0, N_HEADS, per_head, 0)
        kt_scr[...] = jnp.zeros(kt_scr.shape, BF16)

    @pl.when(i == 0)
    def _():
        kt_scr[:, :, :ATT_WINDOW] = jnp.zeros((N_HEADS, 2 * HEAD_DIM, ATT_WINDOW), BF16)
        v_scr[:, :ATT_WINDOW, :] = jnp.zeros((N_HEADS, ATT_WINDOW, 2 * HEAD_DIM), BF16)
        conv_scr[0:CONV_PAD, :] = jnp.zeros((CONV_PAD, D_CONV), F32)

    u = _rmsnorm(h1_ref[...], g_ref[2:3, :]).astype(BF16)
    u_scr[...] = u

    for c in range(D_CONV // MIX_CHUNK):
        cs = slice(c * MIX_CHUNK, (c + 1) * MIX_CHUNK)
        x_in = _dot(u, wxbc_ref[:, cs])
        b_gate = _dot(u, wxbc_ref[:, D_CONV + c * MIX_CHUNK:D_CONV + (c + 1) * MIX_CHUNK])
        c_gate = _dot(u, wxbc_ref[:, 2 * D_CONV + c * MIX_CHUNK:2 * D_CONV + (c + 1) * MIX_CHUNK])
        s_conv = c_gate * x_in
        yc_scr[:, cs] = _short_conv(s_conv, b_gate, convw_ref.at[:, cs],
                                    conv_scr.at[:, cs]).astype(BF16)
    new_prefix = conv_scr[CONV_PAD + tm - 2:CONV_PAD + tm, :]
    conv_scr[CONV_PAD - 2:CONV_PAD, :] = new_prefix

    q = _dot(u, wq_ref[...]) * (HEAD_DIM ** -0.5)
    kt = _dot_nt(wkt_ref[...], u)
    v = _dot(u, wv_ref[...])
    low_half = lax.broadcasted_iota(jnp.int32, (tm, 2 * HEAD_DIM), 1) < HEAD_DIM
    for j in range(N_HEADS // 2):
        blk = slice(2 * j * HEAD_DIM, 2 * (j + 1) * HEAD_DIM)
        q_scr[j] = q[:, blk].astype(BF16)
        v_scr[2 * j, ATT_WINDOW:, :] = jnp.where(low_half, v[:, blk], 1.0).astype(BF16)
        v_scr[2 * j + 1, ATT_WINDOW:, :] = jnp.where(low_half, 1.0, v[:, blk]).astype(BF16)
    for h in range(N_HEADS):
        hs = slice(h * HEAD_DIM, (h + 1) * HEAD_DIM)
        own_rows = slice((h % 2) * HEAD_DIM, (h % 2 + 1) * HEAD_DIM)
        kt_scr[h, own_rows, ATT_WINDOW:] = kt[hs, :].astype(BF16)

    @pl.when(i == tiles_per_seq - 1)
    def _():
        kout_ref[0] = _dot(u, wk_ref[...])[tm - ATT_WINDOW:, :]
        vout_ref[0] = v[tm - ATT_WINDOW:, :]
        cout_ref[0] = new_prefix

    def attend(first_tile):
        col = lax.broadcasted_iota(jnp.int32, (1, PAIR_KEYS), 1)

        def scores(h, jj, slot):
            k0 = jj * PAIR
            s = _dot(q_scr[h // 2, k0:k0 + PAIR, :], kt_scr[h, :, k0:k0 + PAIR_KEYS])
            s = s + bias_scr[h]
            if first_tile:
                s = s + jnp.where(col + k0 < ATT_WINDOW, NEG_INF, 0.0)
            s_scr[slot] = s
            return jnp.max(s, axis=-1, keepdims=True)

        def weighted_values(h, jj, slot, m):
            k0 = jj * PAIR
            p = jnp.exp(s_scr[slot] - m).astype(BF16)
            o_scr[h, k0:k0 + PAIR, :] = _dot(p, v_scr[h, k0:k0 + PAIR_KEYS, :])

        steps = [(h, jj) for h in range(N_HEADS) for jj in range(n_pairs)]
        maxes = {}
        for k in range(len(steps) + ATT_LOOKAHEAD):
            if k < len(steps):
                maxes[k] = scores(*steps[k], k % SCORE_SLOTS)
            if k >= ATT_LOOKAHEAD:
                j = k - ATT_LOOKAHEAD
                weighted_values(*steps[j], j % SCORE_SLOTS, maxes.pop(j))

    @pl.when(i == 0)
    def _():
        attend(True)

    @pl.when(i > 0)
    def _():
        attend(False)

    u = u_scr[...]
    y_conv = yc_scr[...]
    pairs = []
    for j in range(N_HEADS // 2):
        even = o_scr[2 * j]
        odd = o_scr[2 * j + 1]
        pairs.append(jnp.where(low_half,
                               even * pltpu.roll(1.0 / even, HEAD_DIM, 1),
                               odd * pltpu.roll(1.0 / odd, HEAD_DIM, 1)))
    o_att = jnp.concatenate(pairs, axis=-1).astype(BF16)
    for c in range(D_MODEL // MIX_CHUNK):
        cs = slice(c * MIX_CHUNK, (c + 1) * MIX_CHUNK)
        g_conv = _dot(u, wg_ref[:, D_MODEL + c * MIX_CHUNK:D_MODEL + (c + 1) * MIX_CHUNK])
        m_conv = _sigmoid(g_conv) * _dot(y_conv, wco_ref[:, cs])
        g_att = _dot(u, wg_ref[:, cs])
        m_scr[:, cs] = (_sigmoid(g_att) * _dot(o_att, wao_ref[:, cs]) + m_conv).astype(BF16)
    kt_scr[:, :, :ATT_WINDOW] = kt_scr[:, :, tm:]
    v_scr[:, :ATT_WINDOW, :] = v_scr[:, tm:, :]
    h2_ref[...] = h1_ref[...] + _rmsnorm(_dot(m_scr[...], wo_ref[...]), g_ref[3:4, :])


def _mixer_prompt_call(tab, h1, norm_g, wq, wkt, wk, wv, wxbc, wg, convw, wao, wco, wo, batch, seq):
    tm = TOKEN_TILE
    tiles = seq // tm
    row = lambda b, i: (b * tiles + i, 0)
    per_seq = lambda b, i: (b, 0, 0)
    weights = (wq, wkt, wk, wv, wxbc, wg, convw, wao, wco, wo)
    return pl.pallas_call(
        functools.partial(_mixer_prompt_kernel, tm=tm, tiles_per_seq=tiles),
        grid=(batch, tiles),
        in_specs=[pl.BlockSpec(memory_space=pltpu.SMEM),
                  pl.BlockSpec((tm, D_MODEL), row), _resident(norm_g.shape)]
                 + [_resident(w.shape) for w in weights],
        out_specs=[pl.BlockSpec((tm, D_MODEL), row),
                   pl.BlockSpec((1, ATT_WINDOW, D_ATT), per_seq),
                   pl.BlockSpec((1, ATT_WINDOW, D_ATT), per_seq),
                   pl.BlockSpec((1, CONV_WIDTH - 1, D_CONV), per_seq)],
        out_shape=[jax.ShapeDtypeStruct((batch * seq, D_MODEL), F32),
                   jax.ShapeDtypeStruct((batch, ATT_WINDOW, D_ATT), F32),
                   jax.ShapeDtypeStruct((batch, ATT_WINDOW, D_ATT), F32),
                   jax.ShapeDtypeStruct((batch, CONV_WIDTH - 1, D_CONV), F32)],
        scratch_shapes=[pltpu.VMEM((N_HEADS // 2, tm, 2 * HEAD_DIM), BF16),
                        pltpu.VMEM((N_HEADS, 2 * HEAD_DIM, ATT_WINDOW + tm), BF16),
                        pltpu.VMEM((N_HEADS, ATT_WINDOW + tm, 2 * HEAD_DIM), BF16),
                        pltpu.VMEM((N_HEADS, PAIR, PAIR_KEYS), F32),
                        pltpu.VMEM((SCORE_SLOTS, PAIR, PAIR_KEYS), F32),
                        pltpu.VMEM((N_HEADS, tm, 2 * HEAD_DIM), F32),
                        pltpu.VMEM((CONV_PAD + tm, D_CONV), F32),
                        pltpu.VMEM((tm, D_CONV), BF16),
                        pltpu.VMEM((tm, D_MODEL), BF16),
                        pltpu.VMEM((tm, D_MODEL), BF16)],
        compiler_params=_params(2),
        name="mixer_prompt",
    )(tab, h1, norm_g, *weights)


def _inproj_kernel(h1_ref, g_ref, wq_ref, wk_ref, wv_ref, wxbc_ref, wg_ref,
                   q_ref, k_ref, v_ref, xbc_ref, gates_ref):
    u = _rmsnorm(h1_ref[...], g_ref[2:3, :]).astype(BF16)
    q_ref[...] = _dot(u, wq_ref[...]) * (HEAD_DIM ** -0.5)
    k_ref[...] = _dot(u, wk_ref[...])
    v_ref[...] = _dot(u, wv_ref[...])
    xbc_ref[...] = _dot(u, wxbc_ref[...])
    gates_ref[...] = _dot(u, wg_ref[...])


def _inproj_call(h1, norm_g, wq, wk, wv, wxbc, wg):
    n = h1.shape[0]
    ins = (h1, norm_g, wq, wk, wv, wxbc, wg)
    widths = (D_ATT, D_ATT, D_ATT, 3 * D_CONV, 2 * D_MODEL)
    return pl.pallas_call(
        _inproj_kernel,
        grid=(1,),
        in_specs=[_resident(a.shape) for a in ins],
        out_specs=[pl.BlockSpec((n, w), lambda i: (0, 0)) for w in widths],
        out_shape=[jax.ShapeDtypeStruct((n, w), F32) for w in widths],
        compiler_params=_params(1),
        name="inproj_sample",
    )(*ins)


def _attn_sample_kernel(tab_ref, q_ref, k_ref, v_ref, ck_ref, cv_ref, xbc_ref, cconv_ref, convw_ref,
                        o_ref, y_ref, cout_ref, bias_c_scr, bias_n_scr, conv_scr, *, t, lc):
    @pl.when(pl.program_id(0) == 0)
    def _():
        r = lax.broadcasted_iota(jnp.int32, (t, lc), 0)
        c = lax.broadcasted_iota(jnp.int32, (t, lc), 1)
        d_c = jnp.minimum(lc + r - c, MAX_REL)
        r = lax.broadcasted_iota(jnp.int32, (t, t), 0)
        c = lax.broadcasted_iota(jnp.int32, (t, t), 1)
        d_n = jnp.clip(r - c, -MAX_REL, MAX_REL)

        def per_head(h, carry):
            bias_c_scr[h] = _build_bias(tab_ref, h, d_c, 1, MAX_REL)
            bias_n_scr[h] = _build_bias(tab_ref, h, d_n, -min(t - 1, MAX_REL), min(t - 1, MAX_REL))
            return carry
        lax.fori_loop(0, N_HEADS, per_head, 0)

    rows = N_HEADS * t
    row_head = lax.broadcasted_iota(jnp.int32, (N_HEADS, t, D_ATT), 0).reshape(rows, D_ATT)
    col = lax.broadcasted_iota(jnp.int32, (rows, D_ATT), 1)
    own_head = (col >= row_head * HEAD_DIM) & (col < (row_head + 1) * HEAD_DIM)
    q_heads = jnp.where(own_head, jnp.concatenate([q_ref[...]] * N_HEADS, axis=0), 0.0).astype(BF16)
    k_new = k_ref[...].astype(BF16)
    v_new = v_ref[...].astype(BF16)
    s_c = _dot(q_heads, ck_ref[0].astype(BF16)) + bias_c_scr[...].reshape(rows, lc)
    s_n = _dot_nt(q_heads, k_new) + bias_n_scr[...].reshape(rows, t)
    m = jnp.maximum(jnp.max(s_c, axis=-1, keepdims=True), jnp.max(s_n, axis=-1, keepdims=True))
    p_c = jnp.exp(s_c - m)
    p_n = jnp.exp(s_n - m)
    l = jnp.sum(p_c, axis=-1, keepdims=True) + jnp.sum(p_n, axis=-1, keepdims=True)
    o_all = (_dot_nt(p_c.astype(BF16), cv_ref[0].astype(BF16))
             + _dot(p_n.astype(BF16), v_new)) * (1.0 / l)
    o_all = jnp.where(own_head, o_all, 0.0)
    o = o_all[0:t]
    for h in range(1, N_HEADS):
        o = o + o_all[h * t:(h + 1) * t]
    o_ref[...] = o

    xbc = xbc_ref[...]
    s_conv = xbc[:, 2 * D_CONV:] * xbc[:, :D_CONV]
    conv_scr[0:CONV_PAD, :] = jnp.zeros((CONV_PAD, D_CONV), F32)
    conv_scr[CONV_PAD - 2:CONV_PAD, :] = cconv_ref[0]
    y_ref[...] = _short_conv(s_conv, xbc[:, D_CONV:2 * D_CONV], convw_ref, conv_scr)
    cout_ref[0] = conv_scr[CONV_PAD + t - 2:CONV_PAD + t, :]


def _attn_sample_call(tab, q, k, v, cache_k, cache_v, xbc, cache_conv, convw, batch, t):
    lc = cache_k.shape[1]
    row = lambda b: (b, 0)
    per_seq = lambda b: (b, 0, 0)
    cache_k = cache_k.transpose(0, 2, 3, 1).reshape(batch, D_ATT, lc)
    cache_v = cache_v.transpose(0, 2, 3, 1).reshape(batch, D_ATT, lc)
    cache_spec = pl.BlockSpec((1, D_ATT, lc), per_seq)
    return pl.pallas_call(
        functools.partial(_attn_sample_kernel, t=t, lc=lc),
        grid=(batch,),
        in_specs=[pl.BlockSpec(memory_space=pltpu.SMEM),
                  pl.BlockSpec((t, D_ATT), row), pl.BlockSpec((t, D_ATT), row),
                  pl.BlockSpec((t, D_ATT), row),
                  cache_spec, cache_spec,
                  pl.BlockSpec((t, 3 * D_CONV), row),
                  pl.BlockSpec((1, CONV_WIDTH - 1, D_CONV), per_seq),
                  _resident(convw.shape)],
        out_specs=[pl.BlockSpec((t, D_ATT), row), pl.BlockSpec((t, D_CONV), row),
                   pl.BlockSpec((1, CONV_WIDTH - 1, D_CONV), per_seq)],
        out_shape=[jax.ShapeDtypeStruct((batch * t, D_ATT), F32),
                   jax.ShapeDtypeStruct((batch * t, D_CONV), F32),
                   jax.ShapeDtypeStruct((batch, CONV_WIDTH - 1, D_CONV), F32)],
        scratch_shapes=[pltpu.VMEM((N_HEADS, t, lc), F32),
                        pltpu.VMEM((N_HEADS, t, t), F32),
                        pltpu.VMEM((CONV_PAD + t, D_CONV), F32)],
        compiler_params=_params(1),
        name="attn_sample",
    )(tab, q, k, v, cache_k, cache_v, xbc, cache_conv, convw)


def _mix_kernel(h1_ref, o_ref, y_ref, gates_ref, g_ref, wao_ref, wco_ref, wo_ref, h2_ref):
    h2_ref[...] = _mix_out(h1_ref[...], o_ref[...], y_ref[...], gates_ref[...], g_ref[3:4, :],
                           wao_ref, wco_ref, wo_ref)


def _mix_call(h1, o_att, y_conv, gates, norm_g, wao, wco, wo):
    ins = (h1, o_att, y_conv, gates, norm_g, wao, wco, wo)
    return pl.pallas_call(
        _mix_kernel,
        grid=(1,),
        in_specs=[_resident(a.shape) for a in ins],
        out_specs=pl.BlockSpec(h1.shape, lambda i: (0, 0)),
        out_shape=jax.ShapeDtypeStruct(h1.shape, F32),
        compiler_params=_params(1),
        name="mix_sample",
    )(*ins)


def kernel(x_prompt, x_sample, cache_k, cache_v, cache_conv, p_prompt, p_sample, norm_g,
           w1_gate, w1_up, w1_down, w_in, conv_w, rel_bias, w_att_out, w_conv_out, w_out,
           w2_gate, w2_up, w2_down, w_ple_gate, w_ple_proj):
    depth = norm_g.shape[0]
    assert depth == 1, "one layer per step"
    batch, seq, _ = x_prompt.shape
    dec_batch, dec_seq, _ = x_sample.shape
    assert seq % TOKEN_TILE == 0 and TOKEN_TILE % PAIR == 0 and TOKEN_TILE >= ATT_WINDOW
    l = 0

    g = norm_g[l]
    w1g = w1_gate[l].astype(BF16)
    w1u = w1_up[l].astype(BF16)
    w2g = w2_gate[l].astype(BF16)
    w2u = w2_up[l].astype(BF16)
    w1d = w1_down[l].astype(BF16)
    w2d = w2_down[l].astype(BF16)
    win = w_in[l].astype(BF16)
    wq = win[:, :D_ATT]
    wk = win[:, D_ATT:2 * D_ATT]
    wv = win[:, 2 * D_ATT:3 * D_ATT]
    wxbc = win[:, 3 * D_ATT:3 * D_ATT + 3 * D_CONV]
    wg = win[:, 3 * D_ATT + 3 * D_CONV:]
    wkt = wk.T
    wao = w_att_out[l].astype(BF16)
    wco = w_conv_out[l].astype(BF16)
    wo = w_out[l].astype(BF16)
    wpg = w_ple_gate[l].astype(BF16)
    wpp = w_ple_proj[l].astype(BF16)
    tab = rel_bias[l]
    convw = conv_w[l]

    xp = x_prompt.reshape(batch * seq, D_MODEL)
    h1p = _ffn_call(xp, g, w1g, w1u, w1d, 0, 1)
    h2p, k_p, v_p, c_p = _mixer_prompt_call(tab, h1p, g, wq, wkt, wk, wv, wxbc, wg, convw,
                                            wao, wco, wo, batch, seq)
    y_p = _ffn_ple_call(h2p, p_prompt[l].reshape(batch * seq, D_PLE), g, w2g, w2u, w2d, wpg, wpp,
                        4, 5, 6)

    lc = cache_k.shape[2]
    xs = x_sample.reshape(dec_batch * dec_seq, D_MODEL)
    h1s = _ffn_call(xs, g, w1g, w1u, w1d, 0, 1)
    q_s, k_s, v_s, xbc_s, gates_s = _inproj_call(h1s, g, wq, wk, wv, wxbc, wg)
    o_s, yc_s, c_s = _attn_sample_call(tab, q_s, k_s, v_s,
                                       cache_k[l], cache_v[l],
                                       xbc_s, cache_conv[l], convw, dec_batch, dec_seq)
    h2s = _mix_call(h1s, o_s, yc_s, gates_s, g, wao, wco, wo)
    y_s = _ffn_ple_call(h2s, p_sample[l].reshape(dec_batch * dec_seq, D_PLE), g, w2g, w2u, w2d,
                        wpg, wpp, 4, 5, 6)

    kv_p = (1, batch, ATT_WINDOW, N_HEADS, HEAD_DIM)
    kv_s = (1, dec_batch, dec_seq, N_HEADS, HEAD_DIM)
    return (y_p.reshape(batch, seq, D_MODEL), y_s.reshape(dec_batch, dec_seq, D_MODEL),
            k_p.reshape(kv_p), v_p.reshape(kv_p), c_p[None],
            k_s.reshape(kv_s), v_s.reshape(kv_s), c_s[None])
```

```python
import functools

import jax
import jax.numpy as jnp
from jax import lax
from jax.experimental import pallas as pl
from jax.experimental.pallas import tpu as pltpu

F32 = jnp.float32
BF16 = jnp.bfloat16

D_MODEL = 1024
CHUNK = 64
LEFT_CHUNKS = 8
ATT_WINDOW = LEFT_CHUNKS * CHUNK
BAND = ATT_WINDOW + CHUNK
N_HEADS = 8
HEAD_DIM = 64
D_ATT = N_HEADS * HEAD_DIM
D_CONV = D_MODEL // 2
CONV_WIDTH = 3
MAX_REL = 128
D_FF = 2816
D_PLE = 256
EPS = 1e-6
NEG_INF = -1e30

FF_CHUNK = 256
N_FF_CHUNKS = D_FF // FF_CHUNK
MIX_CHUNK = 256
PAIR = 2 * CHUNK
PAIR_KEYS = ATT_WINDOW + PAIR
ATT_LOOKAHEAD = 3
SCORE_SLOTS = ATT_LOOKAHEAD + 2
TOKEN_TILE = 512
FFN_TOKEN_TILE = 1024
FFN_ROW_GROUPS = 2
SAMPLE_SEQS_PER_STEP = 4
CONV_PAD = 8
VMEM_LIMIT_BYTES = 56 * 1024 * 1024


def _dot(a, b):
    return jnp.dot(a, b, preferred_element_type=F32)


def _dot_nt(a, b):
    return lax.dot_general(a, b, (((1,), (1,)), ((), ())), preferred_element_type=F32)


def _rmsnorm(x, g):
    return x * lax.rsqrt(jnp.mean(x * x, axis=-1, keepdims=True) + EPS) * g


def _sigmoid(x):
    return 1.0 / (1.0 + jnp.exp(-x))


def _resident(shape):
    return pl.BlockSpec(shape, lambda *_: (0,) * len(shape), pipeline_mode=pl.Buffered(1))


def _params(n_grid_dims):
    return pltpu.CompilerParams(
        dimension_semantics=("arbitrary",) * n_grid_dims,
        vmem_limit_bytes=VMEM_LIMIT_BYTES,
    )


def _swiglu_hidden(u, wg_ref, wu_ref, act_ref, chunks):
    for c in chunks:
        cs = slice(c * FF_CHUNK, (c + 1) * FF_CHUNK)
        gate = _dot(u, wg_ref[:, cs])
        up = _dot(u, wu_ref[:, cs])
        act_ref[:, cs] = (gate * _sigmoid(gate) * up).astype(BF16)


def _row_groups(n_rows):
    n = FFN_ROW_GROUPS if n_rows % FFN_TOKEN_TILE == 0 else 1
    return [slice(k * n_rows // n, (k + 1) * n_rows // n) for k in range(n)]


def _ffn_hidden_and_down(x_ref, g_pre, g_post, wg_ref, wu_ref, wd_ref, act_scr):
    groups = _row_groups(x_ref.shape[0])
    for r in groups:
        u = _rmsnorm(x_ref[r, :], g_pre).astype(BF16)
        _swiglu_hidden(u, wg_ref, wu_ref, act_scr.at[r, :], range(N_FF_CHUNKS))
    ys = [_dot(act_scr[r, :], wd_ref[...]) for r in groups]
    return groups, [x_ref[r, :] + 0.5 * _rmsnorm(y, g_post) for r, y in zip(groups, ys)]


def _ffn_kernel(x_ref, g_ref, wg_ref, wu_ref, wd_ref, o_ref, act_scr, *, pre, post):
    groups, hs = _ffn_hidden_and_down(x_ref, g_ref[pre:pre + 1, :], g_ref[post:post + 1, :],
                                      wg_ref, wu_ref, wd_ref, act_scr)
    for r, h in zip(groups, hs):
        o_ref[r, :] = h


def _ffn_ple_kernel(x_ref, p_ref, g_ref, wg_ref, wu_ref, wd_ref, wpg_ref, wpp_ref, o_ref, act_scr,
                    *, pre, post, ple):
    groups, hs = _ffn_hidden_and_down(x_ref, g_ref[pre:pre + 1, :], g_ref[post:post + 1, :],
                                      wg_ref, wu_ref, wd_ref, act_scr)
    gated = [_sigmoid(_dot(h.astype(BF16), wpg_ref[...]))
             * _dot(p_ref[r, :].astype(BF16), wpp_ref[...]) for r, h in zip(groups, hs)]
    for r, h, gp in zip(groups, hs, gated):
        o_ref[r, :] = h + _rmsnorm(gp, g_ref[ple:ple + 1, :])


def _ffn_call(x, norm_g, wg, wu, wd, pre, post):
    n = x.shape[0]
    tm = min(FFN_TOKEN_TILE, n)
    row = lambda i: (i, 0)
    return pl.pallas_call(
        functools.partial(_ffn_kernel, pre=pre, post=post),
        grid=(n // tm,),
        in_specs=[pl.BlockSpec((tm, D_MODEL), row), _resident(norm_g.shape),
                  _resident(wg.shape), _resident(wu.shape), _resident(wd.shape)],
        out_specs=pl.BlockSpec((tm, D_MODEL), row),
        out_shape=jax.ShapeDtypeStruct((n, D_MODEL), F32),
        scratch_shapes=[pltpu.VMEM((tm, D_FF), BF16)],
        compiler_params=_params(1),
        name="ffn",
    )(x, norm_g, wg, wu, wd)


def _ffn_ple_call(x, p, norm_g, wg, wu, wd, wpg, wpp, pre, post, ple):
    n = x.shape[0]
    tm = min(FFN_TOKEN_TILE, n)
    row = lambda i: (i, 0)
    return pl.pallas_call(
        functools.partial(_ffn_ple_kernel, pre=pre, post=post, ple=ple),
        grid=(n // tm,),
        in_specs=[pl.BlockSpec((tm, D_MODEL), row), pl.BlockSpec((tm, D_PLE), row),
                  _resident(norm_g.shape), _resident(wg.shape), _resident(wu.shape),
                  _resident(wd.shape), _resident(wpg.shape), _resident(wpp.shape)],
        out_specs=pl.BlockSpec((tm, D_MODEL), row),
        out_shape=jax.ShapeDtypeStruct((n, D_MODEL), F32),
        scratch_shapes=[pltpu.VMEM((tm, D_FF), BF16)],
        compiler_params=_params(1),
        name="ffn_ple",
    )(x, p, norm_g, wg, wu, wd, wpg, wpp)


def _build_bias(tab_ref, head, dist, lo, hi):
    def body(j, acc):
        v = lo + j
        return jnp.where(dist == v, tab_ref[head, v + MAX_REL], acc)
    return lax.fori_loop(0, hi - lo + 1, body, jnp.full(dist.shape, NEG_INF, F32))


def _short_conv(s, bgate, convw_ref, conv_scr):
    t = s.shape[0]
    conv_scr[CONV_PAD:CONV_PAD + t, :] = s
    y = (convw_ref[0:1, :] * conv_scr[CONV_PAD - 2:CONV_PAD - 2 + t, :]
         + convw_ref[1:2, :] * conv_scr[CONV_PAD - 1:CONV_PAD - 1 + t, :]
         + convw_ref[2:3, :] * s)
    return bgate * y


def _mix_out(h1, o_att, y_conv, gates, g_post, wao_ref, wco_ref, wo_ref):
    m = (_sigmoid(gates[:, :D_MODEL]) * _dot(o_att.astype(BF16), wao_ref[...])
         + _sigmoid(gates[:, D_MODEL:]) * _dot(y_conv.astype(BF16), wco_ref[...]))
    return h1 + _rmsnorm(_dot(m.astype(BF16), wo_ref[...]), g_post)


def _mixer_prompt_kernel(tab_ref, h1_ref, g_ref, wq_ref, wkt_ref, wk_ref, wv_ref, wxbc_ref, wg_ref,
                         convw_ref, wao_ref, wco_ref, wo_ref,
                         h2_ref, kout_ref, vout_ref, cout_ref,
                         q_scr, kt_scr, v_scr, bias_scr, s_scr, o_scr, conv_scr, yc_scr, u_scr, m_scr,
                         *, tm, tiles_per_seq):
    b = pl.program_id(0)
    i = pl.program_id(1)
    n_pairs = tm // PAIR

    @pl.when((b == 0) & (i == 0))
    def _():
        rows = 32
        for rb in range(PAIR // rows):
            r = lax.broadcasted_iota(jnp.int32, (rows, PAIR_KEYS), 0) + rb * rows
            c = lax.broadcasted_iota(jnp.int32, (rows, PAIR_KEYS), 1)
            n = c - (r & CHUNK)
            d = jnp.minimum((r & (CHUNK - 1)) + ATT_WINDOW - n, MAX_REL)
            d = jnp.where((n >= 0) & (n < BAND), d, MAX_REL + 1)

            def per_head(h, carry):
                bias_scr[h, rb * rows:(rb + 1) * rows, :] = _build_bias(
                    tab_ref, h, d, -(CHUNK - 1), MAX_REL)
                return carry
            lax.fori_loop(0, N_HEADS, per_head, 0)

    @pl.when(i == 0)
    def _():
        kt_scr[:, :, :ATT_WINDOW] = jnp.zeros((N_HEADS, HEAD_DIM, ATT_WINDOW), BF16)
        v_scr[:, :ATT_WINDOW, :] = jnp.zeros((N_HEADS, ATT_WINDOW, 2 * HEAD_DIM), BF16)
        conv_scr[0:CONV_PAD, :] = jnp.zeros((CONV_PAD, D_CONV), F32)

    u = _rmsnorm(h1_ref[...], g_ref[2:3, :]).astype(BF16)
    u_scr[...] = u

    for c in range(D_CONV // MIX_CHUNK):
        cs = slice(c * MIX_CHUNK, (c + 1) * MIX_CHUNK)
        x_in = _dot(u, wxbc_ref[:, cs])
        b_gate = _dot(u, wxbc_ref[:, D_CONV + c * MIX_CHUNK:D_CONV + (c + 1) * MIX_CHUNK])
        c_gate = _dot(u, wxbc_ref[:, 2 * D_CONV + c * MIX_CHUNK:2 * D_CONV + (c + 1) * MIX_CHUNK])
        s_conv = c_gate * x_in
        yc_scr[:, cs] = _short_conv(s_conv, b_gate, convw_ref.at[:, cs],
                                    conv_scr.at[:, cs]).astype(BF16)
    new_prefix = conv_scr[CONV_PAD + tm - 2:CONV_PAD + tm, :]
    conv_scr[CONV_PAD - 2:CONV_PAD, :] = new_prefix

    q = _dot(u, wq_ref[...]) * (HEAD_DIM ** -0.5)
    kt = _dot_nt(wkt_ref[...], u)
    v = _dot(u, wv_ref[...])
    ones = jnp.ones((tm, HEAD_DIM), F32)
    for h in range(N_HEADS):
        hs = slice(h * HEAD_DIM, (h + 1) * HEAD_DIM)
        q_scr[h] = q[:, hs].astype(BF16)
        kt_scr[h, :, ATT_WINDOW:] = kt[hs, :].astype(BF16)
        v_scr[h, ATT_WINDOW:, :] = jnp.concatenate([v[:, hs], ones], axis=-1).astype(BF16)

    @pl.when(i == tiles_per_seq - 1)
    def _():
        kout_ref[0] = _dot(u, wk_ref[...])[tm - ATT_WINDOW:, :]
        vout_ref[0] = v[tm - ATT_WINDOW:, :]
        cout_ref[0] = new_prefix

    def attend(first_tile):
        col = lax.broadcasted_iota(jnp.int32, (1, PAIR_KEYS), 1)

        def scores(h, jj, slot):
            k0 = jj * PAIR
            s = _dot(q_scr[h, k0:k0 + PAIR, :], kt_scr[h, :, k0:k0 + PAIR_KEYS])
            s = s + bias_scr[h]
            if first_tile:
                s = s + jnp.where(col + k0 < ATT_WINDOW, NEG_INF, 0.0)
            s_scr[slot] = s
            return jnp.max(s, axis=-1, keepdims=True)

        def weighted_values(h, jj, slot, m):
            k0 = jj * PAIR
            p = jnp.exp(s_scr[slot] - m).astype(BF16)
            o_scr[h, k0:k0 + PAIR, :] = _dot(p, v_scr[h, k0:k0 + PAIR_KEYS, :])

        steps = [(h, jj) for h in range(N_HEADS) for jj in range(n_pairs)]
        maxes = {}
        for k in range(len(steps) + ATT_LOOKAHEAD):
            if k < len(steps):
                maxes[k] = scores(*steps[k], k % SCORE_SLOTS)
            if k >= ATT_LOOKAHEAD:
                j = k - ATT_LOOKAHEAD
                weighted_values(*steps[j], j % SCORE_SLOTS, maxes.pop(j))

    @pl.when(i == 0)
    def _():
        attend(True)

    @pl.when(i > 0)
    def _():
        attend(False)

    u = u_scr[...]
    y_conv = yc_scr[...]
    low_half = lax.broadcasted_iota(jnp.int32, (tm, 2 * HEAD_DIM), 1) < HEAD_DIM
    pairs = []
    for j in range(N_HEADS // 2):
        even = o_scr[2 * j]
        odd = o_scr[2 * j + 1]
        pairs.append(jnp.where(low_half,
                               even * pltpu.roll(1.0 / even, HEAD_DIM, 1),
                               pltpu.roll(odd, HEAD_DIM, 1) * (1.0 / odd)))
    o_att = jnp.concatenate(pairs, axis=-1).astype(BF16)
    for c in range(D_MODEL // MIX_CHUNK):
        cs = slice(c * MIX_CHUNK, (c + 1) * MIX_CHUNK)
        g_conv = _dot(u, wg_ref[:, D_MODEL + c * MIX_CHUNK:D_MODEL + (c + 1) * MIX_CHUNK])
        m_conv = _sigmoid(g_conv) * _dot(y_conv, wco_ref[:, cs])
        g_att = _dot(u, wg_ref[:, cs])
        m_scr[:, cs] = (_sigmoid(g_att) * _dot(o_att, wao_ref[:, cs]) + m_conv).astype(BF16)
    kt_scr[:, :, :ATT_WINDOW] = kt_scr[:, :, tm:]
    v_scr[:, :ATT_WINDOW, :] = v_scr[:, tm:, :]
    h2_ref[...] = h1_ref[...] + _rmsnorm(_dot(m_scr[...], wo_ref[...]), g_ref[3:4, :])


def _mixer_prompt_call(tab, h1, norm_g, wq, wkt, wk, wv, wxbc, wg, convw, wao, wco, wo, batch, seq):
    tm = TOKEN_TILE
    tiles = seq // tm
    row = lambda b, i: (b * tiles + i, 0)
    per_seq = lambda b, i: (b, 0, 0)
    weights = (wq, wkt, wk, wv, wxbc, wg, convw, wao, wco, wo)
    return pl.pallas_call(
        functools.partial(_mixer_prompt_kernel, tm=tm, tiles_per_seq=tiles),
        grid=(batch, tiles),
        in_specs=[pl.BlockSpec(memory_space=pltpu.SMEM),
                  pl.BlockSpec((tm, D_MODEL), row), _resident(norm_g.shape)]
                 + [_resident(w.shape) for w in weights],
        out_specs=[pl.BlockSpec((tm, D_MODEL), row),
                   pl.BlockSpec((1, ATT_WINDOW, D_ATT), per_seq),
                   pl.BlockSpec((1, ATT_WINDOW, D_ATT), per_seq),
                   pl.BlockSpec((1, CONV_WIDTH - 1, D_CONV), per_seq)],
        out_shape=[jax.ShapeDtypeStruct((batch * seq, D_MODEL), F32),
                   jax.ShapeDtypeStruct((batch, ATT_WINDOW, D_ATT), F32),
                   jax.ShapeDtypeStruct((batch, ATT_WINDOW, D_ATT), F32),
                   jax.ShapeDtypeStruct((batch, CONV_WIDTH - 1, D_CONV), F32)],
        scratch_shapes=[pltpu.VMEM((N_HEADS, tm, HEAD_DIM), BF16),
                        pltpu.VMEM((N_HEADS, HEAD_DIM, ATT_WINDOW + tm), BF16),
                        pltpu.VMEM((N_HEADS, ATT_WINDOW + tm, 2 * HEAD_DIM), BF16),
                        pltpu.VMEM((N_HEADS, PAIR, PAIR_KEYS), F32),
                        pltpu.VMEM((SCORE_SLOTS, PAIR, PAIR_KEYS), F32),
                        pltpu.VMEM((N_HEADS, tm, 2 * HEAD_DIM), F32),
                        pltpu.VMEM((CONV_PAD + tm, D_CONV), F32),
                        pltpu.VMEM((tm, D_CONV), BF16),
                        pltpu.VMEM((tm, D_MODEL), BF16),
                        pltpu.VMEM((tm, D_MODEL), BF16)],
        compiler_params=_params(2),
        name="mixer_prompt",
    )(tab, h1, norm_g, *weights)


def _inproj_kernel(h1_ref, g_ref, wq_ref, wk_ref, wv_ref, wxbc_ref, wg_ref,
                   q_ref, k_ref, v_ref, xbc_ref, gates_ref):
    u = _rmsnorm(h1_ref[...], g_ref[2:3, :]).astype(BF16)
    q_ref[...] = _dot(u, wq_ref[...]) * (HEAD_DIM ** -0.5)
    k_ref[...] = _dot(u, wk_ref[...])
    v_ref[...] = _dot(u, wv_ref[...])
    xbc_ref[...] = _dot(u, wxbc_ref[...])
    gates_ref[...] = _dot(u, wg_ref[...])


def _inproj_call(h1, norm_g, wq, wk, wv, wxbc, wg):
    n = h1.shape[0]
    ins = (h1, norm_g, wq, wk, wv, wxbc, wg)
    widths = (D_ATT, D_ATT, D_ATT, 3 * D_CONV, 2 * D_MODEL)
    return pl.pallas_call(
        _inproj_kernel,
        grid=(1,),
        in_specs=[_resident(a.shape) for a in ins],
        out_specs=[pl.BlockSpec((n, w), lambda i: (0, 0)) for w in widths],
        out_shape=[jax.ShapeDtypeStruct((n, w), F32) for w in widths],
        compiler_params=_params(1),
        name="inproj_sample",
    )(*ins)


def _attn_sample_kernel(tab_ref, q_ref, k_ref, v_ref, ck_ref, cv_ref, xbc_ref, cconv_ref, convw_ref,
                        o_ref, y_ref, cout_ref, bias_c_scr, bias_n_scr, conv_scr, *, t, lc):
    @pl.when(pl.program_id(0) == 0)
    def _():
        r = lax.broadcasted_iota(jnp.int32, (t, lc), 0)
        c = lax.broadcasted_iota(jnp.int32, (t, lc), 1)
        d_c = jnp.minimum(lc + r - c, MAX_REL)
        r = lax.broadcasted_iota(jnp.int32, (t, t), 0)
        c = lax.broadcasted_iota(jnp.int32, (t, t), 1)
        d_n = jnp.clip(r - c, -MAX_REL, MAX_REL)

        def per_head(h, carry):
            bias_c_scr[h] = _build_bias(tab_ref, h, d_c, 1, MAX_REL)
            bias_n_scr[h] = _build_bias(tab_ref, h, d_n, -min(t - 1, MAX_REL), min(t - 1, MAX_REL))
            return carry
        lax.fori_loop(0, N_HEADS, per_head, 0)

    rows = N_HEADS * t
    row_head = lax.broadcasted_iota(jnp.int32, (N_HEADS, t, D_ATT), 0).reshape(rows, D_ATT)
    col = lax.broadcasted_iota(jnp.int32, (rows, D_ATT), 1)
    own_head = (col >= row_head * HEAD_DIM) & (col < (row_head + 1) * HEAD_DIM)
    bias_c = bias_c_scr[...].reshape(rows, lc)
    bias_n = bias_n_scr[...].reshape(rows, t)
    conv_scr[0:CONV_PAD, :] = jnp.zeros((CONV_PAD, D_CONV), F32)

    for sq in range(q_ref.shape[0] // t):
        rs = slice(sq * t, (sq + 1) * t)
        q_heads = jnp.where(own_head, jnp.concatenate([q_ref[rs, :]] * N_HEADS, axis=0),
                            0.0).astype(BF16)
        k_new = k_ref[rs, :].astype(BF16)
        v_new = v_ref[rs, :].astype(BF16)
        s_c = _dot(q_heads, ck_ref[sq].astype(BF16)) + bias_c
        s_n = _dot_nt(q_heads, k_new) + bias_n
        m = jnp.maximum(jnp.max(s_c, axis=-1, keepdims=True),
                        jnp.max(s_n, axis=-1, keepdims=True))
        p_c = jnp.exp(s_c - m)
        p_n = jnp.exp(s_n - m)
        l = jnp.sum(p_c, axis=-1, keepdims=True) + jnp.sum(p_n, axis=-1, keepdims=True)
        o_all = (_dot_nt(p_c.astype(BF16), cv_ref[sq].astype(BF16))
                 + _dot(p_n.astype(BF16), v_new)) * (1.0 / l)
        o_all = jnp.where(own_head, o_all, 0.0)
        o = o_all[0:t]
        for h in range(1, N_HEADS):
            o = o + o_all[h * t:(h + 1) * t]
        o_ref[rs, :] = o

        xbc = xbc_ref[rs, :]
        s_conv = xbc[:, 2 * D_CONV:] * xbc[:, :D_CONV]
        conv_scr[CONV_PAD - 2:CONV_PAD, :] = cconv_ref[sq]
        y_ref[rs, :] = _short_conv(s_conv, xbc[:, D_CONV:2 * D_CONV], convw_ref, conv_scr)
        cout_ref[sq] = conv_scr[CONV_PAD + t - 2:CONV_PAD + t, :]


def _attn_sample_call(tab, q, k, v, cache_k, cache_v, xbc, cache_conv, convw, batch, t):
    lc = cache_k.shape[1]
    n_seq = SAMPLE_SEQS_PER_STEP if batch % SAMPLE_SEQS_PER_STEP == 0 else 1
    row = lambda b: (b, 0)
    per_seq = lambda b: (b, 0, 0)
    cache_k = cache_k.transpose(0, 2, 3, 1).reshape(batch, D_ATT, lc)
    cache_v = cache_v.transpose(0, 2, 3, 1).reshape(batch, D_ATT, lc)
    cache_spec = pl.BlockSpec((n_seq, D_ATT, lc), per_seq)
    return pl.pallas_call(
        functools.partial(_attn_sample_kernel, t=t, lc=lc),
        grid=(batch // n_seq,),
        in_specs=[pl.BlockSpec(memory_space=pltpu.SMEM),
                  pl.BlockSpec((n_seq * t, D_ATT), row), pl.BlockSpec((n_seq * t, D_ATT), row),
                  pl.BlockSpec((n_seq * t, D_ATT), row),
                  cache_spec, cache_spec,
                  pl.BlockSpec((n_seq * t, 3 * D_CONV), row),
                  pl.BlockSpec((n_seq, CONV_WIDTH - 1, D_CONV), per_seq),
                  _resident(convw.shape)],
        out_specs=[pl.BlockSpec((n_seq * t, D_ATT), row), pl.BlockSpec((n_seq * t, D_CONV), row),
                   pl.BlockSpec((n_seq, CONV_WIDTH - 1, D_CONV), per_seq)],
        out_shape=[jax.ShapeDtypeStruct((batch * t, D_ATT), F32),
                   jax.ShapeDtypeStruct((batch * t, D_CONV), F32),
                   jax.ShapeDtypeStruct((batch, CONV_WIDTH - 1, D_CONV), F32)],
        scratch_shapes=[pltpu.VMEM((N_HEADS, t, lc), F32),
                        pltpu.VMEM((N_HEADS, t, t), F32),
                        pltpu.VMEM((CONV_PAD + t, D_CONV), F32)],
        compiler_params=_params(1),
        name="attn_sample",
    )(tab, q, k, v, cache_k, cache_v, xbc, cache_conv, convw)


def _mix_kernel(h1_ref, o_ref, y_ref, gates_ref, g_ref, wao_ref, wco_ref, wo_ref, h2_ref):
    h2_ref[...] = _mix_out(h1_ref[...], o_ref[...], y_ref[...], gates_ref[...], g_ref[3:4, :],
                           wao_ref, wco_ref, wo_ref)


def _mix_call(h1, o_att, y_conv, gates, norm_g, wao, wco, wo):
    ins = (h1, o_att, y_conv, gates, norm_g, wao, wco, wo)
    return pl.pallas_call(
        _mix_kernel,
        grid=(1,),
        in_specs=[_resident(a.shape) for a in ins],
        out_specs=pl.BlockSpec(h1.shape, lambda i: (0, 0)),
        out_shape=jax.ShapeDtypeStruct(h1.shape, F32),
        compiler_params=_params(1),
        name="mix_sample",
    )(*ins)


def kernel(x_prompt, x_sample, cache_k, cache_v, cache_conv, p_prompt, p_sample, norm_g,
           w1_gate, w1_up, w1_down, w_in, conv_w, rel_bias, w_att_out, w_conv_out, w_out,
           w2_gate, w2_up, w2_down, w_ple_gate, w_ple_proj):
    depth = norm_g.shape[0]
    assert depth == 1, "one layer per step"
    batch, seq, _ = x_prompt.shape
    dec_batch, dec_seq, _ = x_sample.shape
    assert seq % TOKEN_TILE == 0 and TOKEN_TILE % PAIR == 0 and TOKEN_TILE >= ATT_WINDOW
    l = 0

    g = norm_g[l]
    w1g = w1_gate[l].astype(BF16)
    w1u = w1_up[l].astype(BF16)
    w2g = w2_gate[l].astype(BF16)
    w2u = w2_up[l].astype(BF16)
    w1d = w1_down[l].astype(BF16)
    w2d = w2_down[l].astype(BF16)
    win = w_in[l].astype(BF16)
    wq = win[:, :D_ATT]
    wk = win[:, D_ATT:2 * D_ATT]
    wv = win[:, 2 * D_ATT:3 * D_ATT]
    wxbc = win[:, 3 * D_ATT:3 * D_ATT + 3 * D_CONV]
    wg = win[:, 3 * D_ATT + 3 * D_CONV:]
    wkt = wk.T
    wao = w_att_out[l].astype(BF16)
    wco = w_conv_out[l].astype(BF16)
    wo = w_out[l].astype(BF16)
    wpg = w_ple_gate[l].astype(BF16)
    wpp = w_ple_proj[l].astype(BF16)
    tab = rel_bias[l]
    convw = conv_w[l]

    xp = x_prompt.reshape(batch * seq, D_MODEL)
    h1p = _ffn_call(xp, g, w1g, w1u, w1d, 0, 1)
    h2p, k_p, v_p, c_p = _mixer_prompt_call(tab, h1p, g, wq, wkt, wk, wv, wxbc, wg, convw,
                                            wao, wco, wo, batch, seq)
    y_p = _ffn_ple_call(h2p, p_prompt[l].reshape(batch * seq, D_PLE), g, w2g, w2u, w2d, wpg, wpp,
                        4, 5, 6)

    lc = cache_k.shape[2]
    xs = x_sample.reshape(dec_batch * dec_seq, D_MODEL)
    h1s = _ffn_call(xs, g, w1g, w1u, w1d, 0, 1)
    q_s, k_s, v_s, xbc_s, gates_s = _inproj_call(h1s, g, wq, wk, wv, wxbc, wg)
    o_s, yc_s, c_s = _attn_sample_call(tab, q_s, k_s, v_s,
                                       cache_k[l], cache_v[l],
                                       xbc_s, cache_conv[l], convw, dec_batch, dec_seq)
    h2s = _mix_call(h1s, o_s, yc_s, gates_s, g, wao, wco, wo)
    y_s = _ffn_ple_call(h2s, p_sample[l].reshape(dec_batch * dec_seq, D_PLE), g, w2g, w2u, w2d,
                        wpg, wpp, 4, 5, 6)

    kv_p = (1, batch, ATT_WINDOW, N_HEADS, HEAD_DIM)
    kv_s = (1, dec_batch, dec_seq, N_HEADS, HEAD_DIM)
    return (y_p.reshape(batch, seq, D_MODEL), y_s.reshape(dec_batch, dec_seq, D_MODEL),
            k_p.reshape(kv_p), v_p.reshape(kv_p), c_p[None],
            k_s.reshape(kv_s), v_s.reshape(kv_s), c_s[None])
```

```python
import functools

import jax
import jax.numpy as jnp
from jax import lax
from jax.experimental import pallas as pl
from jax.experimental.pallas import tpu as pltpu

F32 = jnp.float32
BF16 = jnp.bfloat16

D_MODEL = 1024
CHUNK = 64
LEFT_CHUNKS = 8
ATT_WINDOW = LEFT_CHUNKS * CHUNK
BAND = ATT_WINDOW + CHUNK
N_HEADS = 8
HEAD_DIM = 64
D_ATT = N_HEADS * HEAD_DIM
D_CONV = D_MODEL // 2
CONV_WIDTH = 3
MAX_REL = 128
D_FF = 2816
D_PLE = 256
EPS = 1e-6
NEG_INF = -1e30

FF_CHUNK = 256
N_FF_CHUNKS = D_FF // FF_CHUNK
MIX_CHUNK = 256
PAIR = 2 * CHUNK
PAIR_KEYS = ATT_WINDOW + PAIR
ATT_LOOKAHEAD = 4
SCORE_SLOTS = ATT_LOOKAHEAD + 2
TOKEN_TILE = 512
FFN_TOKEN_TILE = 1024
FFN_ROW_GROUPS = 2
SAMPLE_SEQS_PER_STEP = 4
CONV_PAD = 8
VMEM_LIMIT_BYTES = 56 * 1024 * 1024


def _dot(a, b):
    return jnp.dot(a, b, preferred_element_type=F32)


def _dot_nt(a, b):
    return lax.dot_general(a, b, (((1,), (1,)), ((), ())), preferred_element_type=F32)


def _rmsnorm(x, g):
    return x * lax.rsqrt(jnp.mean(x * x, axis=-1, keepdims=True) + EPS) * g


def _sigmoid(x):
    return 1.0 / (1.0 + jnp.exp(-x))


def _resident(shape):
    return pl.BlockSpec(shape, lambda *_: (0,) * len(shape), pipeline_mode=pl.Buffered(1))


def _params(n_grid_dims):
    return pltpu.CompilerParams(
        dimension_semantics=("arbitrary",) * n_grid_dims,
        vmem_limit_bytes=VMEM_LIMIT_BYTES,
    )


def _swiglu_hidden(u, wg_ref, wu_ref, act_ref, chunks):
    for c in chunks:
        cs = slice(c * FF_CHUNK, (c + 1) * FF_CHUNK)
        gate = _dot(u, wg_ref[:, cs])
        up = _dot(u, wu_ref[:, cs])
        act_ref[:, cs] = (gate * _sigmoid(gate) * up).astype(BF16)


def _row_groups(n_rows):
    n = FFN_ROW_GROUPS if n_rows % FFN_TOKEN_TILE == 0 else 1
    return [slice(k * n_rows // n, (k + 1) * n_rows // n) for k in range(n)]


def _ffn_hidden_and_down(x_ref, g_pre, g_post, wg_ref, wu_ref, wd_ref, act_scr):
    groups = _row_groups(x_ref.shape[0])
    for r in groups:
        u = _rmsnorm(x_ref[r, :], g_pre).astype(BF16)
        _swiglu_hidden(u, wg_ref, wu_ref, act_scr.at[r, :], range(N_FF_CHUNKS))
    ys = [_dot(act_scr[r, :], wd_ref[...]) for r in groups]
    return groups, [x_ref[r, :] + 0.5 * _rmsnorm(y, g_post) for r, y in zip(groups, ys)]


def _ffn_kernel(x_ref, g_ref, wg_ref, wu_ref, wd_ref, o_ref, act_scr, *, pre, post):
    groups, hs = _ffn_hidden_and_down(x_ref, g_ref[pre:pre + 1, :], g_ref[post:post + 1, :],
                                      wg_ref, wu_ref, wd_ref, act_scr)
    for r, h in zip(groups, hs):
        o_ref[r, :] = h


def _ffn_ple_kernel(x_ref, p_ref, g_ref, wg_ref, wu_ref, wd_ref, wpg_ref, wpp_ref, o_ref, act_scr,
                    *, pre, post, ple):
    groups, hs = _ffn_hidden_and_down(x_ref, g_ref[pre:pre + 1, :], g_ref[post:post + 1, :],
                                      wg_ref, wu_ref, wd_ref, act_scr)
    gated = [_sigmoid(_dot(h.astype(BF16), wpg_ref[...]))
             * _dot(p_ref[r, :].astype(BF16), wpp_ref[...]) for r, h in zip(groups, hs)]
    for r, h, gp in zip(groups, hs, gated):
        o_ref[r, :] = h + _rmsnorm(gp, g_ref[ple:ple + 1, :])


def _ffn_call(x, norm_g, wg, wu, wd, pre, post):
    n = x.shape[0]
    tm = min(FFN_TOKEN_TILE, n)
    row = lambda i: (i, 0)
    return pl.pallas_call(
        functools.partial(_ffn_kernel, pre=pre, post=post),
        grid=(n // tm,),
        in_specs=[pl.BlockSpec((tm, D_MODEL), row), _resident(norm_g.shape),
                  _resident(wg.shape), _resident(wu.shape), _resident(wd.shape)],
        out_specs=pl.BlockSpec((tm, D_MODEL), row),
        out_shape=jax.ShapeDtypeStruct((n, D_MODEL), F32),
        scratch_shapes=[pltpu.VMEM((tm, D_FF), BF16)],
        compiler_params=_params(1),
        name="ffn",
    )(x, norm_g, wg, wu, wd)


def _ffn_ple_call(x, p, norm_g, wg, wu, wd, wpg, wpp, pre, post, ple):
    n = x.shape[0]
    tm = min(FFN_TOKEN_TILE, n)
    row = lambda i: (i, 0)
    return pl.pallas_call(
        functools.partial(_ffn_ple_kernel, pre=pre, post=post, ple=ple),
        grid=(n // tm,),
        in_specs=[pl.BlockSpec((tm, D_MODEL), row), pl.BlockSpec((tm, D_PLE), row),
                  _resident(norm_g.shape), _resident(wg.shape), _resident(wu.shape),
                  _resident(wd.shape), _resident(wpg.shape), _resident(wpp.shape)],
        out_specs=pl.BlockSpec((tm, D_MODEL), row),
        out_shape=jax.ShapeDtypeStruct((n, D_MODEL), F32),
        scratch_shapes=[pltpu.VMEM((tm, D_FF), BF16)],
        compiler_params=_params(1),
        name="ffn_ple",
    )(x, p, norm_g, wg, wu, wd, wpg, wpp)


def _build_bias(tab_ref, head, dist, lo, hi):
    def body(j, acc):
        v = lo + j
        return jnp.where(dist == v, tab_ref[head, v + MAX_REL], acc)
    return lax.fori_loop(0, hi - lo + 1, body, jnp.full(dist.shape, NEG_INF, F32))


def _short_conv(s, bgate, convw_ref, conv_scr):
    t = s.shape[0]
    conv_scr[CONV_PAD:CONV_PAD + t, :] = s
    y = (convw_ref[0:1, :] * conv_scr[CONV_PAD - 2:CONV_PAD - 2 + t, :]
         + convw_ref[1:2, :] * conv_scr[CONV_PAD - 1:CONV_PAD - 1 + t, :]
         + convw_ref[2:3, :] * s)
    return bgate * y


def _mix_out(h1, o_att, y_conv, gates, g_post, wao_ref, wco_ref, wo_ref):
    m = (_sigmoid(gates[:, :D_MODEL]) * _dot(o_att.astype(BF16), wao_ref[...])
         + _sigmoid(gates[:, D_MODEL:]) * _dot(y_conv.astype(BF16), wco_ref[...]))
    return h1 + _rmsnorm(_dot(m.astype(BF16), wo_ref[...]), g_post)


def _mixer_prompt_kernel(tab_ref, h1_ref, g_ref, wq_ref, wkt_ref, wv_ref, wxbc_ref, wg_ref,
                         convw_ref, wao_ref, wco_ref, wo_ref,
                         h2_ref, kout_ref, vout_ref, cout_ref,
                         q_scr, kt_scr, v_scr, bias_scr, s_scr, o_scr, conv_scr, yc_scr, u_scr, m_scr,
                         *, tm, tiles_per_seq):
    b = pl.program_id(0)
    i = pl.program_id(1)
    n_pairs = tm // PAIR

    @pl.when((b == 0) & (i == 0))
    def _():
        rows = 32
        for rb in range(PAIR // rows):
            r = lax.broadcasted_iota(jnp.int32, (rows, PAIR_KEYS), 0) + rb * rows
            c = lax.broadcasted_iota(jnp.int32, (rows, PAIR_KEYS), 1)
            n = c - (r & CHUNK)
            d = jnp.minimum((r & (CHUNK - 1)) + ATT_WINDOW - n, MAX_REL)
            d = jnp.where((n >= 0) & (n < BAND), d, MAX_REL + 1)

            def per_head(h, carry):
                bias_scr[h, rb * rows:(rb + 1) * rows, :] = _build_bias(
                    tab_ref, h, d, -(CHUNK - 1), MAX_REL)
                return carry
            lax.fori_loop(0, N_HEADS, per_head, 0)

    @pl.when(i == 0)
    def _():
        kt_scr[:, :, :ATT_WINDOW] = jnp.zeros((N_HEADS, HEAD_DIM, ATT_WINDOW), BF16)
        v_scr[:, :ATT_WINDOW, :] = jnp.zeros((N_HEADS, ATT_WINDOW, 2 * HEAD_DIM), BF16)
        conv_scr[0:CONV_PAD, :] = jnp.zeros((CONV_PAD, D_CONV), F32)

    u = _rmsnorm(h1_ref[...], g_ref[2:3, :]).astype(BF16)
    u_scr[...] = u

    for c in range(D_CONV // MIX_CHUNK):
        cs = slice(c * MIX_CHUNK, (c + 1) * MIX_CHUNK)
        x_in = _dot(u, wxbc_ref[:, cs])
        b_gate = _dot(u, wxbc_ref[:, D_CONV + c * MIX_CHUNK:D_CONV + (c + 1) * MIX_CHUNK])
        c_gate = _dot(u, wxbc_ref[:, 2 * D_CONV + c * MIX_CHUNK:2 * D_CONV + (c + 1) * MIX_CHUNK])
        s_conv = c_gate * x_in
        yc_scr[:, cs] = _short_conv(s_conv, b_gate, convw_ref.at[:, cs],
                                    conv_scr.at[:, cs]).astype(BF16)
    new_prefix = conv_scr[CONV_PAD + tm - 2:CONV_PAD + tm, :]
    conv_scr[CONV_PAD - 2:CONV_PAD, :] = new_prefix

    q = _dot(u, wq_ref[...]) * (HEAD_DIM ** -0.5)
    kt = _dot_nt(wkt_ref[...], u)
    v = _dot(u, wv_ref[...])
    ones = jnp.ones((tm, HEAD_DIM), F32)
    for h in range(N_HEADS):
        hs = slice(h * HEAD_DIM, (h + 1) * HEAD_DIM)
        q_scr[h] = q[:, hs].astype(BF16)
        kt_scr[h, :, ATT_WINDOW:] = kt[hs, :].astype(BF16)
        v_scr[h, ATT_WINDOW:, :] = jnp.concatenate([v[:, hs], ones], axis=-1).astype(BF16)

    @pl.when(i == tiles_per_seq - 1)
    def _():
        kout_ref[0] = kt[:, tm - ATT_WINDOW:]
        vout_ref[0] = v[tm - ATT_WINDOW:, :]
        cout_ref[0] = new_prefix

    def attend(first_tile):
        col = lax.broadcasted_iota(jnp.int32, (1, PAIR_KEYS), 1)

        def scores(h, jj, slot):
            k0 = jj * PAIR
            s = _dot(q_scr[h, k0:k0 + PAIR, :], kt_scr[h, :, k0:k0 + PAIR_KEYS])
            s = s + bias_scr[h]
            if first_tile:
                s = s + jnp.where(col + k0 < ATT_WINDOW, NEG_INF, 0.0)
            s_scr[slot] = s
            return jnp.max(s, axis=-1, keepdims=True)

        def weighted_values(h, jj, slot, m):
            k0 = jj * PAIR
            p = jnp.exp(s_scr[slot] - m).astype(BF16)
            o_scr[h, k0:k0 + PAIR, :] = _dot(p, v_scr[h, k0:k0 + PAIR_KEYS, :])

        steps = [(h, jj) for h in range(N_HEADS) for jj in range(n_pairs)]
        maxes = {}
        for k in range(len(steps) + ATT_LOOKAHEAD):
            if k < len(steps):
                maxes[k] = scores(*steps[k], k % SCORE_SLOTS)
            if k >= ATT_LOOKAHEAD:
                j = k - ATT_LOOKAHEAD
                weighted_values(*steps[j], j % SCORE_SLOTS, maxes.pop(j))

    @pl.when(i == 0)
    def _():
        attend(True)

    @pl.when(i > 0)
    def _():
        attend(False)

    u = u_scr[...]
    y_conv = yc_scr[...]
    low_half = lax.broadcasted_iota(jnp.int32, (tm, 2 * HEAD_DIM), 1) < HEAD_DIM
    pairs = []
    for j in range(N_HEADS // 2):
        even = o_scr[2 * j]
        odd = o_scr[2 * j + 1]
        pairs.append(jnp.where(low_half,
                               even * pltpu.roll(1.0 / even, HEAD_DIM, 1),
                               pltpu.roll(odd, HEAD_DIM, 1) * (1.0 / odd)))
    o_att = jnp.concatenate(pairs, axis=-1).astype(BF16)
    for c in range(D_MODEL // MIX_CHUNK):
        cs = slice(c * MIX_CHUNK, (c + 1) * MIX_CHUNK)
        g_conv = _dot(u, wg_ref[:, D_MODEL + c * MIX_CHUNK:D_MODEL + (c + 1) * MIX_CHUNK])
        m_conv = _sigmoid(g_conv) * _dot(y_conv, wco_ref[:, cs])
        g_att = _dot(u, wg_ref[:, cs])
        m_scr[:, cs] = (_sigmoid(g_att) * _dot(o_att, wao_ref[:, cs]) + m_conv).astype(BF16)
    kt_scr[:, :, :ATT_WINDOW] = kt_scr[:, :, tm:]
    v_scr[:, :ATT_WINDOW, :] = v_scr[:, tm:, :]
    halves = (slice(0, tm // 2), slice(tm // 2, tm))
    outs = [_dot(m_scr[r, :], wo_ref[...]) for r in halves]
    for r, mo in zip(halves, outs):
        h2_ref[r, :] = h1_ref[r, :] + _rmsnorm(mo, g_ref[3:4, :])


def _mixer_prompt_call(tab, h1, norm_g, wq, wkt, wv, wxbc, wg, convw, wao, wco, wo, batch, seq):
    tm = TOKEN_TILE
    tiles = seq // tm
    row = lambda b, i: (b * tiles + i, 0)
    per_seq = lambda b, i: (b, 0, 0)
    weights = (wq, wkt, wv, wxbc, wg, convw, wao, wco, wo)
    return pl.pallas_call(
        functools.partial(_mixer_prompt_kernel, tm=tm, tiles_per_seq=tiles),
        grid=(batch, tiles),
        in_specs=[pl.BlockSpec(memory_space=pltpu.SMEM),
                  pl.BlockSpec((tm, D_MODEL), row), _resident(norm_g.shape)]
                 + [_resident(w.shape) for w in weights],
        out_specs=[pl.BlockSpec((tm, D_MODEL), row),
                   pl.BlockSpec((1, D_ATT, ATT_WINDOW), per_seq),
                   pl.BlockSpec((1, ATT_WINDOW, D_ATT), per_seq),
                   pl.BlockSpec((1, CONV_WIDTH - 1, D_CONV), per_seq)],
        out_shape=[jax.ShapeDtypeStruct((batch * seq, D_MODEL), F32),
                   jax.ShapeDtypeStruct((batch, D_ATT, ATT_WINDOW), F32),
                   jax.ShapeDtypeStruct((batch, ATT_WINDOW, D_ATT), F32),
                   jax.ShapeDtypeStruct((batch, CONV_WIDTH - 1, D_CONV), F32)],
        scratch_shapes=[pltpu.VMEM((N_HEADS, tm, HEAD_DIM), BF16),
                        pltpu.VMEM((N_HEADS, HEAD_DIM, ATT_WINDOW + tm), BF16),
                        pltpu.VMEM((N_HEADS, ATT_WINDOW + tm, 2 * HEAD_DIM), BF16),
                        pltpu.VMEM((N_HEADS, PAIR, PAIR_KEYS), F32),
                        pltpu.VMEM((SCORE_SLOTS, PAIR, PAIR_KEYS), F32),
                        pltpu.VMEM((N_HEADS, tm, 2 * HEAD_DIM), F32),
                        pltpu.VMEM((CONV_PAD + tm, D_CONV), F32),
                        pltpu.VMEM((tm, D_CONV), BF16),
                        pltpu.VMEM((tm, D_MODEL), BF16),
                        pltpu.VMEM((tm, D_MODEL), BF16)],
        compiler_params=_params(2),
        name="mixer_prompt",
    )(tab, h1, norm_g, *weights)


def _inproj_kernel(h1_ref, g_ref, wq_ref, wk_ref, wv_ref, wxbc_ref, wg_ref,
                   q_ref, k_ref, v_ref, xbc_ref, gates_ref):
    u = _rmsnorm(h1_ref[...], g_ref[2:3, :]).astype(BF16)
    q_ref[...] = _dot(u, wq_ref[...]) * (HEAD_DIM ** -0.5)
    k_ref[...] = _dot(u, wk_ref[...])
    v_ref[...] = _dot(u, wv_ref[...])
    xbc_ref[...] = _dot(u, wxbc_ref[...])
    gates_ref[...] = _dot(u, wg_ref[...])


def _inproj_call(h1, norm_g, wq, wk, wv, wxbc, wg):
    n = h1.shape[0]
    ins = (h1, norm_g, wq, wk, wv, wxbc, wg)
    widths = (D_ATT, D_ATT, D_ATT, 3 * D_CONV, 2 * D_MODEL)
    return pl.pallas_call(
        _inproj_kernel,
        grid=(1,),
        in_specs=[_resident(a.shape) for a in ins],
        out_specs=[pl.BlockSpec((n, w), lambda i: (0, 0)) for w in widths],
        out_shape=[jax.ShapeDtypeStruct((n, w), F32) for w in widths],
        compiler_params=_params(1),
        name="inproj_sample",
    )(*ins)


def _attn_sample_kernel(tab_ref, q_ref, k_ref, v_ref, ck_ref, cv_ref, xbc_ref, cconv_ref, convw_ref,
                        o_ref, y_ref, cout_ref, bias_c_scr, bias_n_scr, conv_scr, *, t, lc):
    @pl.when(pl.program_id(0) == 0)
    def _():
        r = lax.broadcasted_iota(jnp.int32, (t, lc), 0)
        c = lax.broadcasted_iota(jnp.int32, (t, lc), 1)
        d_c = jnp.minimum(lc + r - c, MAX_REL)
        r = lax.broadcasted_iota(jnp.int32, (t, t), 0)
        c = lax.broadcasted_iota(jnp.int32, (t, t), 1)
        d_n = jnp.clip(r - c, -MAX_REL, MAX_REL)

        def per_head(h, carry):
            bias_c_scr[h] = _build_bias(tab_ref, h, d_c, 1, MAX_REL)
            bias_n_scr[h] = _build_bias(tab_ref, h, d_n, -min(t - 1, MAX_REL), min(t - 1, MAX_REL))
            return carry
        lax.fori_loop(0, N_HEADS, per_head, 0)

    rows = N_HEADS * t
    row_head = lax.broadcasted_iota(jnp.int32, (N_HEADS, t, D_ATT), 0).reshape(rows, D_ATT)
    col = lax.broadcasted_iota(jnp.int32, (rows, D_ATT), 1)
    own_head = (col >= row_head * HEAD_DIM) & (col < (row_head + 1) * HEAD_DIM)
    bias_c = bias_c_scr[...].reshape(rows, lc)
    bias_n = bias_n_scr[...].reshape(rows, t)
    conv_scr[0:CONV_PAD, :] = jnp.zeros((CONV_PAD, D_CONV), F32)

    for sq in range(q_ref.shape[0] // t):
        rs = slice(sq * t, (sq + 1) * t)
        q_heads = jnp.where(own_head, jnp.concatenate([q_ref[rs, :]] * N_HEADS, axis=0),
                            0.0).astype(BF16)
        k_new = k_ref[rs, :].astype(BF16)
        v_new = v_ref[rs, :].astype(BF16)
        s_c = _dot(q_heads, ck_ref[sq].astype(BF16)) + bias_c
        s_n = _dot_nt(q_heads, k_new) + bias_n
        m = jnp.maximum(jnp.max(s_c, axis=-1, keepdims=True),
                        jnp.max(s_n, axis=-1, keepdims=True))
        p_c = jnp.exp(s_c - m)
        p_n = jnp.exp(s_n - m)
        l = jnp.sum(p_c, axis=-1, keepdims=True) + jnp.sum(p_n, axis=-1, keepdims=True)
        o_all = (_dot_nt(p_c.astype(BF16), cv_ref[sq].astype(BF16))
                 + _dot(p_n.astype(BF16), v_new)) * (1.0 / l)
        o_all = jnp.where(own_head, o_all, 0.0)
        o = o_all[0:t]
        for h in range(1, N_HEADS):
            o = o + o_all[h * t:(h + 1) * t]
        o_ref[rs, :] = o

        xbc = xbc_ref[rs, :]
        s_conv = xbc[:, 2 * D_CONV:] * xbc[:, :D_CONV]
        conv_scr[CONV_PAD - 2:CONV_PAD, :] = cconv_ref[sq]
        y_ref[rs, :] = _short_conv(s_conv, xbc[:, D_CONV:2 * D_CONV], convw_ref, conv_scr)
        cout_ref[sq] = conv_scr[CONV_PAD + t - 2:CONV_PAD + t, :]


def _attn_sample_call(tab, q, k, v, cache_k, cache_v, xbc, cache_conv, convw, batch, t):
    lc = cache_k.shape[1]
    n_seq = SAMPLE_SEQS_PER_STEP if batch % SAMPLE_SEQS_PER_STEP == 0 else 1
    row = lambda b: (b, 0)
    per_seq = lambda b: (b, 0, 0)
    cache_k = cache_k.transpose(0, 2, 3, 1).reshape(batch, D_ATT, lc)
    cache_v = cache_v.transpose(0, 2, 3, 1).reshape(batch, D_ATT, lc)
    cache_spec = pl.BlockSpec((n_seq, D_ATT, lc), per_seq)
    return pl.pallas_call(
        functools.partial(_attn_sample_kernel, t=t, lc=lc),
        grid=(batch // n_seq,),
        in_specs=[pl.BlockSpec(memory_space=pltpu.SMEM),
                  pl.BlockSpec((n_seq * t, D_ATT), row), pl.BlockSpec((n_seq * t, D_ATT), row),
                  pl.BlockSpec((n_seq * t, D_ATT), row),
                  cache_spec, cache_spec,
                  pl.BlockSpec((n_seq * t, 3 * D_CONV), row),
                  pl.BlockSpec((n_seq, CONV_WIDTH - 1, D_CONV), per_seq),
                  _resident(convw.shape)],
        out_specs=[pl.BlockSpec((n_seq * t, D_ATT), row), pl.BlockSpec((n_seq * t, D_CONV), row),
                   pl.BlockSpec((n_seq, CONV_WIDTH - 1, D_CONV), per_seq)],
        out_shape=[jax.ShapeDtypeStruct((batch * t, D_ATT), F32),
                   jax.ShapeDtypeStruct((batch * t, D_CONV), F32),
                   jax.ShapeDtypeStruct((batch, CONV_WIDTH - 1, D_CONV), F32)],
        scratch_shapes=[pltpu.VMEM((N_HEADS, t, lc), F32),
                        pltpu.VMEM((N_HEADS, t, t), F32),
                        pltpu.VMEM((CONV_PAD + t, D_CONV), F32)],
        compiler_params=_params(1),
        name="attn_sample",
    )(tab, q, k, v, cache_k, cache_v, xbc, cache_conv, convw)


def _mix_kernel(h1_ref, o_ref, y_ref, gates_ref, g_ref, wao_ref, wco_ref, wo_ref, h2_ref):
    h2_ref[...] = _mix_out(h1_ref[...], o_ref[...], y_ref[...], gates_ref[...], g_ref[3:4, :],
                           wao_ref, wco_ref, wo_ref)


def _mix_call(h1, o_att, y_conv, gates, norm_g, wao, wco, wo):
    ins = (h1, o_att, y_conv, gates, norm_g, wao, wco, wo)
    return pl.pallas_call(
        _mix_kernel,
        grid=(1,),
        in_specs=[_resident(a.shape) for a in ins],
        out_specs=pl.BlockSpec(h1.shape, lambda i: (0, 0)),
        out_shape=jax.ShapeDtypeStruct(h1.shape, F32),
        compiler_params=_params(1),
        name="mix_sample",
    )(*ins)


def kernel(x_prompt, x_sample, cache_k, cache_v, cache_conv, p_prompt, p_sample, norm_g,
           w1_gate, w1_up, w1_down, w_in, conv_w, rel_bias, w_att_out, w_conv_out, w_out,
           w2_gate, w2_up, w2_down, w_ple_gate, w_ple_proj):
    depth = norm_g.shape[0]
    assert depth == 1, "one layer per step"
    batch, seq, _ = x_prompt.shape
    dec_batch, dec_seq, _ = x_sample.shape
    assert seq % TOKEN_TILE == 0 and TOKEN_TILE % PAIR == 0 and TOKEN_TILE >= ATT_WINDOW
    l = 0

    g = norm_g[l]
    w1g = w1_gate[l].astype(BF16)
    w1u = w1_up[l].astype(BF16)
    w2g = w2_gate[l].astype(BF16)
    w2u = w2_up[l].astype(BF16)
    w1d = w1_down[l].astype(BF16)
    w2d = w2_down[l].astype(BF16)
    win = w_in[l].astype(BF16)
    wq = win[:, :D_ATT]
    wk = win[:, D_ATT:2 * D_ATT]
    wv = win[:, 2 * D_ATT:3 * D_ATT]
    wxbc = win[:, 3 * D_ATT:3 * D_ATT + 3 * D_CONV]
    wg = win[:, 3 * D_ATT + 3 * D_CONV:]
    wkt = wk.T
    wao = w_att_out[l].astype(BF16)
    wco = w_conv_out[l].astype(BF16)
    wo = w_out[l].astype(BF16)
    wpg = w_ple_gate[l].astype(BF16)
    wpp = w_ple_proj[l].astype(BF16)
    tab = rel_bias[l]
    convw = conv_w[l]

    xp = x_prompt.reshape(batch * seq, D_MODEL)
    h1p = _ffn_call(xp, g, w1g, w1u, w1d, 0, 1)
    h2p, kt_p, v_p, c_p = _mixer_prompt_call(tab, h1p, g, wq, wkt, wv, wxbc, wg, convw,
                                            wao, wco, wo, batch, seq)
    y_p = _ffn_ple_call(h2p, p_prompt[l].reshape(batch * seq, D_PLE), g, w2g, w2u, w2d, wpg, wpp,
                        4, 5, 6)

    lc = cache_k.shape[2]
    xs = x_sample.reshape(dec_batch * dec_seq, D_MODEL)
    h1s = _ffn_call(xs, g, w1g, w1u, w1d, 0, 1)
    q_s, k_s, v_s, xbc_s, gates_s = _inproj_call(h1s, g, wq, wk, wv, wxbc, wg)
    o_s, yc_s, c_s = _attn_sample_call(tab, q_s, k_s, v_s,
                                       cache_k[l], cache_v[l],
                                       xbc_s, cache_conv[l], convw, dec_batch, dec_seq)
    h2s = _mix_call(h1s, o_s, yc_s, gates_s, g, wao, wco, wo)
    y_s = _ffn_ple_call(h2s, p_sample[l].reshape(dec_batch * dec_seq, D_PLE), g, w2g, w2u, w2d,
                        wpg, wpp, 4, 5, 6)

    kv_p = (1, batch, ATT_WINDOW, N_HEADS, HEAD_DIM)
    kv_s = (1, dec_batch, dec_seq, N_HEADS, HEAD_DIM)
    k_p = kt_p.reshape(batch, N_HEADS, HEAD_DIM, ATT_WINDOW).transpose(0, 3, 1, 2)
    return (y_p.reshape(batch, seq, D_MODEL), y_s.reshape(dec_batch, dec_seq, D_MODEL),
            k_p.reshape(kv_p), v_p.reshape(kv_p), c_p[None],
            k_s.reshape(kv_s), v_s.reshape(kv_s), c_s[None])
```

```python
import functools

import jax
import jax.numpy as jnp
from jax import lax
from jax.experimental import pallas as pl
from jax.experimental.pallas import tpu as pltpu

F32 = jnp.float32
BF16 = jnp.bfloat16

D_MODEL = 1024
CHUNK = 64
LEFT_CHUNKS = 8
ATT_WINDOW = LEFT_CHUNKS * CHUNK
BAND = ATT_WINDOW + CHUNK
N_HEADS = 8
HEAD_DIM = 64
D_ATT = N_HEADS * HEAD_DIM
D_CONV = D_MODEL // 2
CONV_WIDTH = 3
MAX_REL = 128
D_FF = 2816
D_PLE = 256
EPS = 1e-6
NEG_INF = -1e30

FF_CHUNK = 256
N_FF_CHUNKS = D_FF // FF_CHUNK
MIX_CHUNK = 256
PAIR = 2 * CHUNK
PAIR_KEYS = ATT_WINDOW + PAIR
ATT_LOOKAHEAD = 4
SCORE_SLOTS = ATT_LOOKAHEAD + 2
TOKEN_TILE = 512
FFN_TOKEN_TILE = 1024
FFN_ROW_GROUPS = 2
SAMPLE_SEQS_PER_STEP = 8
CONV_PAD = 8
VMEM_LIMIT_BYTES = 56 * 1024 * 1024


def _dot(a, b):
    return jnp.dot(a, b, preferred_element_type=F32)


def _dot_nt(a, b):
    return lax.dot_general(a, b, (((1,), (1,)), ((), ())), preferred_element_type=F32)


def _rmsnorm(x, g):
    return x * lax.rsqrt(jnp.mean(x * x, axis=-1, keepdims=True) + EPS) * g


def _sigmoid(x):
    return 1.0 / (1.0 + jnp.exp(-x))


def _resident(shape):
    return pl.BlockSpec(shape, lambda *_: (0,) * len(shape), pipeline_mode=pl.Buffered(1))


def _params(n_grid_dims):
    return pltpu.CompilerParams(
        dimension_semantics=("arbitrary",) * n_grid_dims,
        vmem_limit_bytes=VMEM_LIMIT_BYTES,
    )


def _swiglu_hidden(u, wg_ref, wu_ref, act_ref, chunks):
    for c in chunks:
        cs = slice(c * FF_CHUNK, (c + 1) * FF_CHUNK)
        gate = _dot(u, wg_ref[:, cs])
        up = _dot(u, wu_ref[:, cs])
        act_ref[:, cs] = (gate * _sigmoid(gate) * up).astype(BF16)


def _row_groups(n_rows):
    n = FFN_ROW_GROUPS if n_rows % FFN_TOKEN_TILE == 0 else 1
    return [slice(k * n_rows // n, (k + 1) * n_rows // n) for k in range(n)]


def _ffn_hidden_and_down(x_ref, g_pre, g_post, wg_ref, wu_ref, wd_ref, act_scr):
    groups = _row_groups(x_ref.shape[0])
    for r in groups:
        u = _rmsnorm(x_ref[r, :], g_pre).astype(BF16)
        _swiglu_hidden(u, wg_ref, wu_ref, act_scr.at[r, :], range(N_FF_CHUNKS))
    ys = [_dot(act_scr[r, :], wd_ref[...]) for r in groups]
    half_g = 0.5 * g_post
    return groups, [x_ref[r, :] + _rmsnorm(y, half_g) for r, y in zip(groups, ys)]


def _ffn_kernel(x_ref, g_ref, wg_ref, wu_ref, wd_ref, o_ref, act_scr, *, pre, post):
    groups, hs = _ffn_hidden_and_down(x_ref, g_ref[pre:pre + 1, :], g_ref[post:post + 1, :],
                                      wg_ref, wu_ref, wd_ref, act_scr)
    for r, h in zip(groups, hs):
        o_ref[r, :] = h


def _ffn_ple_kernel(x_ref, p_ref, g_ref, wg_ref, wu_ref, wd_ref, wpg_ref, wpp_ref, o_ref, act_scr,
                    *, pre, post, ple):
    groups, hs = _ffn_hidden_and_down(x_ref, g_ref[pre:pre + 1, :], g_ref[post:post + 1, :],
                                      wg_ref, wu_ref, wd_ref, act_scr)
    gated = [_sigmoid(_dot(h.astype(BF16), wpg_ref[...]))
             * _dot(p_ref[r, :].astype(BF16), wpp_ref[...]) for r, h in zip(groups, hs)]
    for r, h, gp in zip(groups, hs, gated):
        o_ref[r, :] = h + _rmsnorm(gp, g_ref[ple:ple + 1, :])


def _ffn_call(x, norm_g, wg, wu, wd, pre, post):
    n = x.shape[0]
    tm = min(FFN_TOKEN_TILE, n)
    row = lambda i: (i, 0)
    return pl.pallas_call(
        functools.partial(_ffn_kernel, pre=pre, post=post),
        grid=(n // tm,),
        in_specs=[pl.BlockSpec((tm, D_MODEL), row), _resident(norm_g.shape),
                  _resident(wg.shape), _resident(wu.shape), _resident(wd.shape)],
        out_specs=pl.BlockSpec((tm, D_MODEL), row),
        out_shape=jax.ShapeDtypeStruct((n, D_MODEL), F32),
        scratch_shapes=[pltpu.VMEM((tm, D_FF), BF16)],
        compiler_params=_params(1),
        name="ffn",
    )(x, norm_g, wg, wu, wd)


def _ffn_ple_call(x, p, norm_g, wg, wu, wd, wpg, wpp, pre, post, ple):
    n = x.shape[0]
    tm = min(FFN_TOKEN_TILE, n)
    row = lambda i: (i, 0)
    return pl.pallas_call(
        functools.partial(_ffn_ple_kernel, pre=pre, post=post, ple=ple),
        grid=(n // tm,),
        in_specs=[pl.BlockSpec((tm, D_MODEL), row), pl.BlockSpec((tm, D_PLE), row),
                  _resident(norm_g.shape), _resident(wg.shape), _resident(wu.shape),
                  _resident(wd.shape), _resident(wpg.shape), _resident(wpp.shape)],
        out_specs=pl.BlockSpec((tm, D_MODEL), row),
        out_shape=jax.ShapeDtypeStruct((n, D_MODEL), F32),
        scratch_shapes=[pltpu.VMEM((tm, D_FF), BF16)],
        compiler_params=_params(1),
        name="ffn_ple",
    )(x, p, norm_g, wg, wu, wd, wpg, wpp)


def _build_bias(tab_ref, head, dist, lo, hi):
    def body(j, acc):
        v = lo + j
        return jnp.where(dist == v, tab_ref[head, v + MAX_REL], acc)
    return lax.fori_loop(0, hi - lo + 1, body, jnp.full(dist.shape, NEG_INF, F32))


def _short_conv(s, bgate, convw_ref, conv_scr):
    t = s.shape[0]
    conv_scr[CONV_PAD:CONV_PAD + t, :] = s
    y = (convw_ref[0:1, :] * conv_scr[CONV_PAD - 2:CONV_PAD - 2 + t, :]
         + convw_ref[1:2, :] * conv_scr[CONV_PAD - 1:CONV_PAD - 1 + t, :]
         + convw_ref[2:3, :] * s)
    return bgate * y


def _mix_out(h1, o_att, y_conv, gates, g_post, wao_ref, wco_ref, wo_ref):
    m = (_sigmoid(gates[:, :D_MODEL]) * _dot(o_att.astype(BF16), wao_ref[...])
         + _sigmoid(gates[:, D_MODEL:]) * _dot(y_conv.astype(BF16), wco_ref[...]))
    return h1 + _rmsnorm(_dot(m.astype(BF16), wo_ref[...]), g_post)


def _mixer_prompt_kernel(tab_ref, h1_ref, g_ref, wq_ref, wkt_ref, wv_ref, wxbc_ref, wg_ref,
                         convw_ref, wao_ref, wco_ref, wo_ref,
                         h2_ref, kout_ref, vout_ref, cout_ref,
                         q_scr, kt_scr, v_scr, bias_scr, s_scr, o_scr, conv_scr, yc_scr, u_scr, m_scr,
                         *, tm, tiles_per_seq):
    b = pl.program_id(0)
    i = pl.program_id(1)
    n_pairs = tm // PAIR

    @pl.when((b == 0) & (i == 0))
    def _():
        rows = 32
        for rb in range(PAIR // rows):
            r = lax.broadcasted_iota(jnp.int32, (rows, PAIR_KEYS), 0) + rb * rows
            c = lax.broadcasted_iota(jnp.int32, (rows, PAIR_KEYS), 1)
            n = c - (r & CHUNK)
            d = jnp.minimum((r & (CHUNK - 1)) + ATT_WINDOW - n, MAX_REL)
            d = jnp.where((n >= 0) & (n < BAND), d, MAX_REL + 1)

            def per_head(h, carry):
                bias_scr[h, rb * rows:(rb + 1) * rows, :] = _build_bias(
                    tab_ref, h, d, -(CHUNK - 1), MAX_REL)
                return carry
            lax.fori_loop(0, N_HEADS, per_head, 0)

    @pl.when(i == 0)
    def _():
        kt_scr[:, :, :ATT_WINDOW] = jnp.zeros((N_HEADS, HEAD_DIM, ATT_WINDOW), BF16)
        v_scr[:, :ATT_WINDOW, :] = jnp.zeros((N_HEADS, ATT_WINDOW, 2 * HEAD_DIM), BF16)
        conv_scr[0:CONV_PAD, :] = jnp.zeros((CONV_PAD, D_CONV), F32)

    u = _rmsnorm(h1_ref[...], g_ref[2:3, :]).astype(BF16)
    u_scr[...] = u

    for c in range(D_CONV // MIX_CHUNK):
        cs = slice(c * MIX_CHUNK, (c + 1) * MIX_CHUNK)
        x_in = _dot(u, wxbc_ref[:, cs])
        b_gate = _dot(u, wxbc_ref[:, D_CONV + c * MIX_CHUNK:D_CONV + (c + 1) * MIX_CHUNK])
        c_gate = _dot(u, wxbc_ref[:, 2 * D_CONV + c * MIX_CHUNK:2 * D_CONV + (c + 1) * MIX_CHUNK])
        s_conv = c_gate * x_in
        yc_scr[:, cs] = _short_conv(s_conv, b_gate, convw_ref.at[:, cs],
                                    conv_scr.at[:, cs]).astype(BF16)
    new_prefix = conv_scr[CONV_PAD + tm - 2:CONV_PAD + tm, :]
    conv_scr[CONV_PAD - 2:CONV_PAD, :] = new_prefix

    q = _dot(u, wq_ref[...]) * (HEAD_DIM ** -0.5)
    kt = _dot_nt(wkt_ref[...], u)
    v = _dot(u, wv_ref[...])
    ones = jnp.ones((tm, HEAD_DIM), F32)
    for h in range(N_HEADS):
        hs = slice(h * HEAD_DIM, (h + 1) * HEAD_DIM)
        q_scr[h] = q[:, hs].astype(BF16)
        kt_scr[h, :, ATT_WINDOW:] = kt[hs, :].astype(BF16)
        v_scr[h, ATT_WINDOW:, :] = jnp.concatenate([v[:, hs], ones], axis=-1).astype(BF16)

    @pl.when(i == tiles_per_seq - 1)
    def _():
        kout_ref[0] = kt[:, tm - ATT_WINDOW:]
        vout_ref[0] = v[tm - ATT_WINDOW:, :]
        cout_ref[0] = new_prefix

    def attend(first_tile):
        col = lax.broadcasted_iota(jnp.int32, (1, PAIR_KEYS), 1)

        def scores(h, jj, slot):
            k0 = jj * PAIR
            s = _dot(q_scr[h, k0:k0 + PAIR, :], kt_scr[h, :, k0:k0 + PAIR_KEYS])
            s = s + bias_scr[h]
            if first_tile:
                s = s + jnp.where(col + k0 < ATT_WINDOW, NEG_INF, 0.0)
            s_scr[slot] = s
            return jnp.max(s, axis=-1, keepdims=True)

        def weighted_values(h, jj, slot, m):
            k0 = jj * PAIR
            p = jnp.exp(s_scr[slot] - m).astype(BF16)
            o_scr[h, k0:k0 + PAIR, :] = _dot(p, v_scr[h, k0:k0 + PAIR_KEYS, :])

        steps = [(h, jj) for h in range(N_HEADS) for jj in range(n_pairs)]
        maxes = {}
        for k in range(len(steps) + ATT_LOOKAHEAD):
            if k < len(steps):
                maxes[k] = scores(*steps[k], k % SCORE_SLOTS)
            if k >= ATT_LOOKAHEAD:
                j = k - ATT_LOOKAHEAD
                weighted_values(*steps[j], j % SCORE_SLOTS, maxes.pop(j))

    @pl.when(i == 0)
    def _():
        attend(True)

    @pl.when(i > 0)
    def _():
        attend(False)

    u = u_scr[...]
    y_conv = yc_scr[...]
    low_half = lax.broadcasted_iota(jnp.int32, (tm, 2 * HEAD_DIM), 1) < HEAD_DIM
    pairs = []
    for j in range(N_HEADS // 2):
        even = o_scr[2 * j]
        odd = o_scr[2 * j + 1]
        pairs.append(jnp.where(low_half,
                               even * pltpu.roll(1.0 / even, HEAD_DIM, 1),
                               pltpu.roll(odd, HEAD_DIM, 1) * (1.0 / odd)))
    o_att = jnp.concatenate(pairs, axis=-1).astype(BF16)
    for c in range(D_MODEL // MIX_CHUNK):
        cs = slice(c * MIX_CHUNK, (c + 1) * MIX_CHUNK)
        g_conv = _dot(u, wg_ref[:, D_MODEL + c * MIX_CHUNK:D_MODEL + (c + 1) * MIX_CHUNK])
        m_conv = _sigmoid(g_conv) * _dot(y_conv, wco_ref[:, cs])
        g_att = _dot(u, wg_ref[:, cs])
        m_scr[:, cs] = (_sigmoid(g_att) * _dot(o_att, wao_ref[:, cs]) + m_conv).astype(BF16)
    kt_scr[:, :, :ATT_WINDOW] = kt_scr[:, :, tm:]
    v_scr[:, :ATT_WINDOW, :] = v_scr[:, tm:, :]
    halves = (slice(0, tm // 2), slice(tm // 2, tm))
    outs = [_dot(m_scr[r, :], wo_ref[...]) for r in halves]
    for r, mo in zip(halves, outs):
        h2_ref[r, :] = h1_ref[r, :] + _rmsnorm(mo, g_ref[3:4, :])


def _mixer_prompt_call(tab, h1, norm_g, wq, wkt, wv, wxbc, wg, convw, wao, wco, wo, batch, seq):
    tm = TOKEN_TILE
    tiles = seq // tm
    row = lambda b, i: (b * tiles + i, 0)
    per_seq = lambda b, i: (b, 0, 0)
    weights = (wq, wkt, wv, wxbc, wg, convw, wao, wco, wo)
    return pl.pallas_call(
        functools.partial(_mixer_prompt_kernel, tm=tm, tiles_per_seq=tiles),
        grid=(batch, tiles),
        in_specs=[pl.BlockSpec(memory_space=pltpu.SMEM),
                  pl.BlockSpec((tm, D_MODEL), row), _resident(norm_g.shape)]
                 + [_resident(w.shape) for w in weights],
        out_specs=[pl.BlockSpec((tm, D_MODEL), row),
                   pl.BlockSpec((1, D_ATT, ATT_WINDOW), per_seq),
                   pl.BlockSpec((1, ATT_WINDOW, D_ATT), per_seq),
                   pl.BlockSpec((1, CONV_WIDTH - 1, D_CONV), per_seq)],
        out_shape=[jax.ShapeDtypeStruct((batch * seq, D_MODEL), F32),
                   jax.ShapeDtypeStruct((batch, D_ATT, ATT_WINDOW), F32),
                   jax.ShapeDtypeStruct((batch, ATT_WINDOW, D_ATT), F32),
                   jax.ShapeDtypeStruct((batch, CONV_WIDTH - 1, D_CONV), F32)],
        scratch_shapes=[pltpu.VMEM((N_HEADS, tm, HEAD_DIM), BF16),
                        pltpu.VMEM((N_HEADS, HEAD_DIM, ATT_WINDOW + tm), BF16),
                        pltpu.VMEM((N_HEADS, ATT_WINDOW + tm, 2 * HEAD_DIM), BF16),
                        pltpu.VMEM((N_HEADS, PAIR, PAIR_KEYS), F32),
                        pltpu.VMEM((SCORE_SLOTS, PAIR, PAIR_KEYS), F32),
                        pltpu.VMEM((N_HEADS, tm, 2 * HEAD_DIM), F32),
                        pltpu.VMEM((CONV_PAD + tm, D_CONV), F32),
                        pltpu.VMEM((tm, D_CONV), BF16),
                        pltpu.VMEM((tm, D_MODEL), BF16),
                        pltpu.VMEM((tm, D_MODEL), BF16)],
        compiler_params=_params(2),
        name="mixer_prompt",
    )(tab, h1, norm_g, *weights)


def _inproj_kernel(h1_ref, g_ref, wq_ref, wk_ref, wv_ref, wxbc_ref, wg_ref,
                   q_ref, k_ref, v_ref, xbc_ref, gates_ref):
    u = _rmsnorm(h1_ref[...], g_ref[2:3, :]).astype(BF16)
    q_ref[...] = _dot(u, wq_ref[...]) * (HEAD_DIM ** -0.5)
    k_ref[...] = _dot(u, wk_ref[...])
    v_ref[...] = _dot(u, wv_ref[...])
    xbc_ref[...] = _dot(u, wxbc_ref[...])
    gates_ref[...] = _dot(u, wg_ref[...])


def _inproj_call(h1, norm_g, wq, wk, wv, wxbc, wg):
    n = h1.shape[0]
    ins = (h1, norm_g, wq, wk, wv, wxbc, wg)
    widths = (D_ATT, D_ATT, D_ATT, 3 * D_CONV, 2 * D_MODEL)
    return pl.pallas_call(
        _inproj_kernel,
        grid=(1,),
        in_specs=[_resident(a.shape) for a in ins],
        out_specs=[pl.BlockSpec((n, w), lambda i: (0, 0)) for w in widths],
        out_shape=[jax.ShapeDtypeStruct((n, w), F32) for w in widths],
        compiler_params=_params(1),
        name="inproj_sample",
    )(*ins)


def _attn_sample_kernel(tab_ref, q_ref, k_ref, v_ref, ck_ref, cv_ref, xbc_ref, cconv_ref, convw_ref,
                        o_ref, y_ref, cout_ref, bias_c_scr, bias_n_scr, conv_scr, *, t, lc):
    @pl.when(pl.program_id(0) == 0)
    def _():
        r = lax.broadcasted_iota(jnp.int32, (t, lc), 0)
        c = lax.broadcasted_iota(jnp.int32, (t, lc), 1)
        d_c = jnp.minimum(lc + r - c, MAX_REL)
        r = lax.broadcasted_iota(jnp.int32, (t, t), 0)
        c = lax.broadcasted_iota(jnp.int32, (t, t), 1)
        d_n = jnp.clip(r - c, -MAX_REL, MAX_REL)

        def per_head(h, carry):
            bias_c_scr[h] = _build_bias(tab_ref, h, d_c, 1, MAX_REL)
            bias_n_scr[h] = _build_bias(tab_ref, h, d_n, -min(t - 1, MAX_REL), min(t - 1, MAX_REL))
            return carry
        lax.fori_loop(0, N_HEADS, per_head, 0)

    rows = N_HEADS * t
    row_head = lax.broadcasted_iota(jnp.int32, (N_HEADS, t, D_ATT), 0).reshape(rows, D_ATT)
    col = lax.broadcasted_iota(jnp.int32, (rows, D_ATT), 1)
    own_head = (col >= row_head * HEAD_DIM) & (col < (row_head + 1) * HEAD_DIM)
    bias_c = bias_c_scr[...].reshape(rows, lc)
    bias_n = bias_n_scr[...].reshape(rows, t)
    conv_scr[0:CONV_PAD, :] = jnp.zeros((CONV_PAD, D_CONV), F32)

    for sq in range(q_ref.shape[0] // t):
        rs = slice(sq * t, (sq + 1) * t)
        q_heads = jnp.where(own_head, jnp.concatenate([q_ref[rs, :]] * N_HEADS, axis=0),
                            0.0).astype(BF16)
        k_new = k_ref[rs, :].astype(BF16)
        v_new = v_ref[rs, :].astype(BF16)
        s_c = _dot(q_heads, ck_ref[sq].astype(BF16)) + bias_c
        s_n = _dot_nt(q_heads, k_new) + bias_n
        m = jnp.maximum(jnp.max(s_c, axis=-1, keepdims=True),
                        jnp.max(s_n, axis=-1, keepdims=True))
        p_c = jnp.exp(s_c - m)
        p_n = jnp.exp(s_n - m)
        l = jnp.sum(p_c, axis=-1, keepdims=True) + jnp.sum(p_n, axis=-1, keepdims=True)
        o_all = (_dot_nt(p_c.astype(BF16), cv_ref[sq].astype(BF16))
                 + _dot(p_n.astype(BF16), v_new)) * (1.0 / l)
        o_all = jnp.where(own_head, o_all, 0.0)
        o = o_all[0:t]
        for h in range(1, N_HEADS):
            o = o + o_all[h * t:(h + 1) * t]
        o_ref[rs, :] = o

        xbc = xbc_ref[rs, :]
        s_conv = xbc[:, 2 * D_CONV:] * xbc[:, :D_CONV]
        conv_scr[CONV_PAD - 2:CONV_PAD, :] = cconv_ref[sq]
        y_ref[rs, :] = _short_conv(s_conv, xbc[:, D_CONV:2 * D_CONV], convw_ref, conv_scr)
        cout_ref[sq] = conv_scr[CONV_PAD + t - 2:CONV_PAD + t, :]


def _attn_sample_call(tab, q, k, v, cache_k, cache_v, xbc, cache_conv, convw, batch, t):
    lc = cache_k.shape[1]
    n_seq = SAMPLE_SEQS_PER_STEP if batch % SAMPLE_SEQS_PER_STEP == 0 else 1
    row = lambda b: (b, 0)
    per_seq = lambda b: (b, 0, 0)
    cache_k = cache_k.transpose(0, 2, 3, 1).reshape(batch, D_ATT, lc)
    cache_v = cache_v.transpose(0, 2, 3, 1).reshape(batch, D_ATT, lc)
    cache_spec = pl.BlockSpec((n_seq, D_ATT, lc), per_seq)
    return pl.pallas_call(
        functools.partial(_attn_sample_kernel, t=t, lc=lc),
        grid=(batch // n_seq,),
        in_specs=[pl.BlockSpec(memory_space=pltpu.SMEM),
                  pl.BlockSpec((n_seq * t, D_ATT), row), pl.BlockSpec((n_seq * t, D_ATT), row),
                  pl.BlockSpec((n_seq * t, D_ATT), row),
                  cache_spec, cache_spec,
                  pl.BlockSpec((n_seq * t, 3 * D_CONV), row),
                  pl.BlockSpec((n_seq, CONV_WIDTH - 1, D_CONV), per_seq),
                  _resident(convw.shape)],
        out_specs=[pl.BlockSpec((n_seq * t, D_ATT), row), pl.BlockSpec((n_seq * t, D_CONV), row),
                   pl.BlockSpec((n_seq, CONV_WIDTH - 1, D_CONV), per_seq)],
        out_shape=[jax.ShapeDtypeStruct((batch * t, D_ATT), F32),
                   jax.ShapeDtypeStruct((batch * t, D_CONV), F32),
                   jax.ShapeDtypeStruct((batch, CONV_WIDTH - 1, D_CONV), F32)],
        scratch_shapes=[pltpu.VMEM((N_HEADS, t, lc), F32),
                        pltpu.VMEM((N_HEADS, t, t), F32),
                        pltpu.VMEM((CONV_PAD + t, D_CONV), F32)],
        compiler_params=_params(1),
        name="attn_sample",
    )(tab, q, k, v, cache_k, cache_v, xbc, cache_conv, convw)


def _mix_kernel(h1_ref, o_ref, y_ref, gates_ref, g_ref, wao_ref, wco_ref, wo_ref, h2_ref):
    h2_ref[...] = _mix_out(h1_ref[...], o_ref[...], y_ref[...], gates_ref[...], g_ref[3:4, :],
                           wao_ref, wco_ref, wo_ref)


def _mix_call(h1, o_att, y_conv, gates, norm_g, wao, wco, wo):
    ins = (h1, o_att, y_conv, gates, norm_g, wao, wco, wo)
    return pl.pallas_call(
        _mix_kernel,
        grid=(1,),
        in_specs=[_resident(a.shape) for a in ins],
        out_specs=pl.BlockSpec(h1.shape, lambda i: (0, 0)),
        out_shape=jax.ShapeDtypeStruct(h1.shape, F32),
        compiler_params=_params(1),
        name="mix_sample",
    )(*ins)


def kernel(x_prompt, x_sample, cache_k, cache_v, cache_conv, p_prompt, p_sample, norm_g,
           w1_gate, w1_up, w1_down, w_in, conv_w, rel_bias, w_att_out, w_conv_out, w_out,
           w2_gate, w2_up, w2_down, w_ple_gate, w_ple_proj):
    depth = norm_g.shape[0]
    assert depth == 1, "one layer per step"
    batch, seq, _ = x_prompt.shape
    dec_batch, dec_seq, _ = x_sample.shape
    assert seq % TOKEN_TILE == 0 and TOKEN_TILE % PAIR == 0 and TOKEN_TILE >= ATT_WINDOW
    l = 0

    g = norm_g[l]
    w1g = w1_gate[l].astype(BF16)
    w1u = w1_up[l].astype(BF16)
    w2g = w2_gate[l].astype(BF16)
    w2u = w2_up[l].astype(BF16)
    w1d = w1_down[l].astype(BF16)
    w2d = w2_down[l].astype(BF16)
    win = w_in[l].astype(BF16)
    wq = win[:, :D_ATT]
    wk = win[:, D_ATT:2 * D_ATT]
    wv = win[:, 2 * D_ATT:3 * D_ATT]
    wxbc = win[:, 3 * D_ATT:3 * D_ATT + 3 * D_CONV]
    wg = win[:, 3 * D_ATT + 3 * D_CONV:]
    wkt = wk.T
    wao = w_att_out[l].astype(BF16)
    wco = w_conv_out[l].astype(BF16)
    wo = w_out[l].astype(BF16)
    wpg = w_ple_gate[l].astype(BF16)
    wpp = w_ple_proj[l].astype(BF16)
    tab = rel_bias[l]
    convw = conv_w[l]

    xp = x_prompt.reshape(batch * seq, D_MODEL)
    h1p = _ffn_call(xp, g, w1g, w1u, w1d, 0, 1)
    h2p, kt_p, v_p, c_p = _mixer_prompt_call(tab, h1p, g, wq, wkt, wv, wxbc, wg, convw,
                                            wao, wco, wo, batch, seq)
    y_p = _ffn_ple_call(h2p, p_prompt[l].reshape(batch * seq, D_PLE), g, w2g, w2u, w2d, wpg, wpp,
                        4, 5, 6)

    lc = cache_k.shape[2]
    xs = x_sample.reshape(dec_batch * dec_seq, D_MODEL)
    h1s = _ffn_call(xs, g, w1g, w1u, w1d, 0, 1)
    q_s, k_s, v_s, xbc_s, gates_s = _inproj_call(h1s, g, wq, wk, wv, wxbc, wg)
    o_s, yc_s, c_s = _attn_sample_call(tab, q_s, k_s, v_s,
                                       cache_k[l], cache_v[l],
                                       xbc_s, cache_conv[l], convw, dec_batch, dec_seq)
    h2s = _mix_call(h1s, o_s, yc_s, gates_s, g, wao, wco, wo)
    y_s = _ffn_ple_call(h2s, p_sample[l].reshape(dec_batch * dec_seq, D_PLE), g, w2g, w2u, w2d,
                        wpg, wpp, 4, 5, 6)

    kv_p = (1, batch, ATT_WINDOW, N_HEADS, HEAD_DIM)
    kv_s = (1, dec_batch, dec_seq, N_HEADS, HEAD_DIM)
    k_p = kt_p.reshape(batch, N_HEADS, HEAD_DIM, ATT_WINDOW).transpose(0, 3, 1, 2)
    return (y_p.reshape(batch, seq, D_MODEL), y_s.reshape(dec_batch, dec_seq, D_MODEL),
            k_p.reshape(kv_p), v_p.reshape(kv_p), c_p[None],
            k_s.reshape(kv_s), v_s.reshape(kv_s), c_s[None])
```

```python
import functools

import jax
import jax.numpy as jnp
from jax import lax
from jax.experimental import pallas as pl
from jax.experimental.pallas import tpu as pltpu

F32 = jnp.float32
BF16 = jnp.bfloat16

D_MODEL = 1024
CHUNK = 64
LEFT_CHUNKS = 8
ATT_WINDOW = LEFT_CHUNKS * CHUNK
BAND = ATT_WINDOW + CHUNK
N_HEADS = 8
HEAD_DIM = 64
D_ATT = N_HEADS * HEAD_DIM
D_CONV = D_MODEL // 2
CONV_WIDTH = 3
MAX_REL = 128
D_FF = 2816
D_PLE = 256
EPS = 1e-6
NEG_INF = -1e30

FF_CHUNK = 256
N_FF_CHUNKS = D_FF // FF_CHUNK
MIX_CHUNK = 256
PAIR = 4 * CHUNK
PAIR_KEYS = ATT_WINDOW + PAIR
ATT_LOOKAHEAD = 4
SCORE_SLOTS = ATT_LOOKAHEAD + 2
TOKEN_TILE = 512
FFN_TOKEN_TILE = 1024
FFN_ROW_GROUPS = 2
SAMPLE_SEQS_PER_STEP = 8
CONV_PAD = 8
VMEM_LIMIT_BYTES = 56 * 1024 * 1024


def _dot(a, b):
    return jnp.dot(a, b, preferred_element_type=F32)


def _dot_nt(a, b):
    return lax.dot_general(a, b, (((1,), (1,)), ((), ())), preferred_element_type=F32)


def _rmsnorm(x, g):
    return x * lax.rsqrt(jnp.mean(x * x, axis=-1, keepdims=True) + EPS) * g


def _sigmoid(x):
    return 1.0 / (1.0 + jnp.exp(-x))


def _resident(shape):
    return pl.BlockSpec(shape, lambda *_: (0,) * len(shape), pipeline_mode=pl.Buffered(1))


def _params(n_grid_dims):
    return pltpu.CompilerParams(
        dimension_semantics=("arbitrary",) * n_grid_dims,
        vmem_limit_bytes=VMEM_LIMIT_BYTES,
    )


def _swiglu_hidden(u, wg_ref, wu_ref, act_ref, chunks):
    for c in chunks:
        cs = slice(c * FF_CHUNK, (c + 1) * FF_CHUNK)
        gate = _dot(u, wg_ref[:, cs])
        up = _dot(u, wu_ref[:, cs])
        act_ref[:, cs] = (gate * _sigmoid(gate) * up).astype(BF16)


def _row_groups(n_rows):
    n = FFN_ROW_GROUPS if n_rows % FFN_TOKEN_TILE == 0 else 1
    return [slice(k * n_rows // n, (k + 1) * n_rows // n) for k in range(n)]


def _ffn_hidden_and_down(x_ref, g_pre, g_post, wg_ref, wu_ref, wd_ref, act_scr):
    groups = _row_groups(x_ref.shape[0])
    for r in groups:
        u = _rmsnorm(x_ref[r, :], g_pre).astype(BF16)
        _swiglu_hidden(u, wg_ref, wu_ref, act_scr.at[r, :], range(N_FF_CHUNKS))
    ys = [_dot(act_scr[r, :], wd_ref[...]) for r in groups]
    half_g = 0.5 * g_post
    return groups, [x_ref[r, :] + _rmsnorm(y, half_g) for r, y in zip(groups, ys)]


def _ffn_kernel(x_ref, g_ref, wg_ref, wu_ref, wd_ref, o_ref, act_scr, *, pre, post):
    groups, hs = _ffn_hidden_and_down(x_ref, g_ref[pre:pre + 1, :], g_ref[post:post + 1, :],
                                      wg_ref, wu_ref, wd_ref, act_scr)
    for r, h in zip(groups, hs):
        o_ref[r, :] = h


def _ffn_ple_kernel(x_ref, p_ref, g_ref, wg_ref, wu_ref, wd_ref, wpg_ref, wpp_ref, o_ref, act_scr,
                    *, pre, post, ple):
    groups, hs = _ffn_hidden_and_down(x_ref, g_ref[pre:pre + 1, :], g_ref[post:post + 1, :],
                                      wg_ref, wu_ref, wd_ref, act_scr)
    gated = [_sigmoid(_dot(h.astype(BF16), wpg_ref[...]))
             * _dot(p_ref[r, :].astype(BF16), wpp_ref[...]) for r, h in zip(groups, hs)]
    for r, h, gp in zip(groups, hs, gated):
        o_ref[r, :] = h + _rmsnorm(gp, g_ref[ple:ple + 1, :])


def _ffn_call(x, norm_g, wg, wu, wd, pre, post):
    n = x.shape[0]
    tm = min(FFN_TOKEN_TILE, n)
    row = lambda i: (i, 0)
    return pl.pallas_call(
        functools.partial(_ffn_kernel, pre=pre, post=post),
        grid=(n // tm,),
        in_specs=[pl.BlockSpec((tm, D_MODEL), row), _resident(norm_g.shape),
                  _resident(wg.shape), _resident(wu.shape), _resident(wd.shape)],
        out_specs=pl.BlockSpec((tm, D_MODEL), row),
        out_shape=jax.ShapeDtypeStruct((n, D_MODEL), F32),
        scratch_shapes=[pltpu.VMEM((tm, D_FF), BF16)],
        compiler_params=_params(1),
        name="ffn",
    )(x, norm_g, wg, wu, wd)


def _ffn_ple_call(x, p, norm_g, wg, wu, wd, wpg, wpp, pre, post, ple):
    n = x.shape[0]
    tm = min(FFN_TOKEN_TILE, n)
    row = lambda i: (i, 0)
    return pl.pallas_call(
        functools.partial(_ffn_ple_kernel, pre=pre, post=post, ple=ple),
        grid=(n // tm,),
        in_specs=[pl.BlockSpec((tm, D_MODEL), row), pl.BlockSpec((tm, D_PLE), row),
                  _resident(norm_g.shape), _resident(wg.shape), _resident(wu.shape),
                  _resident(wd.shape), _resident(wpg.shape), _resident(wpp.shape)],
        out_specs=pl.BlockSpec((tm, D_MODEL), row),
        out_shape=jax.ShapeDtypeStruct((n, D_MODEL), F32),
        scratch_shapes=[pltpu.VMEM((tm, D_FF), BF16)],
        compiler_params=_params(1),
        name="ffn_ple",
    )(x, p, norm_g, wg, wu, wd, wpg, wpp)


def _build_bias(tab_ref, head, dist, lo, hi):
    def body(j, acc):
        v = lo + j
        return jnp.where(dist == v, tab_ref[head, v + MAX_REL], acc)
    return lax.fori_loop(0, hi - lo + 1, body, jnp.full(dist.shape, NEG_INF, F32))


def _short_conv(s, bgate, convw_ref, conv_scr):
    t = s.shape[0]
    conv_scr[CONV_PAD:CONV_PAD + t, :] = s
    y = (convw_ref[0:1, :] * conv_scr[CONV_PAD - 2:CONV_PAD - 2 + t, :]
         + convw_ref[1:2, :] * conv_scr[CONV_PAD - 1:CONV_PAD - 1 + t, :]
         + convw_ref[2:3, :] * s)
    return bgate * y


def _mix_out(h1, o_att, y_conv, gates, g_post, wao_ref, wco_ref, wo_ref):
    m = (_sigmoid(gates[:, :D_MODEL]) * _dot(o_att.astype(BF16), wao_ref[...])
         + _sigmoid(gates[:, D_MODEL:]) * _dot(y_conv.astype(BF16), wco_ref[...]))
    return h1 + _rmsnorm(_dot(m.astype(BF16), wo_ref[...]), g_post)


def _mixer_prompt_kernel(tab_ref, h1_ref, g_ref, wq_ref, wkt_ref, wv_ref, wxbc_ref, wg_ref,
                         convw_ref, wao_ref, wco_ref, wo_ref,
                         h2_ref, kout_ref, vout_ref, cout_ref,
                         q_scr, kt_scr, v_scr, bias_scr, s_scr, o_scr, conv_scr, yc_scr, u_scr, m_scr,
                         *, tm, tiles_per_seq):
    b = pl.program_id(0)
    i = pl.program_id(1)
    n_pairs = tm // PAIR

    @pl.when((b == 0) & (i == 0))
    def _():
        rows = 32
        for rb in range(PAIR // rows):
            r = lax.broadcasted_iota(jnp.int32, (rows, PAIR_KEYS), 0) + rb * rows
            c = lax.broadcasted_iota(jnp.int32, (rows, PAIR_KEYS), 1)
            n = c - (r & (PAIR - CHUNK))
            d = jnp.minimum((r & (CHUNK - 1)) + ATT_WINDOW - n, MAX_REL)
            d = jnp.where((n >= 0) & (n < BAND), d, MAX_REL + 1)

            def per_head(h, carry):
                bias_scr[h, rb * rows:(rb + 1) * rows, :] = _build_bias(
                    tab_ref, h, d, -(CHUNK - 1), MAX_REL)
                return carry
            lax.fori_loop(0, N_HEADS, per_head, 0)

    @pl.when(i == 0)
    def _():
        kt_scr[:, :, :ATT_WINDOW] = jnp.zeros((N_HEADS, HEAD_DIM, ATT_WINDOW), BF16)
        v_scr[:, :ATT_WINDOW, :] = jnp.zeros((N_HEADS, ATT_WINDOW, 2 * HEAD_DIM), BF16)
        conv_scr[0:CONV_PAD, :] = jnp.zeros((CONV_PAD, D_CONV), F32)

    u = _rmsnorm(h1_ref[...], g_ref[2:3, :]).astype(BF16)
    u_scr[...] = u

    for c in range(D_CONV // MIX_CHUNK):
        cs = slice(c * MIX_CHUNK, (c + 1) * MIX_CHUNK)
        x_in = _dot(u, wxbc_ref[:, cs])
        b_gate = _dot(u, wxbc_ref[:, D_CONV + c * MIX_CHUNK:D_CONV + (c + 1) * MIX_CHUNK])
        c_gate = _dot(u, wxbc_ref[:, 2 * D_CONV + c * MIX_CHUNK:2 * D_CONV + (c + 1) * MIX_CHUNK])
        s_conv = c_gate * x_in
        yc_scr[:, cs] = _short_conv(s_conv, b_gate, convw_ref.at[:, cs],
                                    conv_scr.at[:, cs]).astype(BF16)
    new_prefix = conv_scr[CONV_PAD + tm - 2:CONV_PAD + tm, :]
    conv_scr[CONV_PAD - 2:CONV_PAD, :] = new_prefix

    q = _dot(u, wq_ref[...]) * (HEAD_DIM ** -0.5)
    kt = _dot_nt(wkt_ref[...], u)
    v = _dot(u, wv_ref[...])
    ones = jnp.ones((tm, HEAD_DIM), F32)
    for h in range(N_HEADS):
        hs = slice(h * HEAD_DIM, (h + 1) * HEAD_DIM)
        q_scr[h] = q[:, hs].astype(BF16)
        kt_scr[h, :, ATT_WINDOW:] = kt[hs, :].astype(BF16)
        v_scr[h, ATT_WINDOW:, :] = jnp.concatenate([v[:, hs], ones], axis=-1).astype(BF16)

    @pl.when(i == tiles_per_seq - 1)
    def _():
        kout_ref[0] = kt[:, tm - ATT_WINDOW:]
        vout_ref[0] = v[tm - ATT_WINDOW:, :]
        cout_ref[0] = new_prefix

    def attend(first_tile):
        col = lax.broadcasted_iota(jnp.int32, (1, PAIR_KEYS), 1)

        def scores(h, jj, slot):
            k0 = jj * PAIR
            s = _dot(q_scr[h, k0:k0 + PAIR, :], kt_scr[h, :, k0:k0 + PAIR_KEYS])
            s = s + bias_scr[h]
            if first_tile:
                s = s + jnp.where(col + k0 < ATT_WINDOW, NEG_INF, 0.0)
            s_scr[slot] = s
            return jnp.max(s, axis=-1, keepdims=True)

        def weighted_values(h, jj, slot, m):
            k0 = jj * PAIR
            p = jnp.exp(s_scr[slot] - m).astype(BF16)
            o_scr[h, k0:k0 + PAIR, :] = _dot(p, v_scr[h, k0:k0 + PAIR_KEYS, :])

        steps = [(h, jj) for h in range(N_HEADS) for jj in range(n_pairs)]
        maxes = {}
        for k in range(len(steps) + ATT_LOOKAHEAD):
            if k < len(steps):
                maxes[k] = scores(*steps[k], k % SCORE_SLOTS)
            if k >= ATT_LOOKAHEAD:
                j = k - ATT_LOOKAHEAD
                weighted_values(*steps[j], j % SCORE_SLOTS, maxes.pop(j))

    @pl.when(i == 0)
    def _():
        attend(True)

    @pl.when(i > 0)
    def _():
        attend(False)

    u = u_scr[...]
    y_conv = yc_scr[...]
    low_half = lax.broadcasted_iota(jnp.int32, (tm, 2 * HEAD_DIM), 1) < HEAD_DIM
    pairs = []
    for j in range(N_HEADS // 2):
        even = o_scr[2 * j]
        odd = o_scr[2 * j + 1]
        pairs.append(jnp.where(low_half,
                               even * pltpu.roll(1.0 / even, HEAD_DIM, 1),
                               pltpu.roll(odd, HEAD_DIM, 1) * (1.0 / odd)))
    o_att = jnp.concatenate(pairs, axis=-1).astype(BF16)
    for c in range(D_MODEL // MIX_CHUNK):
        cs = slice(c * MIX_CHUNK, (c + 1) * MIX_CHUNK)
        g_conv = _dot(u, wg_ref[:, D_MODEL + c * MIX_CHUNK:D_MODEL + (c + 1) * MIX_CHUNK])
        m_conv = _sigmoid(g_conv) * _dot(y_conv, wco_ref[:, cs])
        g_att = _dot(u, wg_ref[:, cs])
        m_scr[:, cs] = (_sigmoid(g_att) * _dot(o_att, wao_ref[:, cs]) + m_conv).astype(BF16)
    kt_scr[:, :, :ATT_WINDOW] = kt_scr[:, :, tm:]
    v_scr[:, :ATT_WINDOW, :] = v_scr[:, tm:, :]
    halves = (slice(0, tm // 2), slice(tm // 2, tm))
    outs = [_dot(m_scr[r, :], wo_ref[...]) for r in halves]
    for r, mo in zip(halves, outs):
        h2_ref[r, :] = h1_ref[r, :] + _rmsnorm(mo, g_ref[3:4, :])


def _mixer_prompt_call(tab, h1, norm_g, wq, wkt, wv, wxbc, wg, convw, wao, wco, wo, batch, seq):
    tm = TOKEN_TILE
    tiles = seq // tm
    row = lambda b, i: (b * tiles + i, 0)
    per_seq = lambda b, i: (b, 0, 0)
    weights = (wq, wkt, wv, wxbc, wg, convw, wao, wco, wo)
    return pl.pallas_call(
        functools.partial(_mixer_prompt_kernel, tm=tm, tiles_per_seq=tiles),
        grid=(batch, tiles),
        in_specs=[pl.BlockSpec(memory_space=pltpu.SMEM),
                  pl.BlockSpec((tm, D_MODEL), row), _resident(norm_g.shape)]
                 + [_resident(w.shape) for w in weights],
        out_specs=[pl.BlockSpec((tm, D_MODEL), row),
                   pl.BlockSpec((1, D_ATT, ATT_WINDOW), per_seq),
                   pl.BlockSpec((1, ATT_WINDOW, D_ATT), per_seq),
                   pl.BlockSpec((1, CONV_WIDTH - 1, D_CONV), per_seq)],
        out_shape=[jax.ShapeDtypeStruct((batch * seq, D_MODEL), F32),
                   jax.ShapeDtypeStruct((batch, D_ATT, ATT_WINDOW), F32),
                   jax.ShapeDtypeStruct((batch, ATT_WINDOW, D_ATT), F32),
                   jax.ShapeDtypeStruct((batch, CONV_WIDTH - 1, D_CONV), F32)],
        scratch_shapes=[pltpu.VMEM((N_HEADS, tm, HEAD_DIM), BF16),
                        pltpu.VMEM((N_HEADS, HEAD_DIM, ATT_WINDOW + tm), BF16),
                        pltpu.VMEM((N_HEADS, ATT_WINDOW + tm, 2 * HEAD_DIM), BF16),
                        pltpu.VMEM((N_HEADS, PAIR, PAIR_KEYS), F32),
                        pltpu.VMEM((SCORE_SLOTS, PAIR, PAIR_KEYS), F32),
                        pltpu.VMEM((N_HEADS, tm, 2 * HEAD_DIM), F32),
                        pltpu.VMEM((CONV_PAD + tm, D_CONV), F32),
                        pltpu.VMEM((tm, D_CONV), BF16),
                        pltpu.VMEM((tm, D_MODEL), BF16),
                        pltpu.VMEM((tm, D_MODEL), BF16)],
        compiler_params=_params(2),
        name="mixer_prompt",
    )(tab, h1, norm_g, *weights)


def _inproj_kernel(h1_ref, g_ref, wq_ref, wk_ref, wv_ref, wxbc_ref, wg_ref,
                   q_ref, k_ref, v_ref, xbc_ref, gates_ref):
    u = _rmsnorm(h1_ref[...], g_ref[2:3, :]).astype(BF16)
    q_ref[...] = _dot(u, wq_ref[...]) * (HEAD_DIM ** -0.5)
    k_ref[...] = _dot(u, wk_ref[...])
    v_ref[...] = _dot(u, wv_ref[...])
    xbc_ref[...] = _dot(u, wxbc_ref[...])
    gates_ref[...] = _dot(u, wg_ref[...])


def _inproj_call(h1, norm_g, wq, wk, wv, wxbc, wg):
    n = h1.shape[0]
    ins = (h1, norm_g, wq, wk, wv, wxbc, wg)
    widths = (D_ATT, D_ATT, D_ATT, 3 * D_CONV, 2 * D_MODEL)
    return pl.pallas_call(
        _inproj_kernel,
        grid=(1,),
        in_specs=[_resident(a.shape) for a in ins],
        out_specs=[pl.BlockSpec((n, w), lambda i: (0, 0)) for w in widths],
        out_shape=[jax.ShapeDtypeStruct((n, w), F32) for w in widths],
        compiler_params=_params(1),
        name="inproj_sample",
    )(*ins)


def _attn_sample_kernel(tab_ref, q_ref, k_ref, v_ref, ck_ref, cv_ref, xbc_ref, cconv_ref, convw_ref,
                        o_ref, y_ref, cout_ref, bias_c_scr, bias_n_scr, conv_scr, *, t, lc):
    @pl.when(pl.program_id(0) == 0)
    def _():
        r = lax.broadcasted_iota(jnp.int32, (t, lc), 0)
        c = lax.broadcasted_iota(jnp.int32, (t, lc), 1)
        d_c = jnp.minimum(lc + r - c, MAX_REL)
        r = lax.broadcasted_iota(jnp.int32, (t, t), 0)
        c = lax.broadcasted_iota(jnp.int32, (t, t), 1)
        d_n = jnp.clip(r - c, -MAX_REL, MAX_REL)

        def per_head(h, carry):
            bias_c_scr[h] = _build_bias(tab_ref, h, d_c, 1, MAX_REL)
            bias_n_scr[h] = _build_bias(tab_ref, h, d_n, -min(t - 1, MAX_REL), min(t - 1, MAX_REL))
            return carry
        lax.fori_loop(0, N_HEADS, per_head, 0)

    rows = N_HEADS * t
    row_head = lax.broadcasted_iota(jnp.int32, (N_HEADS, t, D_ATT), 0).reshape(rows, D_ATT)
    col = lax.broadcasted_iota(jnp.int32, (rows, D_ATT), 1)
    own_head = (col >= row_head * HEAD_DIM) & (col < (row_head + 1) * HEAD_DIM)
    bias_c = bias_c_scr[...].reshape(rows, lc)
    bias_n = bias_n_scr[...].reshape(rows, t)
    conv_scr[0:CONV_PAD, :] = jnp.zeros((CONV_PAD, D_CONV), F32)

    for sq in range(q_ref.shape[0] // t):
        rs = slice(sq * t, (sq + 1) * t)
        q_heads = jnp.where(own_head, jnp.concatenate([q_ref[rs, :]] * N_HEADS, axis=0),
                            0.0).astype(BF16)
        k_new = k_ref[rs, :].astype(BF16)
        v_new = v_ref[rs, :].astype(BF16)
        s_c = _dot(q_heads, ck_ref[sq].astype(BF16)) + bias_c
        s_n = _dot_nt(q_heads, k_new) + bias_n
        m = jnp.maximum(jnp.max(s_c, axis=-1, keepdims=True),
                        jnp.max(s_n, axis=-1, keepdims=True))
        p_c = jnp.exp(s_c - m)
        p_n = jnp.exp(s_n - m)
        l = jnp.sum(p_c, axis=-1, keepdims=True) + jnp.sum(p_n, axis=-1, keepdims=True)
        o_all = (_dot_nt(p_c.astype(BF16), cv_ref[sq].astype(BF16))
                 + _dot(p_n.astype(BF16), v_new)) * (1.0 / l)
        o_all = jnp.where(own_head, o_all, 0.0)
        o = o_all[0:t]
        for h in range(1, N_HEADS):
            o = o + o_all[h * t:(h + 1) * t]
        o_ref[rs, :] = o

        xbc = xbc_ref[rs, :]
        s_conv = xbc[:, 2 * D_CONV:] * xbc[:, :D_CONV]
        conv_scr[CONV_PAD - 2:CONV_PAD, :] = cconv_ref[sq]
        y_ref[rs, :] = _short_conv(s_conv, xbc[:, D_CONV:2 * D_CONV], convw_ref, conv_scr)
        cout_ref[sq] = conv_scr[CONV_PAD + t - 2:CONV_PAD + t, :]


def _attn_sample_call(tab, q, k, v, cache_k, cache_v, xbc, cache_conv, convw, batch, t):
    lc = cache_k.shape[1]
    n_seq = SAMPLE_SEQS_PER_STEP if batch % SAMPLE_SEQS_PER_STEP == 0 else 1
    row = lambda b: (b, 0)
    per_seq = lambda b: (b, 0, 0)
    cache_k = cache_k.transpose(0, 2, 3, 1).reshape(batch, D_ATT, lc)
    cache_v = cache_v.transpose(0, 2, 3, 1).reshape(batch, D_ATT, lc)
    cache_spec = pl.BlockSpec((n_seq, D_ATT, lc), per_seq)
    return pl.pallas_call(
        functools.partial(_attn_sample_kernel, t=t, lc=lc),
        grid=(batch // n_seq,),
        in_specs=[pl.BlockSpec(memory_space=pltpu.SMEM),
                  pl.BlockSpec((n_seq * t, D_ATT), row), pl.BlockSpec((n_seq * t, D_ATT), row),
                  pl.BlockSpec((n_seq * t, D_ATT), row),
                  cache_spec, cache_spec,
                  pl.BlockSpec((n_seq * t, 3 * D_CONV), row),
                  pl.BlockSpec((n_seq, CONV_WIDTH - 1, D_CONV), per_seq),
                  _resident(convw.shape)],
        out_specs=[pl.BlockSpec((n_seq * t, D_ATT), row), pl.BlockSpec((n_seq * t, D_CONV), row),
                   pl.BlockSpec((n_seq, CONV_WIDTH - 1, D_CONV), per_seq)],
        out_shape=[jax.ShapeDtypeStruct((batch * t, D_ATT), F32),
                   jax.ShapeDtypeStruct((batch * t, D_CONV), F32),
                   jax.ShapeDtypeStruct((batch, CONV_WIDTH - 1, D_CONV), F32)],
        scratch_shapes=[pltpu.VMEM((N_HEADS, t, lc), F32),
                        pltpu.VMEM((N_HEADS, t, t), F32),
                        pltpu.VMEM((CONV_PAD + t, D_CONV), F32)],
        compiler_params=_params(1),
        name="attn_sample",
    )(tab, q, k, v, cache_k, cache_v, xbc, cache_conv, convw)


def _mix_kernel(h1_ref, o_ref, y_ref, gates_ref, g_ref, wao_ref, wco_ref, wo_ref, h2_ref):
    h2_ref[...] = _mix_out(h1_ref[...], o_ref[...], y_ref[...], gates_ref[...], g_ref[3:4, :],
                           wao_ref, wco_ref, wo_ref)


def _mix_call(h1, o_att, y_conv, gates, norm_g, wao, wco, wo):
    ins = (h1, o_att, y_conv, gates, norm_g, wao, wco, wo)
    return pl.pallas_call(
        _mix_kernel,
        grid=(1,),
        in_specs=[_resident(a.shape) for a in ins],
        out_specs=pl.BlockSpec(h1.shape, lambda i: (0, 0)),
        out_shape=jax.ShapeDtypeStruct(h1.shape, F32),
        compiler_params=_params(1),
        name="mix_sample",
    )(*ins)


def kernel(x_prompt, x_sample, cache_k, cache_v, cache_conv, p_prompt, p_sample, norm_g,
           w1_gate, w1_up, w1_down, w_in, conv_w, rel_bias, w_att_out, w_conv_out, w_out,
           w2_gate, w2_up, w2_down, w_ple_gate, w_ple_proj):
    depth = norm_g.shape[0]
    assert depth == 1, "one layer per step"
    batch, seq, _ = x_prompt.shape
    dec_batch, dec_seq, _ = x_sample.shape
    assert seq % TOKEN_TILE == 0 and TOKEN_TILE % PAIR == 0 and TOKEN_TILE >= ATT_WINDOW
    l = 0

    g = norm_g[l]
    w1g = w1_gate[l].astype(BF16)
    w1u = w1_up[l].astype(BF16)
    w2g = w2_gate[l].astype(BF16)
    w2u = w2_up[l].astype(BF16)
    w1d = w1_down[l].astype(BF16)
    w2d = w2_down[l].astype(BF16)
    win = w_in[l].astype(BF16)
    wq = win[:, :D_ATT]
    wk = win[:, D_ATT:2 * D_ATT]
    wv = win[:, 2 * D_ATT:3 * D_ATT]
    wxbc = win[:, 3 * D_ATT:3 * D_ATT + 3 * D_CONV]
    wg = win[:, 3 * D_ATT + 3 * D_CONV:]
    wkt = wk.T
    wao = w_att_out[l].astype(BF16)
    wco = w_conv_out[l].astype(BF16)
    wo = w_out[l].astype(BF16)
    wpg = w_ple_gate[l].astype(BF16)
    wpp = w_ple_proj[l].astype(BF16)
    tab = rel_bias[l]
    convw = conv_w[l]

    xp = x_prompt.reshape(batch * seq, D_MODEL)
    h1p = _ffn_call(xp, g, w1g, w1u, w1d, 0, 1)
    h2p, kt_p, v_p, c_p = _mixer_prompt_call(tab, h1p, g, wq, wkt, wv, wxbc, wg, convw,
                                            wao, wco, wo, batch, seq)
    y_p = _ffn_ple_call(h2p, p_prompt[l].reshape(batch * seq, D_PLE), g, w2g, w2u, w2d, wpg, wpp,
                        4, 5, 6)

    lc = cache_k.shape[2]
    xs = x_sample.reshape(dec_batch * dec_seq, D_MODEL)
    h1s = _ffn_call(xs, g, w1g, w1u, w1d, 0, 1)
    q_s, k_s, v_s, xbc_s, gates_s = _inproj_call(h1s, g, wq, wk, wv, wxbc, wg)
    o_s, yc_s, c_s = _attn_sample_call(tab, q_s, k_s, v_s,
                                       cache_k[l], cache_v[l],
                                       xbc_s, cache_conv[l], convw, dec_batch, dec_seq)
    h2s = _mix_call(h1s, o_s, yc_s, gates_s, g, wao, wco, wo)
    y_s = _ffn_ple_call(h2s, p_sample[l].reshape(dec_batch * dec_seq, D_PLE), g, w2g, w2u, w2d,
                        wpg, wpp, 4, 5, 6)

    kv_p = (1, batch, ATT_WINDOW, N_HEADS, HEAD_DIM)
    kv_s = (1, dec_batch, dec_seq, N_HEADS, HEAD_DIM)
    k_p = kt_p.reshape(batch, N_HEADS, HEAD_DIM, ATT_WINDOW).transpose(0, 3, 1, 2)
    return (y_p.reshape(batch, seq, D_MODEL), y_s.reshape(dec_batch, dec_seq, D_MODEL),
            k_p.reshape(kv_p), v_p.reshape(kv_p), c_p[None],
            k_s.reshape(kv_s), v_s.reshape(kv_s), c_s[None])
```

```python
import functools

import jax
import jax.numpy as jnp
from jax import lax
from jax.experimental import pallas as pl
from jax.experimental.pallas import tpu as pltpu

F32 = jnp.float32
BF16 = jnp.bfloat16

D_MODEL = 1024
CHUNK = 64
LEFT_CHUNKS = 8
ATT_WINDOW = LEFT_CHUNKS * CHUNK
BAND = ATT_WINDOW + CHUNK
N_HEADS = 8
HEAD_DIM = 64
D_ATT = N_HEADS * HEAD_DIM
D_CONV = D_MODEL // 2
CONV_WIDTH = 3
MAX_REL = 128
D_FF = 2816
D_PLE = 256
EPS = 1e-6
NEG_INF = -1e30
LOG2_E = 1.4426950408889634

FF_CHUNK = 256
N_FF_CHUNKS = D_FF // FF_CHUNK
MIX_CHUNK = 256
PAIR = 2 * CHUNK
PAIR_KEYS = ATT_WINDOW + PAIR
ATT_LOOKAHEAD = 4
SCORE_SLOTS = ATT_LOOKAHEAD + 2
TOKEN_TILE = 512
FFN_TOKEN_TILE = 1024
FFN_ROW_GROUPS = 2
SAMPLE_SEQS_PER_STEP = 8
CONV_PAD = 8
VMEM_LIMIT_BYTES = 56 * 1024 * 1024


def _dot(a, b):
    return jnp.dot(a, b, preferred_element_type=F32)


def _dot_nt(a, b):
    return lax.dot_general(a, b, (((1,), (1,)), ((), ())), preferred_element_type=F32)


def _rmsnorm(x, g):
    return x * lax.rsqrt(jnp.mean(x * x, axis=-1, keepdims=True) + EPS) * g


def _sigmoid(x):
    return 1.0 / (1.0 + jnp.exp(-x))


def _resident(shape):
    return pl.BlockSpec(shape, lambda *_: (0,) * len(shape), pipeline_mode=pl.Buffered(1))


def _params(n_grid_dims):
    return pltpu.CompilerParams(
        dimension_semantics=("arbitrary",) * n_grid_dims,
        vmem_limit_bytes=VMEM_LIMIT_BYTES,
    )


def _swiglu_hidden(u, wg_ref, wu_ref, act_ref, chunks):
    for c in chunks:
        cs = slice(c * FF_CHUNK, (c + 1) * FF_CHUNK)
        gate = _dot(u, wg_ref[:, cs])
        up = _dot(u, wu_ref[:, cs])
        act_ref[:, cs] = (gate * _sigmoid(gate) * up).astype(BF16)


def _row_groups(n_rows):
    n = FFN_ROW_GROUPS if n_rows % FFN_TOKEN_TILE == 0 else 1
    return [slice(k * n_rows // n, (k + 1) * n_rows // n) for k in range(n)]


def _ffn_hidden_and_down(x_ref, g_pre, g_post, wg_ref, wu_ref, wd_ref, act_scr):
    groups = _row_groups(x_ref.shape[0])
    for r in groups:
        u = _rmsnorm(x_ref[r, :], g_pre).astype(BF16)
        _swiglu_hidden(u, wg_ref, wu_ref, act_scr.at[r, :], range(N_FF_CHUNKS))
    ys = [_dot(act_scr[r, :], wd_ref[...]) for r in groups]
    half_g = 0.5 * g_post
    return groups, [x_ref[r, :] + _rmsnorm(y, half_g) for r, y in zip(groups, ys)]


def _ffn_kernel(x_ref, g_ref, wg_ref, wu_ref, wd_ref, o_ref, act_scr, *, pre, post):
    groups, hs = _ffn_hidden_and_down(x_ref, g_ref[pre:pre + 1, :], g_ref[post:post + 1, :],
                                      wg_ref, wu_ref, wd_ref, act_scr)
    for r, h in zip(groups, hs):
        o_ref[r, :] = h


def _ffn_ple_kernel(x_ref, p_ref, g_ref, wg_ref, wu_ref, wd_ref, wpg_ref, wpp_ref, o_ref, act_scr,
                    *, pre, post, ple):
    groups, hs = _ffn_hidden_and_down(x_ref, g_ref[pre:pre + 1, :], g_ref[post:post + 1, :],
                                      wg_ref, wu_ref, wd_ref, act_scr)
    gated = [_sigmoid(_dot(h.astype(BF16), wpg_ref[...]))
             * _dot(p_ref[r, :].astype(BF16), wpp_ref[...]) for r, h in zip(groups, hs)]
    for r, h, gp in zip(groups, hs, gated):
        o_ref[r, :] = h + _rmsnorm(gp, g_ref[ple:ple + 1, :])


def _ffn_call(x, norm_g, wg, wu, wd, pre, post):
    n = x.shape[0]
    tm = min(FFN_TOKEN_TILE, n)
    row = lambda i: (i, 0)
    return pl.pallas_call(
        functools.partial(_ffn_kernel, pre=pre, post=post),
        grid=(n // tm,),
        in_specs=[pl.BlockSpec((tm, D_MODEL), row), _resident(norm_g.shape),
                  _resident(wg.shape), _resident(wu.shape), _resident(wd.shape)],
        out_specs=pl.BlockSpec((tm, D_MODEL), row),
        out_shape=jax.ShapeDtypeStruct((n, D_MODEL), F32),
        scratch_shapes=[pltpu.VMEM((tm, D_FF), BF16)],
        compiler_params=_params(1),
        name="ffn",
    )(x, norm_g, wg, wu, wd)


def _ffn_ple_call(x, p, norm_g, wg, wu, wd, wpg, wpp, pre, post, ple):
    n = x.shape[0]
    tm = min(FFN_TOKEN_TILE, n)
    row = lambda i: (i, 0)
    return pl.pallas_call(
        functools.partial(_ffn_ple_kernel, pre=pre, post=post, ple=ple),
        grid=(n // tm,),
        in_specs=[pl.BlockSpec((tm, D_MODEL), row), pl.BlockSpec((tm, D_PLE), row),
                  _resident(norm_g.shape), _resident(wg.shape), _resident(wu.shape),
                  _resident(wd.shape), _resident(wpg.shape), _resident(wpp.shape)],
        out_specs=pl.BlockSpec((tm, D_MODEL), row),
        out_shape=jax.ShapeDtypeStruct((n, D_MODEL), F32),
        scratch_shapes=[pltpu.VMEM((tm, D_FF), BF16)],
        compiler_params=_params(1),
        name="ffn_ple",
    )(x, p, norm_g, wg, wu, wd, wpg, wpp)


def _build_bias(tab_ref, head, dist, lo, hi):
    def body(j, acc):
        v = lo + j
        return jnp.where(dist == v, tab_ref[head, v + MAX_REL], acc)
    return lax.fori_loop(0, hi - lo + 1, body, jnp.full(dist.shape, NEG_INF, F32))


def _short_conv(s, bgate, convw_ref, conv_scr):
    t = s.shape[0]
    conv_scr[CONV_PAD:CONV_PAD + t, :] = s
    y = (convw_ref[0:1, :] * conv_scr[CONV_PAD - 2:CONV_PAD - 2 + t, :]
         + convw_ref[1:2, :] * conv_scr[CONV_PAD - 1:CONV_PAD - 1 + t, :]
         + convw_ref[2:3, :] * s)
    return bgate * y


def _mix_out(h1, o_att, y_conv, gates, g_post, wao_ref, wco_ref, wo_ref):
    m = (_sigmoid(gates[:, :D_MODEL]) * _dot(o_att.astype(BF16), wao_ref[...])
         + _sigmoid(gates[:, D_MODEL:]) * _dot(y_conv.astype(BF16), wco_ref[...]))
    return h1 + _rmsnorm(_dot(m.astype(BF16), wo_ref[...]), g_post)


def _mixer_prompt_kernel(tab_ref, h1_ref, g_ref, wq_ref, wkt_ref, wv_ref, wxbc_ref, wg_ref,
                         convw_ref, wao_ref, wco_ref, wo_ref,
                         h2_ref, kout_ref, vout_ref, cout_ref,
                         q_scr, kt_scr, v_scr, bias_scr, s_scr, o_scr, conv_scr, yc_scr, u_scr, m_scr,
                         *, tm, tiles_per_seq):
    b = pl.program_id(0)
    i = pl.program_id(1)
    n_pairs = tm // PAIR

    @pl.when((b == 0) & (i == 0))
    def _():
        rows = 32
        for rb in range(PAIR // rows):
            r = lax.broadcasted_iota(jnp.int32, (rows, PAIR_KEYS), 0) + rb * rows
            c = lax.broadcasted_iota(jnp.int32, (rows, PAIR_KEYS), 1)
            n = c - (r & CHUNK)
            d = jnp.minimum((r & (CHUNK - 1)) + ATT_WINDOW - n, MAX_REL)
            d = jnp.where((n >= 0) & (n < BAND), d, MAX_REL + 1)

            def per_head(h, carry):
                bias_scr[h, rb * rows:(rb + 1) * rows, :] = LOG2_E * _build_bias(
                    tab_ref, h, d, -(CHUNK - 1), MAX_REL)
                return carry
            lax.fori_loop(0, N_HEADS, per_head, 0)

    @pl.when(i == 0)
    def _():
        kt_scr[:, :, :ATT_WINDOW] = jnp.zeros((N_HEADS, HEAD_DIM, ATT_WINDOW), BF16)
        v_scr[:, :ATT_WINDOW, :] = jnp.zeros((N_HEADS, ATT_WINDOW, 2 * HEAD_DIM), BF16)
        conv_scr[0:CONV_PAD, :] = jnp.zeros((CONV_PAD, D_CONV), F32)

    u = _rmsnorm(h1_ref[...], g_ref[2:3, :]).astype(BF16)
    u_scr[...] = u

    for c in range(D_CONV // MIX_CHUNK):
        cs = slice(c * MIX_CHUNK, (c + 1) * MIX_CHUNK)
        x_in = _dot(u, wxbc_ref[:, cs])
        b_gate = _dot(u, wxbc_ref[:, D_CONV + c * MIX_CHUNK:D_CONV + (c + 1) * MIX_CHUNK])
        c_gate = _dot(u, wxbc_ref[:, 2 * D_CONV + c * MIX_CHUNK:2 * D_CONV + (c + 1) * MIX_CHUNK])
        s_conv = c_gate * x_in
        yc_scr[:, cs] = _short_conv(s_conv, b_gate, convw_ref.at[:, cs],
                                    conv_scr.at[:, cs]).astype(BF16)
    new_prefix = conv_scr[CONV_PAD + tm - 2:CONV_PAD + tm, :]
    conv_scr[CONV_PAD - 2:CONV_PAD, :] = new_prefix

    q = _dot(u, wq_ref[...]) * (LOG2_E * HEAD_DIM ** -0.5)
    kt = _dot_nt(wkt_ref[...], u)
    v = _dot(u, wv_ref[...])
    ones = jnp.ones((tm, HEAD_DIM), F32)
    for h in range(N_HEADS):
        hs = slice(h * HEAD_DIM, (h + 1) * HEAD_DIM)
        q_scr[h] = q[:, hs].astype(BF16)
        kt_scr[h, :, ATT_WINDOW:] = kt[hs, :].astype(BF16)
        v_scr[h, ATT_WINDOW:, :] = jnp.concatenate([v[:, hs], ones], axis=-1).astype(BF16)

    @pl.when(i == tiles_per_seq - 1)
    def _():
        kout_ref[0] = kt[:, tm - ATT_WINDOW:]
        vout_ref[0] = v[tm - ATT_WINDOW:, :]
        cout_ref[0] = new_prefix

    def attend(first_tile):
        col = lax.broadcasted_iota(jnp.int32, (1, PAIR_KEYS), 1)

        def scores(h, jj, slot):
            k0 = jj * PAIR
            s = _dot(q_scr[h, k0:k0 + PAIR, :], kt_scr[h, :, k0:k0 + PAIR_KEYS])
            s = s + bias_scr[h]
            if first_tile:
                s = s + jnp.where(col + k0 < ATT_WINDOW, NEG_INF, 0.0)
            s_scr[slot] = s
            return jnp.max(s, axis=-1, keepdims=True)

        def weighted_values(h, jj, slot, m):
            k0 = jj * PAIR
            p = jnp.exp2(s_scr[slot] - m).astype(BF16)
            o_scr[h, k0:k0 + PAIR, :] = _dot(p, v_scr[h, k0:k0 + PAIR_KEYS, :])

        steps = [(h, jj) for h in range(N_HEADS) for jj in range(n_pairs)]
        maxes = {}
        for k in range(len(steps) + ATT_LOOKAHEAD):
            if k < len(steps):
                maxes[k] = scores(*steps[k], k % SCORE_SLOTS)
            if k >= ATT_LOOKAHEAD:
                j = k - ATT_LOOKAHEAD
                weighted_values(*steps[j], j % SCORE_SLOTS, maxes.pop(j))

    @pl.when(i == 0)
    def _():
        attend(True)

    @pl.when(i > 0)
    def _():
        attend(False)

    u = u_scr[...]
    y_conv = yc_scr[...]
    low_half = lax.broadcasted_iota(jnp.int32, (tm, 2 * HEAD_DIM), 1) < HEAD_DIM
    pairs = []
    for j in range(N_HEADS // 2):
        even = o_scr[2 * j]
        odd = o_scr[2 * j + 1]
        pairs.append(jnp.where(low_half,
                               even * pltpu.roll(1.0 / even, HEAD_DIM, 1),
                               pltpu.roll(odd, HEAD_DIM, 1) * (1.0 / odd)))
    o_att = jnp.concatenate(pairs, axis=-1).astype(BF16)
    for c in range(D_MODEL // MIX_CHUNK):
        cs = slice(c * MIX_CHUNK, (c + 1) * MIX_CHUNK)
        g_conv = _dot(u, wg_ref[:, D_MODEL + c * MIX_CHUNK:D_MODEL + (c + 1) * MIX_CHUNK])
        m_conv = _sigmoid(g_conv) * _dot(y_conv, wco_ref[:, cs])
        g_att = _dot(u, wg_ref[:, cs])
        m_scr[:, cs] = (_sigmoid(g_att) * _dot(o_att, wao_ref[:, cs]) + m_conv).astype(BF16)
    kt_scr[:, :, :ATT_WINDOW] = kt_scr[:, :, tm:]
    v_scr[:, :ATT_WINDOW, :] = v_scr[:, tm:, :]
    halves = (slice(0, tm // 2), slice(tm // 2, tm))
    outs = [_dot(m_scr[r, :], wo_ref[...]) for r in halves]
    for r, mo in zip(halves, outs):
        h2_ref[r, :] = h1_ref[r, :] + _rmsnorm(mo, g_ref[3:4, :])


def _mixer_prompt_call(tab, h1, norm_g, wq, wkt, wv, wxbc, wg, convw, wao, wco, wo, batch, seq):
    tm = TOKEN_TILE
    tiles = seq // tm
    row = lambda b, i: (b * tiles + i, 0)
    per_seq = lambda b, i: (b, 0, 0)
    weights = (wq, wkt, wv, wxbc, wg, convw, wao, wco, wo)
    return pl.pallas_call(
        functools.partial(_mixer_prompt_kernel, tm=tm, tiles_per_seq=tiles),
        grid=(batch, tiles),
        in_specs=[pl.BlockSpec(memory_space=pltpu.SMEM),
                  pl.BlockSpec((tm, D_MODEL), row), _resident(norm_g.shape)]
                 + [_resident(w.shape) for w in weights],
        out_specs=[pl.BlockSpec((tm, D_MODEL), row),
                   pl.BlockSpec((1, D_ATT, ATT_WINDOW), per_seq),
                   pl.BlockSpec((1, ATT_WINDOW, D_ATT), per_seq),
                   pl.BlockSpec((1, CONV_WIDTH - 1, D_CONV), per_seq)],
        out_shape=[jax.ShapeDtypeStruct((batch * seq, D_MODEL), F32),
                   jax.ShapeDtypeStruct((batch, D_ATT, ATT_WINDOW), F32),
                   jax.ShapeDtypeStruct((batch, ATT_WINDOW, D_ATT), F32),
                   jax.ShapeDtypeStruct((batch, CONV_WIDTH - 1, D_CONV), F32)],
        scratch_shapes=[pltpu.VMEM((N_HEADS, tm, HEAD_DIM), BF16),
                        pltpu.VMEM((N_HEADS, HEAD_DIM, ATT_WINDOW + tm), BF16),
                        pltpu.VMEM((N_HEADS, ATT_WINDOW + tm, 2 * HEAD_DIM), BF16),
                        pltpu.VMEM((N_HEADS, PAIR, PAIR_KEYS), F32),
                        pltpu.VMEM((SCORE_SLOTS, PAIR, PAIR_KEYS), F32),
                        pltpu.VMEM((N_HEADS, tm, 2 * HEAD_DIM), F32),
                        pltpu.VMEM((CONV_PAD + tm, D_CONV), F32),
                        pltpu.VMEM((tm, D_CONV), BF16),
                        pltpu.VMEM((tm, D_MODEL), BF16),
                        pltpu.VMEM((tm, D_MODEL), BF16)],
        compiler_params=_params(2),
        name="mixer_prompt",
    )(tab, h1, norm_g, *weights)


def _inproj_kernel(h1_ref, g_ref, wq_ref, wk_ref, wv_ref, wxbc_ref, wg_ref,
                   q_ref, k_ref, v_ref, xbc_ref, gates_ref):
    u = _rmsnorm(h1_ref[...], g_ref[2:3, :]).astype(BF16)
    q_ref[...] = _dot(u, wq_ref[...]) * (HEAD_DIM ** -0.5)
    k_ref[...] = _dot(u, wk_ref[...])
    v_ref[...] = _dot(u, wv_ref[...])
    xbc_ref[...] = _dot(u, wxbc_ref[...])
    gates_ref[...] = _dot(u, wg_ref[...])


def _inproj_call(h1, norm_g, wq, wk, wv, wxbc, wg):
    n = h1.shape[0]
    ins = (h1, norm_g, wq, wk, wv, wxbc, wg)
    widths = (D_ATT, D_ATT, D_ATT, 3 * D_CONV, 2 * D_MODEL)
    return pl.pallas_call(
        _inproj_kernel,
        grid=(1,),
        in_specs=[_resident(a.shape) for a in ins],
        out_specs=[pl.BlockSpec((n, w), lambda i: (0, 0)) for w in widths],
        out_shape=[jax.ShapeDtypeStruct((n, w), F32) for w in widths],
        compiler_params=_params(1),
        name="inproj_sample",
    )(*ins)


def _attn_sample_kernel(tab_ref, q_ref, k_ref, v_ref, ck_ref, cv_ref, xbc_ref, cconv_ref, convw_ref,
                        o_ref, y_ref, cout_ref, bias_c_scr, bias_n_scr, conv_scr, *, t, lc):
    @pl.when(pl.program_id(0) == 0)
    def _():
        r = lax.broadcasted_iota(jnp.int32, (t, lc), 0)
        c = lax.broadcasted_iota(jnp.int32, (t, lc), 1)
        d_c = jnp.minimum(lc + r - c, MAX_REL)
        r = lax.broadcasted_iota(jnp.int32, (t, t), 0)
        c = lax.broadcasted_iota(jnp.int32, (t, t), 1)
        d_n = jnp.clip(r - c, -MAX_REL, MAX_REL)

        def per_head(h, carry):
            bias_c_scr[h] = _build_bias(tab_ref, h, d_c, 1, MAX_REL)
            bias_n_scr[h] = _build_bias(tab_ref, h, d_n, -min(t - 1, MAX_REL), min(t - 1, MAX_REL))
            return carry
        lax.fori_loop(0, N_HEADS, per_head, 0)

    rows = N_HEADS * t
    row_head = lax.broadcasted_iota(jnp.int32, (N_HEADS, t, D_ATT), 0).reshape(rows, D_ATT)
    col = lax.broadcasted_iota(jnp.int32, (rows, D_ATT), 1)
    own_head = (col >= row_head * HEAD_DIM) & (col < (row_head + 1) * HEAD_DIM)
    bias_c = bias_c_scr[...].reshape(rows, lc)
    bias_n = bias_n_scr[...].reshape(rows, t)
    conv_scr[0:CONV_PAD, :] = jnp.zeros((CONV_PAD, D_CONV), F32)

    for sq in range(q_ref.shape[0] // t):
        rs = slice(sq * t, (sq + 1) * t)
        q_heads = jnp.where(own_head, jnp.concatenate([q_ref[rs, :]] * N_HEADS, axis=0),
                            0.0).astype(BF16)
        k_new = k_ref[rs, :].astype(BF16)
        v_new = v_ref[rs, :].astype(BF16)
        s_c = _dot(q_heads, ck_ref[sq].astype(BF16)) + bias_c
        s_n = _dot_nt(q_heads, k_new) + bias_n
        m = jnp.maximum(jnp.max(s_c, axis=-1, keepdims=True),
                        jnp.max(s_n, axis=-1, keepdims=True))
        p_c = jnp.exp(s_c - m)
        p_n = jnp.exp(s_n - m)
        l = jnp.sum(p_c, axis=-1, keepdims=True) + jnp.sum(p_n, axis=-1, keepdims=True)
        o_all = (_dot_nt(p_c.astype(BF16), cv_ref[sq].astype(BF16))
                 + _dot(p_n.astype(BF16), v_new)) * (1.0 / l)
        o_all = jnp.where(own_head, o_all, 0.0)
        o = o_all[0:t]
        for h in range(1, N_HEADS):
            o = o + o_all[h * t:(h + 1) * t]
        o_ref[rs, :] = o

        xbc = xbc_ref[rs, :]
        s_conv = xbc[:, 2 * D_CONV:] * xbc[:, :D_CONV]
        conv_scr[CONV_PAD - 2:CONV_PAD, :] = cconv_ref[sq]
        y_ref[rs, :] = _short_conv(s_conv, xbc[:, D_CONV:2 * D_CONV], convw_ref, conv_scr)
        cout_ref[sq] = conv_scr[CONV_PAD + t - 2:CONV_PAD + t, :]


def _attn_sample_call(tab, q, k, v, cache_k, cache_v, xbc, cache_conv, convw, batch, t):
    lc = cache_k.shape[1]
    n_seq = SAMPLE_SEQS_PER_STEP if batch % SAMPLE_SEQS_PER_STEP == 0 else 1
    row = lambda b: (b, 0)
    per_seq = lambda b: (b, 0, 0)
    cache_k = cache_k.transpose(0, 2, 3, 1).reshape(batch, D_ATT, lc)
    cache_v = cache_v.transpose(0, 2, 3, 1).reshape(batch, D_ATT, lc)
    cache_spec = pl.BlockSpec((n_seq, D_ATT, lc), per_seq)
    return pl.pallas_call(
        functools.partial(_attn_sample_kernel, t=t, lc=lc),
        grid=(batch // n_seq,),
        in_specs=[pl.BlockSpec(memory_space=pltpu.SMEM),
                  pl.BlockSpec((n_seq * t, D_ATT), row), pl.BlockSpec((n_seq * t, D_ATT), row),
                  pl.BlockSpec((n_seq * t, D_ATT), row),
                  cache_spec, cache_spec,
                  pl.BlockSpec((n_seq * t, 3 * D_CONV), row),
                  pl.BlockSpec((n_seq, CONV_WIDTH - 1, D_CONV), per_seq),
                  _resident(convw.shape)],
        out_specs=[pl.BlockSpec((n_seq * t, D_ATT), row), pl.BlockSpec((n_seq * t, D_CONV), row),
                   pl.BlockSpec((n_seq, CONV_WIDTH - 1, D_CONV), per_seq)],
        out_shape=[jax.ShapeDtypeStruct((batch * t, D_ATT), F32),
                   jax.ShapeDtypeStruct((batch * t, D_CONV), F32),
                   jax.ShapeDtypeStruct((batch, CONV_WIDTH - 1, D_CONV), F32)],
        scratch_shapes=[pltpu.VMEM((N_HEADS, t, lc), F32),
                        pltpu.VMEM((N_HEADS, t, t), F32),
                        pltpu.VMEM((CONV_PAD + t, D_CONV), F32)],
        compiler_params=_params(1),
        name="attn_sample",
    )(tab, q, k, v, cache_k, cache_v, xbc, cache_conv, convw)


def _mix_kernel(h1_ref, o_ref, y_ref, gates_ref, g_ref, wao_ref, wco_ref, wo_ref, h2_ref):
    h2_ref[...] = _mix_out(h1_ref[...], o_ref[...], y_ref[...], gates_ref[...], g_ref[3:4, :],
                           wao_ref, wco_ref, wo_ref)


def _mix_call(h1, o_att, y_conv, gates, norm_g, wao, wco, wo):
    ins = (h1, o_att, y_conv, gates, norm_g, wao, wco, wo)
    return pl.pallas_call(
        _mix_kernel,
        grid=(1,),
        in_specs=[_resident(a.shape) for a in ins],
        out_specs=pl.BlockSpec(h1.shape, lambda i: (0, 0)),
        out_shape=jax.ShapeDtypeStruct(h1.shape, F32),
        compiler_params=_params(1),
        name="mix_sample",
    )(*ins)


def kernel(x_prompt, x_sample, cache_k, cache_v, cache_conv, p_prompt, p_sample, norm_g,
           w1_gate, w1_up, w1_down, w_in, conv_w, rel_bias, w_att_out, w_conv_out, w_out,
           w2_gate, w2_up, w2_down, w_ple_gate, w_ple_proj):
    depth = norm_g.shape[0]
    assert depth == 1, "one layer per step"
    batch, seq, _ = x_prompt.shape
    dec_batch, dec_seq, _ = x_sample.shape
    assert seq % TOKEN_TILE == 0 and TOKEN_TILE % PAIR == 0 and TOKEN_TILE >= ATT_WINDOW
    l = 0

    g = norm_g[l]
    w1g = w1_gate[l].astype(BF16)
    w1u = w1_up[l].astype(BF16)
    w2g = w2_gate[l].astype(BF16)
    w2u = w2_up[l].astype(BF16)
    w1d = w1_down[l].astype(BF16)
    w2d = w2_down[l].astype(BF16)
    win = w_in[l].astype(BF16)
    wq = win[:, :D_ATT]
    wk = win[:, D_ATT:2 * D_ATT]
    wv = win[:, 2 * D_ATT:3 * D_ATT]
    wxbc = win[:, 3 * D_ATT:3 * D_ATT + 3 * D_CONV]
    wg = win[:, 3 * D_ATT + 3 * D_CONV:]
    wkt = wk.T
    wao = w_att_out[l].astype(BF16)
    wco = w_conv_out[l].astype(BF16)
    wo = w_out[l].astype(BF16)
    wpg = w_ple_gate[l].astype(BF16)
    wpp = w_ple_proj[l].astype(BF16)
    tab = rel_bias[l]
    convw = conv_w[l]

    xp = x_prompt.reshape(batch * seq, D_MODEL)
    h1p = _ffn_call(xp, g, w1g, w1u, w1d, 0, 1)
    h2p, kt_p, v_p, c_p = _mixer_prompt_call(tab, h1p, g, wq, wkt, wv, wxbc, wg, convw,
                                            wao, wco, wo, batch, seq)
    y_p = _ffn_ple_call(h2p, p_prompt[l].reshape(batch * seq, D_PLE), g, w2g, w2u, w2d, wpg, wpp,
                        4, 5, 6)

    lc = cache_k.shape[2]
    xs = x_sample.reshape(dec_batch * dec_seq, D_MODEL)
    h1s = _ffn_call(xs, g, w1g, w1u, w1d, 0, 1)
    q_s, k_s, v_s, xbc_s, gates_s = _inproj_call(h1s, g, wq, wk, wv, wxbc, wg)
    o_s, yc_s, c_s = _attn_sample_call(tab, q_s, k_s, v_s,
                                       cache_k[l], cache_v[l],
                                       xbc_s, cache_conv[l], convw, dec_batch, dec_seq)
    h2s = _mix_call(h1s, o_s, yc_s, gates_s, g, wao, wco, wo)
    y_s = _ffn_ple_call(h2s, p_sample[l].reshape(dec_batch * dec_seq, D_PLE), g, w2g, w2u, w2d,
                        wpg, wpp, 4, 5, 6)

    kv_p = (1, batch, ATT_WINDOW, N_HEADS, HEAD_DIM)
    kv_s = (1, dec_batch, dec_seq, N_HEADS, HEAD_DIM)
    k_p = kt_p.reshape(batch, N_HEADS, HEAD_DIM, ATT_WINDOW).transpose(0, 3, 1, 2)
    return (y_p.reshape(batch, seq, D_MODEL), y_s.reshape(dec_batch, dec_seq, D_MODEL),
            k_p.reshape(kv_p), v_p.reshape(kv_p), c_p[None],
            k_s.reshape(kv_s), v_s.reshape(kv_s), c_s[None])
```
